```python
import math
import jax, jax.numpy as jnp
from jax import lax
import numpy as np

D_MODEL = 4096
BATCH = 2
SEQ = 4096
DEPTH = 2
DEC_BATCH = 8
DEC_SEQ = 2048
PAST_LEN = 128

MIX_WIDTH = D_MODEL
ATT_WIDTH = MIX_WIDTH // 2
RWKV_WIDTH = MIX_WIDTH - ATT_WIDTH
HEAD_DIM = 128
N_Q_HEADS = ATT_WIDTH // HEAD_DIM
N_KV_HEADS = N_Q_HEADS // 4
KV_WIDTH = N_KV_HEADS * HEAD_DIM
WINDOW = 128
BLOCK = 128
N_REL_BUCKETS = 32
REL_MAX_DIST = 128
RWKV_HEAD = 64
N_RWKV_HEADS = RWKV_WIDTH // RWKV_HEAD
DECAY_LORA = max(32, int(round(1.8 * RWKV_WIDTH ** 0.5 / 32)) * 32)
AAA_LORA = max(32, int(round(1.8 * RWKV_WIDTH ** 0.5 / 32)) * 32)
GATE_LORA = max(32, int(round(0.6 * RWKV_WIDTH ** 0.8 / 32)) * 32)
LNX_EPS = 64e-5
ATT_COLS = ATT_WIDTH + 2 * KV_WIDTH
SHIFT_WIDTH = 3 * RWKV_WIDTH + 2 * DECAY_LORA + 2 * AAA_LORA + GATE_LORA
IN_WIDTH = ATT_COLS + SHIFT_WIDTH
N_MEM = 256
MEM_HEADS = 4
MEM_HEAD_DIM = 128
MEM_WIDTH = MEM_HEADS * MEM_HEAD_DIM
D_FF = -(-8 * D_MODEL // (3 * 256)) * 256
RMS_EPS = 1e-6
NEG = -1e30

kernel_name = "hymba_window_gqa_rwkv7_bidir_encoder"


def _rms(x, g):
    x32 = x.astype(jnp.float32)
    y = x32 * lax.rsqrt(jnp.mean(x32 * x32, axis=-1, keepdims=True) + RMS_EPS)
    return (y * g.astype(jnp.float32)).astype(x.dtype)


def _rel_bucket(rel):
    half = N_REL_BUCKETS // 2
    exact = half // 2
    n = np.abs(rel)
    large = exact + (np.log(np.maximum(n, 1) / exact) / np.log(REL_MAX_DIST / exact)
                     * (half - exact)).astype(np.int32)
    large = np.minimum(large, half - 1)
    return (rel > 0).astype(np.int32) * half + np.where(n < exact, n, large)


def _window_attention(q, k, v, rel_bias, sink):
    B, T = q.shape[:2]
    nb = T // BLOCK
    G = N_Q_HEADS // N_KV_HEADS
    qb = q.reshape(B, nb, BLOCK, N_KV_HEADS, G, HEAD_DIM)

    def windows(t):
        tp = jnp.pad(t, ((0, 0), (BLOCK, BLOCK), (0, 0), (0, 0)))
        tp = tp.reshape(B, nb + 2, BLOCK, N_KV_HEADS, HEAD_DIM)
        return jnp.concatenate([tp[:, :-2], tp[:, 1:-1], tp[:, 2:]], axis=2)

    kw, vw = windows(k), windows(v)
    s = jnp.einsum('bnqkgd,bnskd->bnkgqs', qb, kw).astype(jnp.float32) * (HEAD_DIM ** -0.5)
    qi = np.arange(BLOCK)[:, None]
    kj = np.arange(3 * BLOCK)[None, :]
    rel = kj - BLOCK - qi
    bias = rel_bias.astype(jnp.float32)[_rel_bucket(rel)]
    bias = bias.transpose(2, 0, 1).reshape(N_KV_HEADS, G, BLOCK, 3 * BLOCK)
    kpos = np.arange(nb)[:, None] * BLOCK + np.arange(3 * BLOCK)[None, :] - BLOCK
    valid = (kpos >= 0) & (kpos < T)
    mask = (np.abs(rel) <= WINDOW)[None] & valid[:, None, :]
    s = jnp.where(mask[None, :, None, None], s + bias, NEG)
    sink_l = sink.astype(jnp.float32).reshape(N_KV_HEADS, G, 1, 1)
    m = jnp.maximum(jnp.max(s, axis=-1, keepdims=True), sink_l)
    p = jnp.exp(s - m)
    p = p / (jnp.sum(p, axis=-1, keepdims=True) + jnp.exp(sink_l - m))
    o = jnp.einsum('bnkgqs,bnskd->bnqkgd', p.astype(v.dtype), vw)
    return o.reshape(B, T, ATT_WIDTH)


def _rwkv_step(S, inp):
    r_t, w_t, k_t, v_t, kk_t, b_t = inp
    sa = jnp.einsum('zbhij,zbhj->zbhi', S, -kk_t)
    S = S * w_t[..., None, :] + sa[..., :, None] * b_t[..., None, :] + v_t[..., :, None] * k_t[..., None, :]
    y = jnp.einsum('zbhij,zbhj->zbhi', S, r_t)
    return S, y


def _rwkv_mixer(p, shift_prev, shift_next, w0, w2, a0, a2, g2, k_k, k_a, r_k, lnx_w, lnx_b):
    B, T, _ = p.shape
    H, N, C = N_RWKV_HEADS, RWKV_HEAD, RWKV_WIDTH
    prev = jnp.pad(p, ((0, 0), (1, 0), (0, 0)))[:, :-1]
    nxt = jnp.pad(p, ((0, 0), (0, 1), (0, 0)))[:, 1:]
    p = (p + shift_prev * (prev - p) + shift_next * (nxt - p)).astype(jnp.float32)
    cuts = [C, 2 * C, 3 * C, 3 * C + 2 * DECAY_LORA, 3 * C + 2 * DECAY_LORA + 2 * AAA_LORA]
    r, k, v, wd, ad, gd = jnp.split(p, cuts, axis=-1)
    w_raw = w0 + jnp.einsum('btzl,zlc->btzc', jnp.tanh(wd.reshape(B, T, 2, DECAY_LORA)), w2)
    decay = jnp.exp(-jnp.exp(-jax.nn.softplus(-w_raw) - 0.5))
    a = jax.nn.sigmoid(a0 + jnp.einsum('btzl,zlc->btzc', ad.reshape(B, T, 2, AAA_LORA), a2))
    g = jax.nn.sigmoid(gd) @ g2
    kk = (k * k_k).reshape(B, T, H, N)
    kk = (kk / jnp.maximum(jnp.sqrt(jnp.sum(kk * kk, axis=-1, keepdims=True)), 1e-12)).reshape(B, T, C)
    kd = k[:, :, None] * (1.0 + (a - 1.0) * k_a)
    b = kk[:, :, None] * a

    def shared(t):
        s = jnp.stack([t, t[:, ::-1]], axis=0)
        return s.reshape(2, B, T, H, N).transpose(2, 0, 1, 3, 4)

    def split(t):
        s = jnp.stack([t[:, :, 0], t[:, ::-1, 1]], axis=0)
        return s.reshape(2, B, T, H, N).transpose(2, 0, 1, 3, 4)

    xs = (shared(r), split(decay), split(kd), shared(v), shared(kk), split(b))
    S0 = jnp.zeros((2, B, H, N, N), jnp.float32)
    _, ys = lax.scan(_rwkv_step, S0, xs)
    ys = ys.transpose(1, 2, 0, 3, 4)
    y = ys[0] + ys[1][:, ::-1]
    mu = jnp.mean(y, axis=-1, keepdims=True)
    var = jnp.mean((y - mu) ** 2, axis=-1, keepdims=True)
    y = ((y - mu) * lax.rsqrt(var + LNX_EPS)).reshape(B, T, C) * lnx_w + lnx_b
    bonus = jnp.einsum('bthn,btzhn,hn->bth', r.reshape(B, T, H, N), kd.reshape(B, T, 2, H, N), r_k)
    bonus = (bonus[..., None] * v.reshape(B, T, H, N)).reshape(B, T, C)
    return (y + bonus) * g


def _memory_attention(h, mem, wq, wk, wv, wo, qn, kn):
    B, T, _ = h.shape
    q = _rms((h @ wq).reshape(B, T, MEM_HEADS, MEM_HEAD_DIM), qn)
    k = _rms((mem @ wk).reshape(B, N_MEM, MEM_HEADS, MEM_HEAD_DIM), kn)
    v = (mem @ wv).reshape(B, N_MEM, MEM_HEADS, MEM_HEAD_DIM)
    s = jnp.einsum('bthd,bmhd->bhtm', q, k).astype(jnp.float32) * (MEM_HEAD_DIM ** -0.5)
    p = jax.nn.softmax(s, axis=-1)
    o = jnp.einsum('bhtm,bmhd->bthd', p.astype(v.dtype), v).reshape(B, T, MEM_WIDTH)
    return o @ wo


def _trunk(x, mem, P):
    B, T, _ = x.shape
    for l in range(DEPTH):
        h = _rms(x, P['norm_mix'][l])
        proj = h @ P['w_in'][l]
        q = _rms(proj[..., :ATT_WIDTH].reshape(B, T, N_Q_HEADS, HEAD_DIM), P['q_norm'][l])
        k = _rms(proj[..., ATT_WIDTH:ATT_WIDTH + KV_WIDTH].reshape(B, T, N_KV_HEADS, HEAD_DIM), P['k_norm'][l])
        v = proj[..., ATT_WIDTH + KV_WIDTH:ATT_COLS].reshape(B, T, N_KV_HEADS, HEAD_DIM)
        att = _window_attention(q, k, v, P['rel_bias'], P['sink'][l])
        rw = _rwkv_mixer(proj[..., ATT_COLS:], P['shift_prev'][l], P['shift_next'][l], P['w0'][l], P['w2'][l],
                         P['a0'][l], P['a2'][l], P['g2'][l], P['k_k'][l], P['k_a'][l], P['r_k'][l],
                         P['lnx_w'][l], P['lnx_b'][l]).astype(x.dtype)
        x = x + jnp.concatenate([att, rw], axis=-1) @ P['w_out'][l]
        h = _rms(x, P['norm_mem'][l])
        m = _rms(mem, P['norm_memkv'][l])
        x = x + _memory_attention(h, m, P['wq_mem'][l], P['wk_mem'][l], P['wv_mem'][l], P['wo_mem'][l],
                                  P['qn_mem'][l], P['kn_mem'][l])
        h = _rms(x, P['norm_ffn'][l])
        x = x + (jax.nn.silu(h @ P['w_gate'][l]) * (h @ P['w_up'][l])) @ P['w_down'][l]
    return x


def setup_inputs(seed: int = 0) -> dict:
    key = jax.random.key(seed)
    ks = iter(jax.random.split(key, 48))
    L = DEPTH

    def nrm(shape, scale):
        return jax.random.normal(next(ks), shape, jnp.float32) * scale

    def gain(shape):
        return 1.0 + 0.05 * jax.random.normal(next(ks), shape, jnp.float32)

    def uni(shape, lo, hi):
        return jax.random.uniform(next(ks), shape, jnp.float32, lo, hi)

    return {
        "x_prompt": nrm((BATCH, SEQ, D_MODEL), 1.0),
        "x_sample": nrm((DEC_BATCH, DEC_SEQ, D_MODEL), 1.0),
        "mem_prompt": nrm((BATCH, N_MEM, D_MODEL), 1.0),
        "mem_sample": nrm((DEC_BATCH, N_MEM, D_MODEL), 1.0),
        "rel_bias": nrm((N_REL_BUCKETS, N_Q_HEADS), 0.5),
        "norm_mix": gain((L, D_MODEL)),
        "w_in": nrm((L, D_MODEL, IN_WIDTH), D_MODEL ** -0.5),
        "q_norm": gain((L, HEAD_DIM)),
        "k_norm": gain((L, HEAD_DIM)),
        "sink": nrm((L, N_Q_HEADS), 0.5),
        "shift_prev": uni((L, SHIFT_WIDTH), 0.0, 0.5),
        "shift_next": uni((L, SHIFT_WIDTH), 0.0, 0.5),
        "w0": -0.5 + nrm((L, 2, RWKV_WIDTH), 0.5),
        "w2": nrm((L, 2, DECAY_LORA, RWKV_WIDTH), 0.3 * DECAY_LORA ** -0.5),
        "a0": nrm((L, 2, RWKV_WIDTH), 0.5),
        "a2": nrm((L, 2, AAA_LORA, RWKV_WIDTH), 0.3 * AAA_LORA ** -0.5),
        "g2": nrm((L, GATE_LORA, RWKV_WIDTH), GATE_LORA ** -0.5),
        "k_k": 0.85 + nrm((L, RWKV_WIDTH), 0.05),
        "k_a": gain((L, RWKV_WIDTH)),
        "r_k": nrm((L, N_RWKV_HEADS, RWKV_HEAD), 0.1),
        "lnx_w": gain((L, RWKV_WIDTH)),
        "lnx_b": nrm((L, RWKV_WIDTH), 0.02),
        "w_out": nrm((L, MIX_WIDTH, D_MODEL), MIX_WIDTH ** -0.5),
        "norm_mem": gain((L, D_MODEL)),
        "norm_memkv": gain((L, D_MODEL)),
        "wq_mem": nrm((L, D_MODEL, MEM_WIDTH), D_MODEL ** -0.5),
        "wk_mem": nrm((L, D_MODEL, MEM_WIDTH), D_MODEL ** -0.5),
        "wv_mem": nrm((L, D_MODEL, MEM_WIDTH), D_MODEL ** -0.5),
        "wo_mem": nrm((L, MEM_WIDTH, D_MODEL), MEM_WIDTH ** -0.5),
        "qn_mem": gain((L, MEM_HEAD_DIM)),
        "kn_mem": gain((L, MEM_HEAD_DIM)),
        "norm_ffn": gain((L, D_MODEL)),
        "w_gate": nrm((L, D_MODEL, D_FF), D_MODEL ** -0.5),
        "w_up": nrm((L, D_MODEL, D_FF), D_MODEL ** -0.5),
        "w_down": nrm((L, D_FF, D_MODEL), D_FF ** -0.5),
    }


def reference(x_prompt, x_sample, mem_prompt, mem_sample, rel_bias, norm_mix, w_in, q_norm, k_norm, sink,
              shift_prev, shift_next, w0, w2, a0, a2, g2, k_k, k_a, r_k, lnx_w, lnx_b, w_out,
              norm_mem, norm_memkv, wq_mem, wk_mem, wv_mem, wo_mem, qn_mem, kn_mem,
              norm_ffn, w_gate, w_up, w_down):
    P = dict(rel_bias=rel_bias, norm_mix=norm_mix, w_in=w_in, q_norm=q_norm, k_norm=k_norm, sink=sink,
             shift_prev=shift_prev, shift_next=shift_next, w0=w0, w2=w2, a0=a0, a2=a2, g2=g2,
             k_k=k_k, k_a=k_a, r_k=r_k, lnx_w=lnx_w, lnx_b=lnx_b, w_out=w_out,
             norm_mem=norm_mem, norm_memkv=norm_memkv, wq_mem=wq_mem, wk_mem=wk_mem, wv_mem=wv_mem,
             wo_mem=wo_mem, qn_mem=qn_mem, kn_mem=kn_mem, norm_ffn=norm_ffn,
             w_gate=w_gate, w_up=w_up, w_down=w_down)
    y_prompt = _trunk(x_prompt, mem_prompt, P)
    y_sample = _trunk(x_sample, mem_sample, P)
    return (y_prompt, y_sample)
```

```python
import functools
import math

import numpy as np
import jax
import jax.numpy as jnp
from jax import lax
from jax.experimental import pallas as pl
from jax.experimental.pallas import tpu as pltpu

F32 = jnp.float32
BF = jnp.bfloat16

HEAD_DIM = 128
N_Q_HEADS = 16
N_KV_HEADS = 4
GQA = N_Q_HEADS // N_KV_HEADS
BLOCK = 128
N_REL_BUCKETS = 32
REL_MAX_DIST = 128
RWKV_HEAD = 64
DECAY_LORA = 96
AAA_LORA = 96
GATE_LORA = 256
LNX_EPS = 64e-5
MEM_HEADS = 4
MEM_HEAD_DIM = 128
N_MEM = 256
RMS_EPS = 1e-6
NEG = -1e30

LANE = 128
CHUNK = 64
LORA_W = 2 * DECAY_LORA + 2 * AAA_LORA
XTRA_W = 768
VMEM_LIMIT = 56 * 1024 * 1024


def _cp(*sem):
    return pltpu.CompilerParams(dimension_semantics=sem, vmem_limit_bytes=VMEM_LIMIT)


def _dot(a, b):
    return jnp.dot(a, b, preferred_element_type=F32)


def _dot_nt(a, b):
    return lax.dot_general(a, b, (((1,), (1,)), ((), ())), preferred_element_type=F32)


def _dot_tn(a, b):
    return lax.dot_general(a, b, (((0,), (0,)), ((), ())), preferred_element_type=F32)


def _split2(x):
    hi = x.astype(BF)
    lo = (x - hi.astype(F32)).astype(BF)
    return hi, lo


def _rms_rows(x, g):
    ms = jnp.mean(x * x, axis=-1, keepdims=True)
    return x * lax.rsqrt(ms + RMS_EPS) * g


def _seq_info(row, segs):
    off = 0
    pos = None
    tlen = None
    for n, t in segs:
        p = lax.rem(row - off, t)
        if pos is None:
            pos, tlen = p, jnp.int32(t)
        else:
            inside = row >= off
            pos = jnp.where(inside, p, pos)
            tlen = jnp.where(inside, t, tlen)
        off += n * t
    return pos, tlen


def _batch_of(row, segs):
    off = 0
    boff = 0
    res = None
    for n, t in segs:
        b = boff + (row - off) // t
        res = b if res is None else jnp.where(row >= off, b, res)
        off += n * t
        boff += n
    return res


def _inproj_kernel(x_ref, g_ref, w_ref, o_ref, h_ref):
    @pl.when(pl.program_id(1) == 0)
    def _():
        h_ref[...] = _rms_rows(x_ref[...], g_ref[...]).astype(BF)

    o_ref[...] = _dot(h_ref[...], w_ref[...])


def _inproj(x, g, w, tm, tn):
    m, d = x.shape
    n = w.shape[1]
    return pl.pallas_call(
        _inproj_kernel,
        grid=(m // tm, n // tn),
        in_specs=[pl.BlockSpec((tm, d), lambda i, j: (i, 0)),
                  pl.BlockSpec((1, d), lambda i, j: (0, 0)),
                  pl.BlockSpec((d, tn), lambda i, j: (0, j))],
        out_specs=pl.BlockSpec((tm, tn), lambda i, j: (i, j)),
        out_shape=jax.ShapeDtypeStruct((m, n), F32),
        scratch_shapes=[pltpu.VMEM((tm, d), BF)],
        compiler_params=_cp("parallel", "arbitrary"),
        name="in_proj",
    )(x, g, w)


def _attn_kernel(segs, q_ref, kp_ref, kc_ref, kn_ref, vp_ref, vc_ref, vn_ref,
                 bias_ref, qn_ref, kn_g_ref, sink_ref, o_ref):
    n = pl.program_id(0)
    pos, tlen = _seq_info(n * BLOCK, segs)
    first = pos == 0
    last = pos + BLOCK == tlen
    col = lax.broadcasted_iota(jnp.int32, (BLOCK, 3 * BLOCK), 1)
    dead = (first & (col < BLOCK)) | (last & (col >= 2 * BLOCK))
    scale = HEAD_DIM ** -0.5
    for kh in range(N_KV_HEADS):
        ls = slice(kh * HEAD_DIM, (kh + 1) * HEAD_DIM)
        kw = jnp.concatenate([kp_ref[:, ls], kc_ref[:, ls], kn_ref[:, ls]], axis=0)
        kw = _rms_rows(kw, kn_g_ref[...]).astype(BF)
        vw = jnp.concatenate([vp_ref[:, ls], vc_ref[:, ls], vn_ref[:, ls]], axis=0).astype(BF)
        for g in range(GQA):
            h = kh * GQA + g
            hs = slice(h * HEAD_DIM, (h + 1) * HEAD_DIM)
            q = _rms_rows(q_ref[:, hs], qn_ref[...]).astype(BF)
            s = _dot_nt(q, kw) * scale + bias_ref[h]
            s = jnp.where(dead, NEG, s)
            sk = sink_ref[h]
            mx = jnp.maximum(jnp.max(s, axis=-1, keepdims=True), sk)
            p = jnp.exp(s - mx)
            den = jnp.sum(p, axis=-1, keepdims=True) + jnp.exp(sk - mx)
            o = _dot(p.astype(BF), vw) / den
            o_ref[:, hs] = o.astype(o_ref.dtype)


def _attention(proj, bias, qn, kn, sink, segs):
    m = proj.shape[0]
    nb = m // BLOCK
    aw = N_Q_HEADS * HEAD_DIM
    kvw = N_KV_HEADS * HEAD_DIM
    kcol = aw // kvw
    vcol = kcol + 1
    prev = lambda n: jnp.maximum(n - 1, 0)
    nxt = lambda n: jnp.minimum(n + 1, nb - 1)
    specs = [pl.BlockSpec((BLOCK, aw), lambda n: (n, 0))]
    for c in (kcol, vcol):
        specs += [pl.BlockSpec((BLOCK, kvw), lambda n, c=c: (prev(n), c)),
                  pl.BlockSpec((BLOCK, kvw), lambda n, c=c: (n, c)),
                  pl.BlockSpec((BLOCK, kvw), lambda n, c=c: (nxt(n), c))]
    specs += [pl.BlockSpec((N_Q_HEADS, BLOCK, 3 * BLOCK), lambda n: (0, 0, 0)),
              pl.BlockSpec((1, HEAD_DIM), lambda n: (0, 0)),
              pl.BlockSpec((1, HEAD_DIM), lambda n: (0, 0)),
              pl.BlockSpec(memory_space=pltpu.SMEM)]
    return pl.pallas_call(
        functools.partial(_attn_kernel, segs),
        grid=(nb,),
        in_specs=specs,
        out_specs=pl.BlockSpec((BLOCK, aw), lambda n: (n, 0)),
        out_shape=jax.ShapeDtypeStruct((m, aw), BF),
        compiler_params=_cp("parallel"),
        name="window_attn",
    )(proj, proj, proj, proj, proj, proj, proj, bias, qn, kn, sink)


def _prep_kernel(segs, tt, *refs):
    (pr, pr_p, pr_n, pk, pk_p, pk_n, pv, pv_p, pv_n, px, px_p, px_n,
     spr, snr, spk, snk, spv, snv, spx, snx, w0, a0, wl, g2, kk_g, ka_g, rk_g, bd, tri) = refs[:29]
    outs = refs[29:]
    zouts = (outs[0:7], outs[7:14])
    vo, bvo, go = outs[14:17]

    row0 = pl.program_id(0) * tt
    pos, tlen = _seq_info(row0, segs)
    first = pos == 0
    last = pos + tt == tlen
    rows = lax.broadcasted_iota(jnp.int32, (tt, 1), 0)

    def shifted(x_ref, p_ref, n_ref, sp, sn):
        x = x_ref[...]
        prow = jnp.where(first, 0.0, p_ref[7:8, :])
        nrow = jnp.where(last, 0.0, n_ref[0:1, :])
        prev = jnp.where(rows == 0, prow, pltpu.roll(x, 1, 0))
        nxt = jnp.where(rows == tt - 1, nrow, pltpu.roll(x, tt - 1, 0))
        return x + sp[...] * (prev - x) + sn[...] * (nxt - x)

    r = shifted(pr, pr_p, pr_n, spr, snr)
    k = shifted(pk, pk_p, pk_n, spk, snk)
    v = shifted(pv, pv_p, pv_n, spv, snv)
    x = shifted(px, px_p, px_n, spx, snx)

    xl = x[:, :LORA_W]
    lane = lax.broadcasted_iota(jnp.int32, xl.shape, 1)
    lx = jnp.where(lane < 2 * DECAY_LORA, jnp.tanh(xl), xl).astype(BF)
    sg = jax.nn.sigmoid(x[:, LORA_W:LORA_W + GATE_LORA]).astype(BF)
    go[...] = _dot(sg, g2[...]).astype(go.dtype)

    kkr = k * kk_g[...]
    n2 = _dot((kkr * kkr).astype(BF), bd[...])
    kk = kkr / jnp.maximum(jnp.sqrt(n2), 1e-12)

    kd_sum = None
    for z in (0, 1):
        at_o, rt_o, bt_o, kt_o, bg_o, kg_o, ct_o = zouts[z]
        wraw = w0[z:z + 1, :] + _dot(lx, wl[z])
        a = jax.nn.sigmoid(a0[z:z + 1, :] + _dot(lx, wl[2 + z]))
        lw = -math.exp(-0.5) * jax.nn.sigmoid(wraw)
        hi, lo = _split2(lw)
        cin = _dot(tri[2 * z], hi) + _dot(tri[2 * z], lo)
        suf = _dot(tri[2 * z + 1], hi) + _dot(tri[2 * z + 1], lo)
        kd = k * (1.0 + (a - 1.0) * ka_g[...])
        b = kk * a
        e_neg = jnp.exp(-cin)
        e_suf = jnp.exp(suf)
        at_o[...] = (-kk * jnp.exp(cin - lw)).astype(BF)
        rt_o[...] = (r * jnp.exp(cin)).astype(BF)
        bt_o[...] = (b * e_neg).astype(BF)
        kt_o[...] = (kd * e_neg).astype(BF)
        bg_o[...] = (b * e_suf).astype(BF)
        kg_o[...] = (kd * e_suf).astype(BF)
        ctot = cin + suf
        ct_o[...] = jnp.concatenate([ctot[c * CHUNK:c * CHUNK + 8] for c in range(tt // CHUNK)], axis=0)
        kd_sum = kd if kd_sum is None else kd_sum + kd

    bonus = _dot((r * kd_sum * rk_g[...]).astype(BF), bd[...])
    bvo[...] = (bonus * v).astype(bvo.dtype)
    vo[...] = v.astype(BF)


def _prep(proj, p, segs, tt, cb):
    m = proj.shape[0]
    c = p["k_k"].shape[1]
    rcol0 = (N_Q_HEADS + 2 * N_KV_HEADS) * HEAD_DIM
    nrow8 = m // 8
    t8 = tt // 8

    def trio(width, colfn):
        return [pl.BlockSpec((tt, width), lambda i, j: (i, colfn(j))),
                pl.BlockSpec((8, width), lambda i, j: (jnp.maximum(i * t8 - 1, 0), colfn(j))),
                pl.BlockSpec((8, width), lambda i, j: (jnp.minimum((i + 1) * t8, nrow8 - 1), colfn(j)))]

    specs = []
    for sec in range(3):
        base = (rcol0 + sec * c) // cb
        specs += trio(cb, lambda j, base=base: base + j)
    xblk = (rcol0 + 3 * c) // XTRA_W
    specs += trio(XTRA_W, lambda j: xblk)
    vec = lambda: pl.BlockSpec((1, cb), lambda i, j: (0, j))
    specs += [vec(), vec(), vec(), vec(), vec(), vec(),
              pl.BlockSpec((1, XTRA_W), lambda i, j: (0, 0)),
              pl.BlockSpec((1, XTRA_W), lambda i, j: (0, 0)),
              pl.BlockSpec((2, cb), lambda i, j: (0, j)),
              pl.BlockSpec((2, cb), lambda i, j: (0, j)),
              pl.BlockSpec((4, LORA_W, cb), lambda i, j: (0, 0, j)),
              pl.BlockSpec((GATE_LORA, cb), lambda i, j: (0, j)),
              vec(), vec(), vec(),
              pl.BlockSpec((cb, cb), lambda i, j: (0, 0)),
              pl.BlockSpec((4, tt, tt), lambda i, j: (0, 0, 0))]
    big = pl.BlockSpec((tt, cb), lambda i, j: (i, j))
    small = pl.BlockSpec((t8, cb), lambda i, j: (i, j))
    out_specs = ([big] * 6 + [small]) * 2 + [big] * 3
    big_s = jax.ShapeDtypeStruct((m, c), BF)
    small_s = jax.ShapeDtypeStruct((nrow8, c), F32)
    out_shape = ([big_s] * 6 + [small_s]) * 2 + [big_s] * 3
    args = [proj] * 12 + [p["sp_r"], p["sn_r"], p["sp_k"], p["sn_k"], p["sp_v"], p["sn_v"],
                          p["sp_x"], p["sn_x"], p["w0"], p["a0"], p["wl"], p["g2"],
                          p["k_k"], p["k_a"], p["r_k"], p["bd"], p["tri"]]
    return pl.pallas_call(
        functools.partial(_prep_kernel, segs, tt),
        grid=(m // tt, c // cb),
        in_specs=specs,
        out_specs=out_specs,
        out_shape=out_shape,
        compiler_params=_cp("parallel", "arbitrary"),
        name="rwkv_prep",
    )(*args)


def _pair_chunk(z, hmat, at, rt, bt, kt, bg, kg, v, ct_row, cst):
    m_lo, m_hi, strict, incl, blk16, off32, off64, eye = cst

    def stack(x):
        return jnp.concatenate([x * m_lo, x * m_hi], axis=0)

    xa, xr, xb, xk, xbg, xkg, vs = (stack(t) for t in (at, rt, bt, kt, bg, kg, v))
    pmat = _dot_nt(jnp.concatenate([xa, xr], axis=0), jnp.concatenate([xb, xk], axis=0))
    n = 2 * CHUNK
    a_ab = jnp.where(strict, pmat[:n, :n], 0.0)
    a_ak = jnp.where(strict, pmat[:n, n:], 0.0).astype(BF)
    a_rb = jnp.where(incl, pmat[n:, :n], 0.0).astype(BF)
    a_rk = jnp.where(incl, pmat[n:, n:], 0.0).astype(BF)

    ad = jnp.where(blk16, a_ab, 0.0)
    adb = ad.astype(BF)
    a2 = _dot(adb, adb).astype(BF)
    a4 = _dot(a2, a2).astype(BF)
    a8 = _dot(a4, a4).astype(BF)
    tinv = eye + ad
    tinv = tinv + _dot(tinv.astype(BF), a2)
    tinv = tinv + _dot(tinv.astype(BF), a4)
    tinv = tinv + _dot(tinv.astype(BF), a8)
    for off in (off32, off64):
        tb = tinv.astype(BF)
        ao = jnp.where(off, a_ab, 0.0).astype(BF)
        tinv = tinv + _dot(_dot(tb, ao).astype(BF), tb)

    hb = hmat.astype(BF)
    rhs = _dot(jnp.concatenate([xa, a_ak], axis=1), jnp.concatenate([hb, vs], axis=0))
    ub = _dot(tinv.astype(BF), rhs.astype(BF)).astype(BF)
    yst = _dot(jnp.concatenate([xr, a_rb, a_rk], axis=1), jnp.concatenate([hb, ub, vs], axis=0))
    y = yst[:CHUNK] + yst[CHUNK:]
    upd = _dot_tn(jnp.concatenate([xbg, xkg], axis=0), jnp.concatenate([ub, vs], axis=0))
    gam = jnp.exp(jnp.transpose(jnp.broadcast_to(ct_row, (n, n))))
    return gam * hmat + upd, y


def _scan_kernel(z, segs, ts, npair, at, rt, bt, kt, bg, kg, v, ct, y_ref, h_ref):
    i = pl.program_id(1)
    nblk = pl.num_programs(1)
    blk = i if z == 0 else nblk - 1 - i
    pos, tlen = _seq_info(blk * ts, segs)
    reset = (pos == 0) if z == 0 else (pos + ts == tlen)

    @pl.when(reset)
    def _():
        h_ref[...] = jnp.zeros_like(h_ref)

    n = 2 * CHUNK
    lane = lax.broadcasted_iota(jnp.int32, (CHUNK, LANE), 1)
    m_lo = jnp.where(lane < RWKV_HEAD, 1.0, 0.0).astype(BF)
    m_hi = jnp.where(lane < RWKV_HEAD, 0.0, 1.0).astype(BF)
    row = lax.broadcasted_iota(jnp.int32, (n, n), 0)
    col = lax.broadcasted_iota(jnp.int32, (n, n), 1)
    strict = (row > col) if z == 0 else (row < col)
    incl = (row >= col) if z == 0 else (row <= col)
    blk16 = (row // 16) == (col // 16)
    off32 = ((row // 32) == (col // 32)) & ((row // 16) != (col // 16))
    off64 = ((row // 64) == (col // 64)) & ((row // 32) != (col // 32))
    eye = jnp.where(row == col, 1.0, 0.0).astype(F32)
    cst = (m_lo, m_hi, strict, incl, blk16, off32, off64, eye)

    nch = ts // CHUNK
    order = range(nch) if z == 0 else range(nch - 1, -1, -1)
    for p in range(npair):
        ls = slice(p * LANE, (p + 1) * LANE)
        hmat = h_ref[p]
        for c in order:
            rs = slice(c * CHUNK, (c + 1) * CHUNK)
            hmat, y = _pair_chunk(z, hmat, at[rs, ls], rt[rs, ls], bt[rs, ls], kt[rs, ls],
                                  bg[rs, ls], kg[rs, ls], v[rs, ls], ct[c * 8:c * 8 + 1, ls], cst)
            y_ref[rs, ls] = y
        h_ref[p] = hmat


def _scan(z, ops, v, ct, segs, ts, cs):
    m, c = v.shape
    nblk = m // ts
    npair = cs // LANE
    rowmap = (lambda j, i: (i, j)) if z == 0 else (lambda j, i: (nblk - 1 - i, j))
    big = pl.BlockSpec((ts, cs), rowmap)
    small = pl.BlockSpec((ts // 8, cs), rowmap)
    return pl.pallas_call(
        functools.partial(_scan_kernel, z, segs, ts, npair),
        grid=(c // cs, nblk),
        in_specs=[big] * 7 + [small],
        out_specs=big,
        out_shape=jax.ShapeDtypeStruct((m, c), F32),
        scratch_shapes=[pltpu.VMEM((npair, LANE, LANE), F32)],
        compiler_params=_cp("parallel", "arbitrary"),
        name="rwkv_scan_fwd" if z == 0 else "rwkv_scan_bwd",
    )(*ops, v, ct)


def _post_kernel(y0, y1, bv, g, lw, lb, bd, o_ref):
    y = y0[...] + y1[...]
    inv_n = 1.0 / RWKV_HEAD
    hi, lo = _split2(y)
    mu = (_dot(hi, bd[...]) + _dot(lo, bd[...])) * inv_n
    d = y - mu
    var = _dot((d * d).astype(BF), bd[...]) * inv_n
    yn = d * lax.rsqrt(var + LNX_EPS) * lw[...] + lb[...]
    o_ref[...] = ((yn + bv[...].astype(F32)) * g[...].astype(F32)).astype(o_ref.dtype)


def _post(y0, y1, bv, g, lw, lb, bd, tt, cb):
    m, c = y0.shape
    big = pl.BlockSpec((tt, cb), lambda i, j: (i, j))
    vec = pl.BlockSpec((1, cb), lambda i, j: (0, j))
    return pl.pallas_call(
        _post_kernel,
        grid=(m // tt, c // cb),
        in_specs=[big, big, big, big, vec, vec, pl.BlockSpec((cb, cb), lambda i, j: (0, 0))],
        out_specs=big,
        out_shape=jax.ShapeDtypeStruct((m, c), BF),
        compiler_params=_cp("parallel", "parallel"),
        name="rwkv_post",
    )(y0, y1, bv, g, lw, lb, bd)


def _outproj_kernel(ka, a_ref, b_ref, w_ref, x_ref, o_ref):
    acc = _dot(a_ref[...], w_ref[:ka, :]) + _dot(b_ref[...], w_ref[ka:, :])
    o_ref[...] = x_ref[...] + acc


def _outproj(att, rw, w, x, tm, tn):
    m, ka = att.shape
    kb = rw.shape[1]
    n = w.shape[1]
    return pl.pallas_call(
        functools.partial(_outproj_kernel, ka),
        grid=(m // tm, n // tn),
        in_specs=[pl.BlockSpec((tm, ka), lambda i, j: (i, 0)),
                  pl.BlockSpec((tm, kb), lambda i, j: (i, 0)),
                  pl.BlockSpec((ka + kb, tn), lambda i, j: (0, j)),
                  pl.BlockSpec((tm, tn), lambda i, j: (i, j))],
        out_specs=pl.BlockSpec((tm, tn), lambda i, j: (i, j)),
        out_shape=jax.ShapeDtypeStruct((m, n), F32),
        compiler_params=_cp("parallel", "parallel"),
        name="out_proj",
    )(att, rw, w, x)


def _memkv_kernel(m_ref, g_ref, wk_ref, wv_ref, kn_ref, k_out, v_out):
    mm = _rms_rows(m_ref[...], g_ref[...]).astype(BF)
    k = _dot(mm, wk_ref[...])
    for h in range(MEM_HEADS):
        hs = slice(h * MEM_HEAD_DIM, (h + 1) * MEM_HEAD_DIM)
        k_out[:, hs] = _rms_rows(k[:, hs], kn_ref[...]).astype(BF)
    v_out[...] = _dot(mm, wv_ref[...]).astype(BF)


def _memkv(mem, g, wk, wv, kn):
    rows, d = mem.shape
    w = wk.shape[1]
    full = lambda shape: pl.BlockSpec(shape, lambda i: (0, 0))
    return pl.pallas_call(
        _memkv_kernel,
        grid=(rows // N_MEM,),
        in_specs=[pl.BlockSpec((N_MEM, d), lambda i: (i, 0)), full((1, d)), full((d, w)), full((d, w)),
                  full((1, MEM_HEAD_DIM))],
        out_specs=[pl.BlockSpec((N_MEM, w), lambda i: (i, 0))] * 2,
        out_shape=[jax.ShapeDtypeStruct((rows, w), BF)] * 2,
        compiler_params=_cp("parallel"),
        name="mem_kv",
    )(mem, g, wk, wv, kn)


def _memattn_kernel(x_ref, k_ref, v_ref, gm_ref, wq_ref, qn_ref, wo_ref, gf_ref, x_out, h_out):
    x = x_ref[...]
    h = _rms_rows(x, gm_ref[...]).astype(BF)
    q = _dot(h, wq_ref[...])
    scale = MEM_HEAD_DIM ** -0.5
    outs = []
    for hd in range(MEM_HEADS):
        hs = slice(hd * MEM_HEAD_DIM, (hd + 1) * MEM_HEAD_DIM)
        qh = _rms_rows(q[:, hs], qn_ref[...]).astype(BF)
        s = _dot_nt(qh, k_ref[:, hs]) * scale
        p = jnp.exp(s - jnp.max(s, axis=-1, keepdims=True))
        den = jnp.sum(p, axis=-1, keepdims=True)
        outs.append((_dot(p.astype(BF), v_ref[:, hs]) / den).astype(BF))
    o = jnp.concatenate(outs, axis=1)
    x2 = x + _dot(o, wo_ref[...])
    x_out[...] = x2
    h_out[...] = _rms_rows(x2, gf_ref[...]).astype(BF)


def _memattn(x, k, v, gm, wq, qn, wo, gf, segs, tm):
    m, d = x.shape
    w = wq.shape[1]
    full = lambda shape: pl.BlockSpec(shape, lambda i: (0, 0))
    kv = pl.BlockSpec((N_MEM, w), lambda i: (_batch_of(i * tm, segs), 0))
    row = pl.BlockSpec((tm, d), lambda i: (i, 0))
    return pl.pallas_call(
        _memattn_kernel,
        grid=(m // tm,),
        in_specs=[row, kv, kv, full((1, d)), full((d, w)), full((1, MEM_HEAD_DIM)), full((w, d)), full((1, d))],
        out_specs=[row, row],
        out_shape=[jax.ShapeDtypeStruct((m, d), F32), jax.ShapeDtypeStruct((m, d), BF)],
        compiler_params=_cp("parallel"),
        name="mem_attn",
    )(x, k, v, gm, wq, qn, wo, gf)


def _ffn_up_kernel(h_ref, wg_ref, wu_ref, o_ref):
    h = h_ref[...]
    gate = _dot(h, wg_ref[0])
    up = _dot(h, wu_ref[0])
    o_ref[...] = (gate * jax.nn.sigmoid(gate) * up).astype(o_ref.dtype)


def _ffn_up(h, wg, wu, tm):
    m, d = h.shape
    nt, _, tn = wg.shape
    wspec = pl.BlockSpec((1, d, tn), lambda i, j: (j, 0, 0))
    return pl.pallas_call(
        _ffn_up_kernel,
        grid=(m // tm, nt),
        in_specs=[pl.BlockSpec((tm, d), lambda i, j: (i, 0)), wspec, wspec],
        out_specs=pl.BlockSpec((tm, tn), lambda i, j: (i, j)),
        out_shape=jax.ShapeDtypeStruct((m, nt * tn), BF),
        compiler_params=_cp("parallel", "parallel"),
        name="ffn_up",
    )(h, wg, wu)


def _ffn_down_kernel(a_ref, w_ref, x_ref, o_ref):
    o_ref[...] = x_ref[...] + _dot(a_ref[...], w_ref[0])


def _ffn_down(act, wd, x, tm):
    m, f = act.shape
    nt, _, tn = wd.shape
    return pl.pallas_call(
        _ffn_down_kernel,
        grid=(m // tm, nt),
        in_specs=[pl.BlockSpec((tm, f), lambda i, j: (i, 0)),
                  pl.BlockSpec((1, f, tn), lambda i, j: (j, 0, 0)),
                  pl.BlockSpec((tm, tn), lambda i, j: (i, j))],
        out_specs=pl.BlockSpec((tm, tn), lambda i, j: (i, j)),
        out_shape=jax.ShapeDtypeStruct((m, nt * tn), F32),
        compiler_params=_cp("parallel", "parallel"),
        name="ffn_down",
    )(act, wd, x)


def _rel_bucket(rel):
    half = N_REL_BUCKETS // 2
    exact = half // 2
    n = np.abs(rel)
    large = exact + (np.log(np.maximum(n, 1) / exact) / np.log(REL_MAX_DIST / exact)
                     * (half - exact)).astype(np.int32)
    large = np.minimum(large, half - 1)
    return (rel > 0).astype(np.int32) * half + np.where(n < exact, n, large)


def _attn_bias(rel_bias):
    qi = np.arange(BLOCK)[:, None]
    kj = np.arange(3 * BLOCK)[None, :]
    rel = kj - BLOCK - qi
    bias = rel_bias.astype(F32)[_rel_bucket(rel)]
    bias = jnp.transpose(bias, (2, 0, 1))
    return jnp.where(jnp.asarray(np.abs(rel) <= BLOCK)[None], bias, NEG)


def _tri_consts(tt):
    t = np.arange(tt)[:, None]
    s = np.arange(tt)[None, :]
    same = (t // CHUNK) == (s // CHUNK)
    mats = [same & (s <= t), same & (s > t), same & (s >= t), same & (s < t)]
    return jnp.asarray(np.stack(mats).astype(np.float32), dtype=BF)


def _block_diag_ones(cb):
    i = np.arange(cb)
    return jnp.asarray(((i[:, None] // RWKV_HEAD) == (i[None, :] // RWKV_HEAD)).astype(np.float32), dtype=BF)


def _tiles(w, tn):
    k, n = w.shape
    return jnp.transpose(w.reshape(k, n // tn, tn), (1, 0, 2))


def _pick(n, pref):
    t = min(n, pref)
    assert n % t == 0, (n, t)
    return t


def kernel(x_prompt, x_sample, mem_prompt, mem_sample, rel_bias, norm_mix, w_in, q_norm, k_norm, sink,
           shift_prev, shift_next, w0, w2, a0, a2, g2, k_k, k_a, r_k, lnx_w, lnx_b, w_out,
           norm_mem, norm_memkv, wq_mem, wk_mem, wv_mem, wo_mem, qn_mem, kn_mem,
           norm_ffn, w_gate, w_up, w_down):
    b1, t1, d = x_prompt.shape
    b2, t2, _ = x_sample.shape
    segs = ((b1, t1), (b2, t2))
    m1 = b1 * t1
    x = jnp.concatenate([x_prompt.reshape(m1, d), x_sample.reshape(b2 * t2, d)], axis=0)
    mem = jnp.concatenate([mem_prompt.reshape(-1, d), mem_sample.reshape(-1, d)], axis=0)
    m = x.shape[0]
    depth = w_in.shape[0]
    c = k_k.shape[1]
    att_cols = (N_Q_HEADS + 2 * N_KV_HEADS) * HEAD_DIM
    lora0 = 3 * c

    tm = _pick(m, 512)
    tt = _pick(min(t1, t2), 256)
    cb = _pick(c, 512)
    ts = _pick(min(t1, t2), 256)
    cs = _pick(c, 512)
    tm_up = _pick(m, 1024)
    tm_mem = _pick(min(t1, t2), 256)
    tn_ffn = 256

    bias = _attn_bias(rel_bias)
    tri = _tri_consts(tt)
    bd = _block_diag_ones(cb)
    row = lambda a: a.reshape(1, -1).astype(F32)

    for l in range(depth):
        w_in_p = jnp.pad(w_in[l], ((0, 0), (0, XTRA_W - LORA_W - GATE_LORA))).astype(BF)
        sp, sn = shift_prev[l], shift_next[l]
        padx = lambda a: jnp.pad(a[lora0:], (0, XTRA_W - LORA_W - GATE_LORA)).reshape(1, -1)
        wl = jnp.zeros((4, LORA_W, c), F32)
        wl = wl.at[0, 0:DECAY_LORA].set(w2[l, 0]).at[1, DECAY_LORA:2 * DECAY_LORA].set(w2[l, 1])
        wl = wl.at[2, 2 * DECAY_LORA:2 * DECAY_LORA + AAA_LORA].set(a2[l, 0])
        wl = wl.at[3, 2 * DECAY_LORA + AAA_LORA:].set(a2[l, 1])
        prm = dict(sp_r=row(sp[0:c]), sn_r=row(sn[0:c]), sp_k=row(sp[c:2 * c]), sn_k=row(sn[c:2 * c]),
                   sp_v=row(sp[2 * c:3 * c]), sn_v=row(sn[2 * c:3 * c]), sp_x=padx(sp), sn_x=padx(sn),
                   w0=w0[l], a0=a0[l], wl=wl.astype(BF), g2=g2[l].astype(BF),
                   k_k=row(k_k[l]), k_a=row(k_a[l]), r_k=row(r_k[l]), bd=bd, tri=tri)

        proj = _inproj(x, row(norm_mix[l]), w_in_p, tm, 768)
        att = _attention(proj, bias, row(q_norm[l]), row(k_norm[l]), sink[l].astype(F32), segs)
        po = _prep(proj, prm, segs, tt, cb)
        v_b, bv, g = po[14], po[15], po[16]
        y0 = _scan(0, po[0:6], v_b, po[6], segs, ts, cs)
        y1 = _scan(1, po[7:13], v_b, po[13], segs, ts, cs)
        rw = _post(y0, y1, bv, g, row(lnx_w[l]), row(lnx_b[l]), bd, tt, cb)
        x = _outproj(att, rw, w_out[l].astype(BF), x, tm, 1024)

        km, vm = _memkv(mem, row(norm_memkv[l]), wk_mem[l].astype(BF), wv_mem[l].astype(BF), row(kn_mem[l]))
        x, h3 = _memattn(x, km, vm, row(norm_mem[l]), wq_mem[l].astype(BF), row(qn_mem[l]),
                         wo_mem[l].astype(BF), row(norm_ffn[l]), segs, tm_mem)

        act = _ffn_up(h3, _tiles(w_gate[l].astype(BF), tn_ffn), _tiles(w_up[l].astype(BF), tn_ffn), tm_up)
        x = _ffn_down(act, _tiles(w_down[l].astype(BF), tn_ffn), x, tm)

    return (x[:m1].reshape(b1, t1, d), x[m1:].reshape(b2, t2, d))
```

```python
import functools
import math

import numpy as np
import jax
import jax.numpy as jnp
from jax import lax
from jax.experimental import pallas as pl
from jax.experimental.pallas import tpu as pltpu

F32 = jnp.float32
BF = jnp.bfloat16

HEAD_DIM = 128
N_Q_HEADS = 16
N_KV_HEADS = 4
GQA = N_Q_HEADS // N_KV_HEADS
BLOCK = 128
N_REL_BUCKETS = 32
REL_MAX_DIST = 128
RWKV_HEAD = 64
DECAY_LORA = 96
AAA_LORA = 96
GATE_LORA = 256
LNX_EPS = 64e-5
MEM_HEADS = 4
MEM_HEAD_DIM = 128
N_MEM = 256
RMS_EPS = 1e-6
NEG = -1e30

LANE = 128
CHUNK = 64
LORA_W = 2 * DECAY_LORA + 2 * AAA_LORA
XTRA_W = 768
VMEM_LIMIT = 56 * 1024 * 1024


def _cp(*sem):
    return pltpu.CompilerParams(dimension_semantics=sem, vmem_limit_bytes=VMEM_LIMIT)


def _dot(a, b):
    return jnp.dot(a, b, preferred_element_type=F32)


def _dot_nt(a, b):
    return lax.dot_general(a, b, (((1,), (1,)), ((), ())), preferred_element_type=F32)


def _dot_tn(a, b):
    return lax.dot_general(a, b, (((0,), (0,)), ((), ())), preferred_element_type=F32)


def _split2(x):
    hi = x.astype(BF)
    lo = (x - hi.astype(F32)).astype(BF)
    return hi, lo


def _rms_rows(x, g):
    ms = jnp.mean(x * x, axis=-1, keepdims=True)
    return x * lax.rsqrt(ms + RMS_EPS) * g


def _seq_info(row, segs):
    off = 0
    pos = None
    tlen = None
    for n, t in segs:
        p = lax.rem(row - off, t)
        if pos is None:
            pos, tlen = p, jnp.int32(t)
        else:
            inside = row >= off
            pos = jnp.where(inside, p, pos)
            tlen = jnp.where(inside, t, tlen)
        off += n * t
    return pos, tlen


def _batch_of(row, segs):
    off = 0
    boff = 0
    res = None
    for n, t in segs:
        b = boff + (row - off) // t
        res = b if res is None else jnp.where(row >= off, b, res)
        off += n * t
        boff += n
    return res


def _inproj_kernel(x_ref, g_ref, w_ref, o_ref, h_ref):
    @pl.when(pl.program_id(1) == 0)
    def _():
        h_ref[...] = _rms_rows(x_ref[...], g_ref[...]).astype(BF)

    o_ref[...] = _dot(h_ref[...], w_ref[...])


def _inproj(x, g, w, tm, tn):
    m, d = x.shape
    n = w.shape[1]
    return pl.pallas_call(
        _inproj_kernel,
        grid=(m // tm, n // tn),
        in_specs=[pl.BlockSpec((tm, d), lambda i, j: (i, 0)),
                  pl.BlockSpec((1, d), lambda i, j: (0, 0)),
                  pl.BlockSpec((d, tn), lambda i, j: (0, j))],
        out_specs=pl.BlockSpec((tm, tn), lambda i, j: (i, j)),
        out_shape=jax.ShapeDtypeStruct((m, n), F32),
        scratch_shapes=[pltpu.VMEM((tm, d), BF)],
        compiler_params=_cp("parallel", "arbitrary"),
        name="in_proj",
    )(x, g, w)


def _attn_kernel(segs, q_ref, kp_ref, kc_ref, kn_ref, vp_ref, vc_ref, vn_ref,
                 bias_ref, qn_ref, kn_g_ref, sink_ref, o_ref):
    n = pl.program_id(0)
    pos, tlen = _seq_info(n * BLOCK, segs)
    first = pos == 0
    last = pos + BLOCK == tlen
    col = lax.broadcasted_iota(jnp.int32, (BLOCK, 3 * BLOCK), 1)
    dead = (first & (col < BLOCK)) | (last & (col >= 2 * BLOCK))
    scale = HEAD_DIM ** -0.5
    for kh in range(N_KV_HEADS):
        ls = slice(kh * HEAD_DIM, (kh + 1) * HEAD_DIM)
        kw = jnp.concatenate([kp_ref[:, ls], kc_ref[:, ls], kn_ref[:, ls]], axis=0)
        kw = _rms_rows(kw, kn_g_ref[...]).astype(BF)
        vw = jnp.concatenate([vp_ref[:, ls], vc_ref[:, ls], vn_ref[:, ls]], axis=0).astype(BF)
        for g in range(GQA):
            h = kh * GQA + g
            hs = slice(h * HEAD_DIM, (h + 1) * HEAD_DIM)
            q = _rms_rows(q_ref[:, hs], qn_ref[...]).astype(BF)
            s = _dot_nt(q, kw) * scale + bias_ref[h]
            s = jnp.where(dead, NEG, s)
            sk = sink_ref[h]
            mx = jnp.maximum(jnp.max(s, axis=-1, keepdims=True), sk)
            p = jnp.exp(s - mx)
            den = jnp.sum(p, axis=-1, keepdims=True) + jnp.exp(sk - mx)
            o = _dot(p.astype(BF), vw) / den
            o_ref[:, hs] = o.astype(o_ref.dtype)


def _attention(proj, bias, qn, kn, sink, segs):
    m = proj.shape[0]
    nb = m // BLOCK
    aw = N_Q_HEADS * HEAD_DIM
    kvw = N_KV_HEADS * HEAD_DIM
    kcol = aw // kvw
    vcol = kcol + 1
    prev = lambda n: jnp.maximum(n - 1, 0)
    nxt = lambda n: jnp.minimum(n + 1, nb - 1)
    specs = [pl.BlockSpec((BLOCK, aw), lambda n: (n, 0))]
    for c in (kcol, vcol):
        specs += [pl.BlockSpec((BLOCK, kvw), lambda n, c=c: (prev(n), c)),
                  pl.BlockSpec((BLOCK, kvw), lambda n, c=c: (n, c)),
                  pl.BlockSpec((BLOCK, kvw), lambda n, c=c: (nxt(n), c))]
    specs += [pl.BlockSpec((N_Q_HEADS, BLOCK, 3 * BLOCK), lambda n: (0, 0, 0)),
              pl.BlockSpec((1, HEAD_DIM), lambda n: (0, 0)),
              pl.BlockSpec((1, HEAD_DIM), lambda n: (0, 0)),
              pl.BlockSpec(memory_space=pltpu.SMEM)]
    return pl.pallas_call(
        functools.partial(_attn_kernel, segs),
        grid=(nb,),
        in_specs=specs,
        out_specs=pl.BlockSpec((BLOCK, aw), lambda n: (n, 0)),
        out_shape=jax.ShapeDtypeStruct((m, aw), BF),
        compiler_params=_cp("parallel"),
        name="window_attn",
    )(proj, proj, proj, proj, proj, proj, proj, bias, qn, kn, sink)


def _prep_kernel(segs, tt, *refs):
    (pr, pr_p, pr_n, pk, pk_p, pk_n, pv, pv_p, pv_n, px, px_p, px_n,
     spr, snr, spk, snk, spv, snv, spx, snx, w0, a0, wl, g2, kk_g, ka_g, rk_g, bd, tri) = refs[:29]
    outs = refs[29:]
    zouts = (outs[0:7], outs[7:14])
    vo, bvo, go = outs[14:17]

    row0 = pl.program_id(0) * tt
    pos, tlen = _seq_info(row0, segs)
    first = pos == 0
    last = pos + tt == tlen
    rows = lax.broadcasted_iota(jnp.int32, (tt, 1), 0)

    def shifted(x_ref, p_ref, n_ref, sp, sn):
        x = x_ref[...]
        prow = jnp.where(first, 0.0, p_ref[7:8, :])
        nrow = jnp.where(last, 0.0, n_ref[0:1, :])
        prev = jnp.where(rows == 0, prow, pltpu.roll(x, 1, 0))
        nxt = jnp.where(rows == tt - 1, nrow, pltpu.roll(x, tt - 1, 0))
        return x + sp[...] * (prev - x) + sn[...] * (nxt - x)

    r = shifted(pr, pr_p, pr_n, spr, snr)
    k = shifted(pk, pk_p, pk_n, spk, snk)
    v = shifted(pv, pv_p, pv_n, spv, snv)
    x = shifted(px, px_p, px_n, spx, snx)

    xl = x[:, :LORA_W]
    lane = lax.broadcasted_iota(jnp.int32, xl.shape, 1)
    lx = jnp.where(lane < 2 * DECAY_LORA, jnp.tanh(xl), xl).astype(BF)
    sg = jax.nn.sigmoid(x[:, LORA_W:LORA_W + GATE_LORA]).astype(BF)
    go[...] = _dot(sg, g2[...]).astype(go.dtype)

    kkr = k * kk_g[...]
    n2 = _dot((kkr * kkr).astype(BF), bd[...])
    kk = kkr / jnp.maximum(jnp.sqrt(n2), 1e-12)

    kd_sum = None
    for z in (0, 1):
        at_o, rt_o, bt_o, kt_o, bg_o, kg_o, ct_o = zouts[z]
        wraw = w0[z:z + 1, :] + _dot(lx, wl[z])
        a = jax.nn.sigmoid(a0[z:z + 1, :] + _dot(lx, wl[2 + z]))
        lw = -math.exp(-0.5) * jax.nn.sigmoid(wraw)
        hi, lo = _split2(lw)
        cin = _dot(tri[2 * z], hi) + _dot(tri[2 * z], lo)
        suf = _dot(tri[2 * z + 1], hi) + _dot(tri[2 * z + 1], lo)
        kd = k * (1.0 + (a - 1.0) * ka_g[...])
        b = kk * a
        e_neg = jnp.exp(-cin)
        e_suf = jnp.exp(suf)
        at_o[...] = (-kk * jnp.exp(cin - lw)).astype(BF)
        rt_o[...] = (r * jnp.exp(cin)).astype(BF)
        bt_o[...] = (b * e_neg).astype(BF)
        kt_o[...] = (kd * e_neg).astype(BF)
        bg_o[...] = (b * e_suf).astype(BF)
        kg_o[...] = (kd * e_suf).astype(BF)
        ctot = cin + suf
        ct_o[...] = jnp.concatenate([ctot[c * CHUNK:c * CHUNK + 8] for c in range(tt // CHUNK)], axis=0)
        kd_sum = kd if kd_sum is None else kd_sum + kd

    bonus = _dot((r * kd_sum * rk_g[...]).astype(BF), bd[...])
    bvo[...] = (bonus * v).astype(bvo.dtype)
    vo[...] = v.astype(BF)


def _prep(proj, p, segs, tt, cb):
    m = proj.shape[0]
    c = p["k_k"].shape[1]
    rcol0 = (N_Q_HEADS + 2 * N_KV_HEADS) * HEAD_DIM
    nrow8 = m // 8
    t8 = tt // 8

    def trio(width, colfn):
        return [pl.BlockSpec((tt, width), lambda i, j: (i, colfn(j))),
                pl.BlockSpec((8, width), lambda i, j: (jnp.maximum(i * t8 - 1, 0), colfn(j))),
                pl.BlockSpec((8, width), lambda i, j: (jnp.minimum((i + 1) * t8, nrow8 - 1), colfn(j)))]

    specs = []
    for sec in range(3):
        base = (rcol0 + sec * c) // cb
        specs += trio(cb, lambda j, base=base: base + j)
    xblk = (rcol0 + 3 * c) // XTRA_W
    specs += trio(XTRA_W, lambda j: xblk)
    vec = lambda: pl.BlockSpec((1, cb), lambda i, j: (0, j))
    specs += [vec(), vec(), vec(), vec(), vec(), vec(),
              pl.BlockSpec((1, XTRA_W), lambda i, j: (0, 0)),
              pl.BlockSpec((1, XTRA_W), lambda i, j: (0, 0)),
              pl.BlockSpec((2, cb), lambda i, j: (0, j)),
              pl.BlockSpec((2, cb), lambda i, j: (0, j)),
              pl.BlockSpec((4, LORA_W, cb), lambda i, j: (0, 0, j)),
              pl.BlockSpec((GATE_LORA, cb), lambda i, j: (0, j)),
              vec(), vec(), vec(),
              pl.BlockSpec((cb, cb), lambda i, j: (0, 0)),
              pl.BlockSpec((4, tt, tt), lambda i, j: (0, 0, 0))]
    big = pl.BlockSpec((tt, cb), lambda i, j: (i, j))
    small = pl.BlockSpec((t8, cb), lambda i, j: (i, j))
    out_specs = ([big] * 6 + [small]) * 2 + [big] * 3
    big_s = jax.ShapeDtypeStruct((m, c), BF)
    small_s = jax.ShapeDtypeStruct((nrow8, c), F32)
    out_shape = ([big_s] * 6 + [small_s]) * 2 + [big_s] * 3
    args = [proj] * 12 + [p["sp_r"], p["sn_r"], p["sp_k"], p["sn_k"], p["sp_v"], p["sn_v"],
                          p["sp_x"], p["sn_x"], p["w0"], p["a0"], p["wl"], p["g2"],
                          p["k_k"], p["k_a"], p["r_k"], p["bd"], p["tri"]]
    return pl.pallas_call(
        functools.partial(_prep_kernel, segs, tt),
        grid=(m // tt, c // cb),
        in_specs=specs,
        out_specs=out_specs,
        out_shape=out_shape,
        compiler_params=_cp("parallel", "arbitrary"),
        name="rwkv_prep",
    )(*args)


def _scan_kernel(z, segs, ts, npair, at, rt, bt, kt, bg, kg, v, ct, y_ref, h_ref):
    i = pl.program_id(1)
    nblk = pl.num_programs(1)
    blk = i if z == 0 else nblk - 1 - i
    pos, tlen = _seq_info(blk * ts, segs)
    reset = (pos == 0) if z == 0 else (pos + ts == tlen)

    @pl.when(reset)
    def _():
        h_ref[...] = jnp.zeros_like(h_ref)

    n = 2 * CHUNK
    lane = lax.broadcasted_iota(jnp.int32, (CHUNK, LANE), 1)
    m_lo = jnp.where(lane < RWKV_HEAD, 1.0, 0.0).astype(BF)
    m_hi = jnp.where(lane < RWKV_HEAD, 0.0, 1.0).astype(BF)
    row = lax.broadcasted_iota(jnp.int32, (n, n), 0)
    col = lax.broadcasted_iota(jnp.int32, (n, n), 1)
    strict = (row > col) if z == 0 else (row < col)
    incl = (row >= col) if z == 0 else (row <= col)
    blk16 = (row // 16) == (col // 16)
    off32 = ((row // 32) == (col // 32)) & ((row // 16) != (col // 16))
    off64 = ((row // 64) == (col // 64)) & ((row // 32) != (col // 32))
    eye = jnp.where(row == col, 1.0, 0.0).astype(F32)

    nch = ts // CHUNK
    order = list(range(nch)) if z == 0 else list(range(nch - 1, -1, -1))
    items = [(p, c) for c in order for p in range(npair)]

    def tile(ref, p, c):
        return ref[c * CHUNK:(c + 1) * CHUNK, p * LANE:(p + 1) * LANE]

    def stack(x):
        return jnp.concatenate([x * m_lo, x * m_hi], axis=0)

    def each(fn, *lists):
        return [fn(*args) for args in zip(*lists)]

    xa = [stack(tile(at, p, c)) for p, c in items]
    xr = [stack(tile(rt, p, c)) for p, c in items]
    xb = [stack(tile(bt, p, c)) for p, c in items]
    xk = [stack(tile(kt, p, c)) for p, c in items]
    xbg = [stack(tile(bg, p, c)) for p, c in items]
    xkg = [stack(tile(kg, p, c)) for p, c in items]
    vs = [stack(tile(v, p, c)) for p, c in items]

    pm = each(lambda a, r, b, k: _dot_nt(jnp.concatenate([a, r], axis=0), jnp.concatenate([b, k], axis=0)),
              xa, xr, xb, xk)
    a_ab = [jnp.where(strict, m[:n, :n], 0.0) for m in pm]
    a_ak = [jnp.where(strict, m[:n, n:], 0.0).astype(BF) for m in pm]
    a_rb = [jnp.where(incl, m[n:, :n], 0.0).astype(BF) for m in pm]
    a_rk = [jnp.where(incl, m[n:, n:], 0.0).astype(BF) for m in pm]

    ad = [jnp.where(blk16, a, 0.0) for a in a_ab]
    adb = [a.astype(BF) for a in ad]
    a2 = each(lambda a: _dot(a, a).astype(BF), adb)
    av = each(lambda a, w: _dot(a, w).astype(BF), a_ak, vs)
    a4 = each(lambda a: _dot(a, a).astype(BF), a2)
    tinv = each(lambda a, a2_: eye + a + _dot((eye + a).astype(BF), a2_), ad, a2)
    a8 = each(lambda a: _dot(a, a).astype(BF), a4)
    tinv = each(lambda t, a: t + _dot(t.astype(BF), a), tinv, a4)
    tinv = each(lambda t, a: t + _dot(t.astype(BF), a), tinv, a8)
    for off in (off32, off64):
        tb = [t.astype(BF) for t in tinv]
        ao = [jnp.where(off, a, 0.0).astype(BF) for a in a_ab]
        mid = each(lambda t, a: _dot(t, a).astype(BF), tb, ao)
        tinv = each(lambda t, m_, t_b: t + _dot(m_, t_b), tinv, mid, tb)

    wq = each(lambda t, a, q: _dot(t.astype(BF), jnp.concatenate([a, q], axis=1)).astype(BF), tinv, xa, av)
    rb = each(lambda a, w: _dot(a, w), a_rb, wq)
    rkv = each(lambda a, w: _dot(a, w), a_rk, vs)
    bgw = each(lambda b, w: _dot_tn(b, w), xbg, wq)
    kgv = each(lambda k, w: _dot_tn(k, w), xkg, vs)
    ry = each(lambda r, m_: (r.astype(F32) + m_[:, :n]).astype(BF), xr, rb)
    qy = each(lambda m_, k_: m_[:, n:] + k_, rb, rkv)
    gm = each(lambda m_: m_[:, :n].astype(BF), bgw)
    jm = each(lambda m_, k_: m_[:, n:] + k_, bgw, kgv)
    gam = [jnp.exp(jnp.transpose(jnp.broadcast_to(ct[c * 8:c * 8 + 1, p * LANE:(p + 1) * LANE], (n, n))))
           for p, c in items]

    hs = [h_ref[p] for p in range(npair)]
    for idx, (p, c) in enumerate(items):
        hb = hs[p].astype(BF)
        yst = _dot(ry[idx], hb) + qy[idx]
        y_ref[c * CHUNK:(c + 1) * CHUNK, p * LANE:(p + 1) * LANE] = yst[:CHUNK] + yst[CHUNK:]
        hs[p] = gam[idx] * hs[p] + _dot(gm[idx], hb) + jm[idx]
    for p in range(npair):
        h_ref[p] = hs[p]


def _scan(z, ops, v, ct, segs, ts, cs):
    m, c = v.shape
    nblk = m // ts
    npair = cs // LANE
    rowmap = (lambda j, i: (i, j)) if z == 0 else (lambda j, i: (nblk - 1 - i, j))
    big = pl.BlockSpec((ts, cs), rowmap)
    small = pl.BlockSpec((ts // 8, cs), rowmap)
    return pl.pallas_call(
        functools.partial(_scan_kernel, z, segs, ts, npair),
        grid=(c // cs, nblk),
        in_specs=[big] * 7 + [small],
        out_specs=big,
        out_shape=jax.ShapeDtypeStruct((m, c), F32),
        scratch_shapes=[pltpu.VMEM((npair, LANE, LANE), F32)],
        compiler_params=_cp("parallel", "arbitrary"),
        name="rwkv_scan_fwd" if z == 0 else "rwkv_scan_bwd",
    )(*ops, v, ct)


def _post_kernel(y0, y1, bv, g, lw, lb, bd, o_ref):
    y = y0[...] + y1[...]
    inv_n = 1.0 / RWKV_HEAD
    hi, lo = _split2(y)
    mu = (_dot(hi, bd[...]) + _dot(lo, bd[...])) * inv_n
    d = y - mu
    var = _dot((d * d).astype(BF), bd[...]) * inv_n
    yn = d * lax.rsqrt(var + LNX_EPS) * lw[...] + lb[...]
    o_ref[...] = ((yn + bv[...].astype(F32)) * g[...].astype(F32)).astype(o_ref.dtype)


def _post(y0, y1, bv, g, lw, lb, bd, tt, cb):
    m, c = y0.shape
    big = pl.BlockSpec((tt, cb), lambda i, j: (i, j))
    vec = pl.BlockSpec((1, cb), lambda i, j: (0, j))
    return pl.pallas_call(
        _post_kernel,
        grid=(m // tt, c // cb),
        in_specs=[big, big, big, big, vec, vec, pl.BlockSpec((cb, cb), lambda i, j: (0, 0))],
        out_specs=big,
        out_shape=jax.ShapeDtypeStruct((m, c), BF),
        compiler_params=_cp("parallel", "parallel"),
        name="rwkv_post",
    )(y0, y1, bv, g, lw, lb, bd)


def _outproj_kernel(ka, a_ref, b_ref, w_ref, x_ref, o_ref):
    acc = _dot(a_ref[...], w_ref[:ka, :]) + _dot(b_ref[...], w_ref[ka:, :])
    o_ref[...] = x_ref[...] + acc


def _outproj(att, rw, w, x, tm, tn):
    m, ka = att.shape
    kb = rw.shape[1]
    n = w.shape[1]
    return pl.pallas_call(
        functools.partial(_outproj_kernel, ka),
        grid=(m // tm, n // tn),
        in_specs=[pl.BlockSpec((tm, ka), lambda i, j: (i, 0)),
                  pl.BlockSpec((tm, kb), lambda i, j: (i, 0)),
                  pl.BlockSpec((ka + kb, tn), lambda i, j: (0, j)),
                  pl.BlockSpec((tm, tn), lambda i, j: (i, j))],
        out_specs=pl.BlockSpec((tm, tn), lambda i, j: (i, j)),
        out_shape=jax.ShapeDtypeStruct((m, n), F32),
        compiler_params=_cp("parallel", "parallel"),
        name="out_proj",
    )(att, rw, w, x)


def _memkv_kernel(m_ref, g_ref, wk_ref, wv_ref, kn_ref, k_out, v_out):
    mm = _rms_rows(m_ref[...], g_ref[...]).astype(BF)
    k = _dot(mm, wk_ref[...])
    for h in range(MEM_HEADS):
        hs = slice(h * MEM_HEAD_DIM, (h + 1) * MEM_HEAD_DIM)
        k_out[:, hs] = _rms_rows(k[:, hs], kn_ref[...]).astype(BF)
    v_out[...] = _dot(mm, wv_ref[...]).astype(BF)


def _memkv(mem, g, wk, wv, kn):
    rows, d = mem.shape
    w = wk.shape[1]
    full = lambda shape: pl.BlockSpec(shape, lambda i: (0, 0))
    return pl.pallas_call(
        _memkv_kernel,
        grid=(rows // N_MEM,),
        in_specs=[pl.BlockSpec((N_MEM, d), lambda i: (i, 0)), full((1, d)), full((d, w)), full((d, w)),
                  full((1, MEM_HEAD_DIM))],
        out_specs=[pl.BlockSpec((N_MEM, w), lambda i: (i, 0))] * 2,
        out_shape=[jax.ShapeDtypeStruct((rows, w), BF)] * 2,
        compiler_params=_cp("parallel"),
        name="mem_kv",
    )(mem, g, wk, wv, kn)


def _memattn_kernel(x_ref, k_ref, v_ref, gm_ref, wq_ref, qn_ref, wo_ref, gf_ref, x_out, h_out):
    x = x_ref[...]
    h = _rms_rows(x, gm_ref[...]).astype(BF)
    q = _dot(h, wq_ref[...])
    scale = MEM_HEAD_DIM ** -0.5
    outs = []
    for hd in range(MEM_HEADS):
        hs = slice(hd * MEM_HEAD_DIM, (hd + 1) * MEM_HEAD_DIM)
        qh = _rms_rows(q[:, hs], qn_ref[...]).astype(BF)
        s = _dot_nt(qh, k_ref[:, hs]) * scale
        p = jnp.exp(s - jnp.max(s, axis=-1, keepdims=True))
        den = jnp.sum(p, axis=-1, keepdims=True)
        outs.append((_dot(p.astype(BF), v_ref[:, hs]) / den).astype(BF))
    o = jnp.concatenate(outs, axis=1)
    x2 = x + _dot(o, wo_ref[...])
    x_out[...] = x2
    h_out[...] = _rms_rows(x2, gf_ref[...]).astype(BF)


def _memattn(x, k, v, gm, wq, qn, wo, gf, segs, tm):
    m, d = x.shape
    w = wq.shape[1]
    full = lambda shape: pl.BlockSpec(shape, lambda i: (0, 0))
    kv = pl.BlockSpec((N_MEM, w), lambda i: (_batch_of(i * tm, segs), 0))
    row = pl.BlockSpec((tm, d), lambda i: (i, 0))
    return pl.pallas_call(
        _memattn_kernel,
        grid=(m // tm,),
        in_specs=[row, kv, kv, full((1, d)), full((d, w)), full((1, MEM_HEAD_DIM)), full((w, d)), full((1, d))],
        out_specs=[row, row],
        out_shape=[jax.ShapeDtypeStruct((m, d), F32), jax.ShapeDtypeStruct((m, d), BF)],
        compiler_params=_cp("parallel"),
        name="mem_attn",
    )(x, k, v, gm, wq, qn, wo, gf)


def _ffn_up_kernel(h_ref, wg_ref, wu_ref, o_ref):
    h = h_ref[...]
    gate = _dot(h, wg_ref[0])
    up = _dot(h, wu_ref[0])
    o_ref[...] = (gate * jax.nn.sigmoid(gate) * up).astype(o_ref.dtype)


def _ffn_up(h, wg, wu, tm):
    m, d = h.shape
    nt, _, tn = wg.shape
    wspec = pl.BlockSpec((1, d, tn), lambda i, j: (j, 0, 0))
    return pl.pallas_call(
        _ffn_up_kernel,
        grid=(m // tm, nt),
        in_specs=[pl.BlockSpec((tm, d), lambda i, j: (i, 0)), wspec, wspec],
        out_specs=pl.BlockSpec((tm, tn), lambda i, j: (i, j)),
        out_shape=jax.ShapeDtypeStruct((m, nt * tn), BF),
        compiler_params=_cp("parallel", "parallel"),
        name="ffn_up",
    )(h, wg, wu)


def _ffn_down_kernel(a_ref, w_ref, x_ref, o_ref):
    o_ref[...] = x_ref[...] + _dot(a_ref[...], w_ref[0])


def _ffn_down(act, wd, x, tm):
    m, f = act.shape
    nt, _, tn = wd.shape
    return pl.pallas_call(
        _ffn_down_kernel,
        grid=(m // tm, nt),
        in_specs=[pl.BlockSpec((tm, f), lambda i, j: (i, 0)),
                  pl.BlockSpec((1, f, tn), lambda i, j: (j, 0, 0)),
                  pl.BlockSpec((tm, tn), lambda i, j: (i, j))],
        out_specs=pl.BlockSpec((tm, tn), lambda i, j: (i, j)),
        out_shape=jax.ShapeDtypeStruct((m, nt * tn), F32),
        compiler_params=_cp("parallel", "parallel"),
        name="ffn_down",
    )(act, wd, x)


def _rel_bucket(rel):
    half = N_REL_BUCKETS // 2
    exact = half // 2
    n = np.abs(rel)
    large = exact + (np.log(np.maximum(n, 1) / exact) / np.log(REL_MAX_DIST / exact)
                     * (half - exact)).astype(np.int32)
    large = np.minimum(large, half - 1)
    return (rel > 0).astype(np.int32) * half + np.where(n < exact, n, large)


def _attn_bias(rel_bias):
    qi = np.arange(BLOCK)[:, None]
    kj = np.arange(3 * BLOCK)[None, :]
    rel = kj - BLOCK - qi
    bias = rel_bias.astype(F32)[_rel_bucket(rel)]
    bias = jnp.transpose(bias, (2, 0, 1))
    return jnp.where(jnp.asarray(np.abs(rel) <= BLOCK)[None], bias, NEG)


def _tri_consts(tt):
    t = np.arange(tt)[:, None]
    s = np.arange(tt)[None, :]
    same = (t // CHUNK) == (s // CHUNK)
    mats = [same & (s <= t), same & (s > t), same & (s >= t), same & (s < t)]
    return jnp.asarray(np.stack(mats).astype(np.float32), dtype=BF)


def _block_diag_ones(cb):
    i = np.arange(cb)
    return jnp.asarray(((i[:, None] // RWKV_HEAD) == (i[None, :] // RWKV_HEAD)).astype(np.float32), dtype=BF)


def _tiles(w, tn):
    k, n = w.shape
    return jnp.transpose(w.reshape(k, n // tn, tn), (1, 0, 2))


def _pick(n, pref):
    t = min(n, pref)
    assert n % t == 0, (n, t)
    return t


def kernel(x_prompt, x_sample, mem_prompt, mem_sample, rel_bias, norm_mix, w_in, q_norm, k_norm, sink,
           shift_prev, shift_next, w0, w2, a0, a2, g2, k_k, k_a, r_k, lnx_w, lnx_b, w_out,
           norm_mem, norm_memkv, wq_mem, wk_mem, wv_mem, wo_mem, qn_mem, kn_mem,
           norm_ffn, w_gate, w_up, w_down):
    b1, t1, d = x_prompt.shape
    b2, t2, _ = x_sample.shape
    segs = ((b1, t1), (b2, t2))
    m1 = b1 * t1
    x = jnp.concatenate([x_prompt.reshape(m1, d), x_sample.reshape(b2 * t2, d)], axis=0)
    mem = jnp.concatenate([mem_prompt.reshape(-1, d), mem_sample.reshape(-1, d)], axis=0)
    m = x.shape[0]
    depth = w_in.shape[0]
    c = k_k.shape[1]
    att_cols = (N_Q_HEADS + 2 * N_KV_HEADS) * HEAD_DIM
    lora0 = 3 * c

    tm = _pick(m, 512)
    tt = _pick(min(t1, t2), 256)
    cb = _pick(c, 512)
    ts = _pick(min(t1, t2), 256)
    cs = _pick(c, 512)
    tm_up = _pick(m, 1024)
    tm_mem = _pick(min(t1, t2), 256)
    tn_ffn = 256

    bias = _attn_bias(rel_bias)
    tri = _tri_consts(tt)
    bd = _block_diag_ones(cb)
    row = lambda a: a.reshape(1, -1).astype(F32)

    for l in range(depth):
        w_in_p = jnp.pad(w_in[l], ((0, 0), (0, XTRA_W - LORA_W - GATE_LORA))).astype(BF)
        sp, sn = shift_prev[l], shift_next[l]
        padx = lambda a: jnp.pad(a[lora0:], (0, XTRA_W - LORA_W - GATE_LORA)).reshape(1, -1)
        wl = jnp.zeros((4, LORA_W, c), F32)
        wl = wl.at[0, 0:DECAY_LORA].set(w2[l, 0]).at[1, DECAY_LORA:2 * DECAY_LORA].set(w2[l, 1])
        wl = wl.at[2, 2 * DECAY_LORA:2 * DECAY_LORA + AAA_LORA].set(a2[l, 0])
        wl = wl.at[3, 2 * DECAY_LORA + AAA_LORA:].set(a2[l, 1])
        prm = dict(sp_r=row(sp[0:c]), sn_r=row(sn[0:c]), sp_k=row(sp[c:2 * c]), sn_k=row(sn[c:2 * c]),
                   sp_v=row(sp[2 * c:3 * c]), sn_v=row(sn[2 * c:3 * c]), sp_x=padx(sp), sn_x=padx(sn),
                   w0=w0[l], a0=a0[l], wl=wl.astype(BF), g2=g2[l].astype(BF),
                   k_k=row(k_k[l]), k_a=row(k_a[l]), r_k=row(r_k[l]), bd=bd, tri=tri)

        proj = _inproj(x, row(norm_mix[l]), w_in_p, tm, 768)
        att = _attention(proj, bias, row(q_norm[l]), row(k_norm[l]), sink[l].astype(F32), segs)
        po = _prep(proj, prm, segs, tt, cb)
        v_b, bv, g = po[14], po[15], po[16]
        y0 = _scan(0, po[0:6], v_b, po[6], segs, ts, cs)
        y1 = _scan(1, po[7:13], v_b, po[13], segs, ts, cs)
        rw = _post(y0, y1, bv, g, row(lnx_w[l]), row(lnx_b[l]), bd, tt, cb)
        x = _outproj(att, rw, w_out[l].astype(BF), x, tm, 1024)

        km, vm = _memkv(mem, row(norm_memkv[l]), wk_mem[l].astype(BF), wv_mem[l].astype(BF), row(kn_mem[l]))
        x, h3 = _memattn(x, km, vm, row(norm_mem[l]), wq_mem[l].astype(BF), row(qn_mem[l]),
                         wo_mem[l].astype(BF), row(norm_ffn[l]), segs, tm_mem)

        act = _ffn_up(h3, _tiles(w_gate[l].astype(BF), tn_ffn), _tiles(w_up[l].astype(BF), tn_ffn), tm_up)
        x = _ffn_down(act, _tiles(w_down[l].astype(BF), tn_ffn), x, tm)

    return (x[:m1].reshape(b1, t1, d), x[m1:].reshape(b2, t2, d))
```

```python
import functools
import math

import numpy as np
import jax
import jax.numpy as jnp
from jax import lax
from jax.experimental import pallas as pl
from jax.experimental.pallas import tpu as pltpu

F32 = jnp.float32
BF = jnp.bfloat16

HEAD_DIM = 128
N_Q_HEADS = 16
N_KV_HEADS = 4
GQA = N_Q_HEADS // N_KV_HEADS
BLOCK = 128
N_REL_BUCKETS = 32
REL_MAX_DIST = 128
RWKV_HEAD = 64
DECAY_LORA = 96
AAA_LORA = 96
GATE_LORA = 256
LNX_EPS = 64e-5
MEM_HEADS = 4
MEM_HEAD_DIM = 128
N_MEM = 256
RMS_EPS = 1e-6
NEG = -1e30

LANE = 128
CHUNK = 64
LORA_W = 2 * DECAY_LORA + 2 * AAA_LORA
XTRA_W = 768
VMEM_LIMIT = 56 * 1024 * 1024


def _cp(*sem):
    return pltpu.CompilerParams(dimension_semantics=sem, vmem_limit_bytes=VMEM_LIMIT)


def _dot(a, b):
    return jnp.dot(a, b, preferred_element_type=F32)


def _dot_nt(a, b):
    return lax.dot_general(a, b, (((1,), (1,)), ((), ())), preferred_element_type=F32)


def _dot_tn(a, b):
    return lax.dot_general(a, b, (((0,), (0,)), ((), ())), preferred_element_type=F32)


def _split2(x):
    hi = x.astype(BF)
    lo = (x - hi.astype(F32)).astype(BF)
    return hi, lo


def _rms_rows(x, g):
    ms = jnp.mean(x * x, axis=-1, keepdims=True)
    return x * lax.rsqrt(ms + RMS_EPS) * g


def _seq_info(row, segs):
    off = 0
    pos = None
    tlen = None
    for n, t in segs:
        p = lax.rem(row - off, t)
        if pos is None:
            pos, tlen = p, jnp.int32(t)
        else:
            inside = row >= off
            pos = jnp.where(inside, p, pos)
            tlen = jnp.where(inside, t, tlen)
        off += n * t
    return pos, tlen


def _batch_of(row, segs):
    off = 0
    boff = 0
    res = None
    for n, t in segs:
        b = boff + (row - off) // t
        res = b if res is None else jnp.where(row >= off, b, res)
        off += n * t
        boff += n
    return res


def _over_parts(call, parts, tm):
    out, t0 = None, 0
    for part in parts:
        assert part.shape[0] % tm == 0
        out = call(part, t0, out)
        t0 += part.shape[0] // tm
    return out


def _inproj_kernel(x_ref, g_ref, w_ref, *rest):
    o_ref, h_ref = rest[-2:]

    @pl.when(pl.program_id(1) == 0)
    def _():
        h_ref[...] = _rms_rows(x_ref[...], g_ref[...]).astype(BF)

    o_ref[...] = _dot(h_ref[...], w_ref[...])


def _inproj(xs, g, w, tm, tn, m):
    d = xs[0].shape[1]
    n = w.shape[1]

    def call(x, t0, prev):
        carried = [] if prev is None else [prev]
        return pl.pallas_call(
            _inproj_kernel,
            grid=(x.shape[0] // tm, n // tn),
            in_specs=[pl.BlockSpec((tm, d), lambda i, j: (i, 0)),
                      pl.BlockSpec((1, d), lambda i, j: (0, 0)),
                      pl.BlockSpec((d, tn), lambda i, j: (0, j))]
            + [pl.BlockSpec(memory_space=pl.ANY)] * len(carried),
            out_specs=pl.BlockSpec((tm, tn), lambda i, j: (t0 + i, j)),
            out_shape=jax.ShapeDtypeStruct((m, n), F32),
            input_output_aliases={3: 0} if carried else {},
            scratch_shapes=[pltpu.VMEM((tm, d), BF)],
            compiler_params=_cp("parallel", "arbitrary"),
            name="in_proj",
        )(x, g, w, *carried)

    return _over_parts(call, xs, tm)


def _attn_kernel(segs, q_ref, kp_ref, kc_ref, kn_ref, vp_ref, vc_ref, vn_ref,
                 bias_ref, qn_ref, kn_g_ref, sink_ref, o_ref):
    n = pl.program_id(0)
    pos, tlen = _seq_info(n * BLOCK, segs)
    first = pos == 0
    last = pos + BLOCK == tlen
    col = lax.broadcasted_iota(jnp.int32, (BLOCK, 3 * BLOCK), 1)
    dead = (first & (col < BLOCK)) | (last & (col >= 2 * BLOCK))
    scale = HEAD_DIM ** -0.5
    heads = range(N_Q_HEADS)
    hsl = [slice(h * HEAD_DIM, (h + 1) * HEAD_DIM) for h in heads]
    kw, vw = [], []
    for kh in range(N_KV_HEADS):
        ls = hsl[kh]
        kcat = jnp.concatenate([kp_ref[:, ls], kc_ref[:, ls], kn_ref[:, ls]], axis=0)
        kw.append(_rms_rows(kcat, kn_g_ref[...]).astype(BF))
        vw.append(jnp.concatenate([vp_ref[:, ls], vc_ref[:, ls], vn_ref[:, ls]], axis=0).astype(BF))
    q = [_rms_rows(q_ref[:, hsl[h]], qn_ref[...]).astype(BF) for h in heads]
    s = [_dot_nt(q[h], kw[h // GQA]) * scale + bias_ref[h] for h in heads]
    s = [jnp.where(dead, NEG, s[h]) for h in heads]
    mx = [jnp.maximum(jnp.max(s[h], axis=-1, keepdims=True), sink_ref[h]) for h in heads]
    p = [jnp.exp(s[h] - mx[h]) for h in heads]
    den = [jnp.sum(p[h], axis=-1, keepdims=True) + jnp.exp(sink_ref[h] - mx[h]) for h in heads]
    o = [_dot(p[h].astype(BF), vw[h // GQA]) for h in heads]
    for h in heads:
        o_ref[:, hsl[h]] = (o[h] / den[h]).astype(o_ref.dtype)


def _attention(proj, bias, qn, kn, sink, segs):
    m = proj.shape[0]
    nb = m // BLOCK
    aw = N_Q_HEADS * HEAD_DIM
    kvw = N_KV_HEADS * HEAD_DIM
    kcol = aw // kvw
    vcol = kcol + 1
    prev = lambda n: jnp.maximum(n - 1, 0)
    nxt = lambda n: jnp.minimum(n + 1, nb - 1)
    specs = [pl.BlockSpec((BLOCK, aw), lambda n: (n, 0))]
    for c in (kcol, vcol):
        specs += [pl.BlockSpec((BLOCK, kvw), lambda n, c=c: (prev(n), c)),
                  pl.BlockSpec((BLOCK, kvw), lambda n, c=c: (n, c)),
                  pl.BlockSpec((BLOCK, kvw), lambda n, c=c: (nxt(n), c))]
    specs += [pl.BlockSpec((N_Q_HEADS, BLOCK, 3 * BLOCK), lambda n: (0, 0, 0)),
              pl.BlockSpec((1, HEAD_DIM), lambda n: (0, 0)),
              pl.BlockSpec((1, HEAD_DIM), lambda n: (0, 0)),
              pl.BlockSpec(memory_space=pltpu.SMEM)]
    return pl.pallas_call(
        functools.partial(_attn_kernel, segs),
        grid=(nb,),
        in_specs=specs,
        out_specs=pl.BlockSpec((BLOCK, aw), lambda n: (n, 0)),
        out_shape=jax.ShapeDtypeStruct((m, aw), BF),
        compiler_params=_cp("parallel"),
        name="window_attn",
    )(proj, proj, proj, proj, proj, proj, proj, bias, qn, kn, sink)


def _prep_kernel(segs, tt, *refs):
    (pr, pr_p, pr_n, pk, pk_p, pk_n, pv, pv_p, pv_n, px, px_p, px_n,
     spr, snr, spk, snk, spv, snv, spx, snx, w0, a0, wl, g2, kk_g, ka_g, rk_g, bd, tri) = refs[:29]
    outs = refs[29:]
    zouts = (outs[0:5], outs[5:10])
    vo, bvo, go = outs[10:13]

    row0 = pl.program_id(0) * tt
    pos, tlen = _seq_info(row0, segs)
    first = pos == 0
    last = pos + tt == tlen
    rows = lax.broadcasted_iota(jnp.int32, (tt, 1), 0)

    def shifted(x_ref, p_ref, n_ref, sp, sn):
        x = x_ref[...]
        prow = jnp.where(first, 0.0, p_ref[7:8, :])
        nrow = jnp.where(last, 0.0, n_ref[0:1, :])
        prev = jnp.where(rows == 0, prow, pltpu.roll(x, 1, 0))
        nxt = jnp.where(rows == tt - 1, nrow, pltpu.roll(x, tt - 1, 0))
        return x + sp[...] * (prev - x) + sn[...] * (nxt - x)

    r = shifted(pr, pr_p, pr_n, spr, snr)
    k = shifted(pk, pk_p, pk_n, spk, snk)
    v = shifted(pv, pv_p, pv_n, spv, snv)
    x = shifted(px, px_p, px_n, spx, snx)

    xl = x[:, :LORA_W]
    lane = lax.broadcasted_iota(jnp.int32, xl.shape, 1)
    lx = jnp.where(lane < 2 * DECAY_LORA, jnp.tanh(xl), xl).astype(BF)
    sg = jax.nn.sigmoid(x[:, LORA_W:LORA_W + GATE_LORA]).astype(BF)
    go[...] = _dot(sg, g2[...]).astype(go.dtype)

    kkr = k * kk_g[...]
    n2 = _dot((kkr * kkr).astype(BF), bd[...])
    kk = kkr / jnp.maximum(jnp.sqrt(n2), 1e-12)

    kd_sum = None
    for z in (0, 1):
        at_o, rt_o, bt_o, kt_o, ct_o = zouts[z]
        wraw = w0[z:z + 1, :] + _dot(lx, wl[z])
        a = jax.nn.sigmoid(a0[z:z + 1, :] + _dot(lx, wl[2 + z]))
        lw = -math.exp(-0.5) * jax.nn.sigmoid(wraw)
        hi, lo = _split2(lw)
        cin = _dot(tri[z], hi) + _dot(tri[z], lo)
        kd = k * (1.0 + (a - 1.0) * ka_g[...])
        b = kk * a
        e_neg = jnp.exp(-cin)
        at_o[...] = (-kk * jnp.exp(cin - lw)).astype(BF)
        rt_o[...] = (r * jnp.exp(cin)).astype(BF)
        bt_o[...] = (b * e_neg).astype(BF)
        kt_o[...] = (kd * e_neg).astype(BF)
        end = CHUNK - 1 if z == 0 else 0
        ct_o[...] = jnp.concatenate(
            [jnp.broadcast_to(cin[c * CHUNK + end:c * CHUNK + end + 1], (8, cin.shape[1]))
             for c in range(tt // CHUNK)], axis=0)
        kd_sum = kd if kd_sum is None else kd_sum + kd

    bonus = _dot((r * kd_sum * rk_g[...]).astype(BF), bd[...])
    bvo[...] = (bonus * v).astype(bvo.dtype)
    vo[...] = v.astype(BF)


def _prep(proj, p, segs, tt, cb):
    m = proj.shape[0]
    c = p["k_k"].shape[1]
    rcol0 = (N_Q_HEADS + 2 * N_KV_HEADS) * HEAD_DIM
    nrow8 = m // 8
    t8 = tt // 8

    def trio(width, colfn):
        return [pl.BlockSpec((tt, width), lambda i, j: (i, colfn(j))),
                pl.BlockSpec((8, width), lambda i, j: (jnp.maximum(i * t8 - 1, 0), colfn(j))),
                pl.BlockSpec((8, width), lambda i, j: (jnp.minimum((i + 1) * t8, nrow8 - 1), colfn(j)))]

    specs = []
    for sec in range(3):
        base = (rcol0 + sec * c) // cb
        specs += trio(cb, lambda j, base=base: base + j)
    xblk = (rcol0 + 3 * c) // XTRA_W
    specs += trio(XTRA_W, lambda j: xblk)
    vec = lambda: pl.BlockSpec((1, cb), lambda i, j: (0, j))
    specs += [vec(), vec(), vec(), vec(), vec(), vec(),
              pl.BlockSpec((1, XTRA_W), lambda i, j: (0, 0)),
              pl.BlockSpec((1, XTRA_W), lambda i, j: (0, 0)),
              pl.BlockSpec((2, cb), lambda i, j: (0, j)),
              pl.BlockSpec((2, cb), lambda i, j: (0, j)),
              pl.BlockSpec((4, LORA_W, cb), lambda i, j: (0, 0, j)),
              pl.BlockSpec((GATE_LORA, cb), lambda i, j: (0, j)),
              vec(), vec(), vec(),
              pl.BlockSpec((cb, cb), lambda i, j: (0, 0)),
              pl.BlockSpec((2, tt, tt), lambda i, j: (0, 0, 0))]
    big = pl.BlockSpec((tt, cb), lambda i, j: (i, j))
    small = pl.BlockSpec((t8, cb), lambda i, j: (i, j))
    out_specs = ([big] * 4 + [small]) * 2 + [big] * 3
    big_s = jax.ShapeDtypeStruct((m, c), BF)
    small_s = jax.ShapeDtypeStruct((nrow8, c), F32)
    out_shape = ([big_s] * 4 + [small_s]) * 2 + [big_s] * 3
    args = [proj] * 12 + [p["sp_r"], p["sn_r"], p["sp_k"], p["sn_k"], p["sp_v"], p["sn_v"],
                          p["sp_x"], p["sn_x"], p["w0"], p["a0"], p["wl"], p["g2"],
                          p["k_k"], p["k_a"], p["r_k"], p["bd"], p["tri"]]
    return pl.pallas_call(
        functools.partial(_prep_kernel, segs, tt),
        grid=(m // tt, c // cb),
        in_specs=specs,
        out_specs=out_specs,
        out_shape=out_shape,
        compiler_params=_cp("parallel", "arbitrary"),
        name="rwkv_prep",
    )(*args)


def _scan_kernel(z, segs, ts, npair, at, rt, bt, kt, v, ct, y_ref, h_ref):
    i = pl.program_id(1)
    nblk = pl.num_programs(1)
    blk = i if z == 0 else nblk - 1 - i
    pos, tlen = _seq_info(blk * ts, segs)
    reset = (pos == 0) if z == 0 else (pos + ts == tlen)

    @pl.when(reset)
    def _():
        h_ref[...] = jnp.zeros_like(h_ref)

    n = 2 * CHUNK
    lane = lax.broadcasted_iota(jnp.int32, (CHUNK, LANE), 1)
    m_lo = jnp.where(lane < RWKV_HEAD, 1.0, 0.0).astype(BF)
    m_hi = jnp.where(lane < RWKV_HEAD, 0.0, 1.0).astype(BF)
    row = lax.broadcasted_iota(jnp.int32, (n, n), 0)
    col = lax.broadcasted_iota(jnp.int32, (n, n), 1)
    strict = (row > col) if z == 0 else (row < col)
    incl = (row >= col) if z == 0 else (row <= col)
    blk16 = (row // 16) == (col // 16)
    off32 = ((row // 32) == (col // 32)) & ((row // 16) != (col // 16))
    off64 = ((row // 64) == (col // 64)) & ((row // 32) != (col // 32))
    eye = jnp.where(row == col, 1.0, 0.0).astype(F32)

    nch = ts // CHUNK
    order = list(range(nch)) if z == 0 else list(range(nch - 1, -1, -1))
    items = [(p, c) for c in order for p in range(npair)]

    def tile(ref, p, c):
        return ref[c * CHUNK:(c + 1) * CHUNK, p * LANE:(p + 1) * LANE]

    def stack(x):
        return jnp.concatenate([x * m_lo, x * m_hi], axis=0)

    def each(fn, *lists):
        return [fn(*args) for args in zip(*lists)]

    xa = [stack(tile(at, p, c)) for p, c in items]
    xr = [stack(tile(rt, p, c)) for p, c in items]
    xb = [stack(tile(bt, p, c)) for p, c in items]
    xk = [stack(tile(kt, p, c)) for p, c in items]
    vs = [stack(tile(v, p, c)) for p, c in items]
    ctr = [ct[c * 8:c * 8 + 1, p * LANE:(p + 1) * LANE] for p, c in items]
    grow = [jnp.broadcast_to(jnp.exp(r), (n, LANE)).astype(BF) for r in ctr]
    xbg = each(lambda x, g: x * g, xb, grow)
    xkg = each(lambda x, g: x * g, xk, grow)
    gam = [jnp.exp(jnp.transpose(jnp.broadcast_to(r, (n, n)))) for r in ctr]

    pm = each(lambda a, r, b, k: _dot_nt(jnp.concatenate([a, r], axis=0), jnp.concatenate([b, k], axis=0)),
              xa, xr, xb, xk)
    a_ab = [jnp.where(strict, m[:n, :n], 0.0) for m in pm]
    a_ak = [jnp.where(strict, m[:n, n:], 0.0).astype(BF) for m in pm]
    a_rb = [jnp.where(incl, m[n:, :n], 0.0).astype(BF) for m in pm]
    a_rk = [jnp.where(incl, m[n:, n:], 0.0).astype(BF) for m in pm]

    ad = [jnp.where(blk16, a, 0.0) for a in a_ab]
    pk = [a.astype(BF) for a in ad]
    tinv = [eye + a for a in ad]
    pk = each(lambda a: _dot(a, a).astype(BF), pk)
    av = each(lambda a, w: _dot(a, w).astype(BF), a_ak, vs)
    for _ in range(2):
        both = each(lambda a, t: _dot(a, jnp.concatenate([a, t.astype(BF)], axis=1)), pk, tinv)
        pk = [m[:, :n].astype(BF) for m in both]
        tinv = each(lambda t, m: t + m[:, n:], tinv, both)
    tinv = each(lambda t, a: t + _dot(a, t.astype(BF)), tinv, pk)
    for off in (off32, off64):
        tb = [t.astype(BF) for t in tinv]
        ao = [jnp.where(off, a, 0.0).astype(BF) for a in a_ab]
        mid = each(lambda t, a: _dot(t, a).astype(BF), tb, ao)
        tinv = each(lambda t, m_, t_b: t + _dot(m_, t_b), tinv, mid, tb)

    wq = each(lambda t, a, q: _dot(t.astype(BF), jnp.concatenate([a, q], axis=1)).astype(BF), tinv, xa, av)
    zero = jnp.zeros((n, n), BF)
    wqv = each(lambda w, v_: jnp.concatenate([w, jnp.concatenate([zero, v_], axis=1)], axis=0), wq, vs)
    rb = each(lambda b, k, w: _dot(jnp.concatenate([b, k], axis=1), w), a_rb, a_rk, wqv)
    bgw = each(lambda b, k, w: _dot_tn(jnp.concatenate([b, k], axis=0), w), xbg, xkg, wqv)
    ry = each(lambda r, m_: (r.astype(F32) + m_[:, :n]).astype(BF), xr, rb)
    qy = [m_[:, n:] for m_ in rb]
    gm = [m_[:, :n].astype(BF) for m_ in bgw]
    jm = [m_[:, n:] for m_ in bgw]

    hs = [h_ref[p] for p in range(npair)]
    for idx, (p, c) in enumerate(items):
        hb = hs[p].astype(BF)
        yst = _dot(ry[idx], hb) + qy[idx]
        y_ref[c * CHUNK:(c + 1) * CHUNK, p * LANE:(p + 1) * LANE] = yst[:CHUNK] + yst[CHUNK:]
        hs[p] = gam[idx] * hs[p] + _dot(gm[idx], hb) + jm[idx]
    for p in range(npair):
        h_ref[p] = hs[p]


def _scan(z, ops, v, ct, segs, ts, cs):
    m, c = v.shape
    nblk = m // ts
    npair = cs // LANE
    rowmap = (lambda j, i: (i, j)) if z == 0 else (lambda j, i: (nblk - 1 - i, j))
    big = pl.BlockSpec((ts, cs), rowmap)
    small = pl.BlockSpec((ts // 8, cs), rowmap)
    return pl.pallas_call(
        functools.partial(_scan_kernel, z, segs, ts, npair),
        grid=(c // cs, nblk),
        in_specs=[big] * 5 + [small],
        out_specs=big,
        out_shape=jax.ShapeDtypeStruct((m, c), F32),
        scratch_shapes=[pltpu.VMEM((npair, LANE, LANE), F32)],
        compiler_params=_cp("parallel", "arbitrary"),
        name="rwkv_scan_fwd" if z == 0 else "rwkv_scan_bwd",
    )(*ops, v, ct)


def _post_kernel(y0, y1, bv, g, lw, lb, bd, o_ref):
    y = y0[...] + y1[...]
    inv_n = 1.0 / RWKV_HEAD
    hi, lo = _split2(y)
    mu = (_dot(hi, bd[...]) + _dot(lo, bd[...])) * inv_n
    d = y - mu
    var = _dot((d * d).astype(BF), bd[...]) * inv_n
    yn = d * lax.rsqrt(var + LNX_EPS) * lw[...] + lb[...]
    o_ref[...] = ((yn + bv[...].astype(F32)) * g[...].astype(F32)).astype(o_ref.dtype)


def _post(y0, y1, bv, g, lw, lb, bd, tt, cb):
    m, c = y0.shape
    big = pl.BlockSpec((tt, cb), lambda i, j: (i, j))
    vec = pl.BlockSpec((1, cb), lambda i, j: (0, j))
    return pl.pallas_call(
        _post_kernel,
        grid=(m // tt, c // cb),
        in_specs=[big, big, big, big, vec, vec, pl.BlockSpec((cb, cb), lambda i, j: (0, 0))],
        out_specs=big,
        out_shape=jax.ShapeDtypeStruct((m, c), BF),
        compiler_params=_cp("parallel", "parallel"),
        name="rwkv_post",
    )(y0, y1, bv, g, lw, lb, bd)


def _outproj_kernel(ka, a_ref, b_ref, w_ref, x_ref, *rest):
    o_ref = rest[-1]
    acc = _dot(a_ref[...], w_ref[:ka, :]) + _dot(b_ref[...], w_ref[ka:, :])
    o_ref[...] = x_ref[...] + acc


def _outproj(att, rw, w, xs, tm, tn):
    m, ka = att.shape
    kb = rw.shape[1]
    n = w.shape[1]

    def call(x, t0, prev):
        carried = [] if prev is None else [prev]
        return pl.pallas_call(
            functools.partial(_outproj_kernel, ka),
            grid=(x.shape[0] // tm, n // tn),
            in_specs=[pl.BlockSpec((tm, ka), lambda i, j: (t0 + i, 0)),
                      pl.BlockSpec((tm, kb), lambda i, j: (t0 + i, 0)),
                      pl.BlockSpec((ka + kb, tn), lambda i, j: (0, j)),
                      pl.BlockSpec((tm, tn), lambda i, j: (i, j))]
            + [pl.BlockSpec(memory_space=pl.ANY)] * len(carried),
            out_specs=pl.BlockSpec((tm, tn), lambda i, j: (t0 + i, j)),
            out_shape=jax.ShapeDtypeStruct((m, n), F32),
            input_output_aliases={4: 0} if carried else {},
            compiler_params=_cp("parallel", "parallel"),
            name="out_proj",
        )(att, rw, w, x, *carried)

    return _over_parts(call, xs, tm)


def _memkv_kernel(m_ref, g_ref, wk_ref, wv_ref, kn_ref, k_out, v_out):
    mm = _rms_rows(m_ref[...], g_ref[...]).astype(BF)
    k = _dot(mm, wk_ref[...])
    for h in range(MEM_HEADS):
        hs = slice(h * MEM_HEAD_DIM, (h + 1) * MEM_HEAD_DIM)
        k_out[:, hs] = _rms_rows(k[:, hs], kn_ref[...]).astype(BF)
    v_out[...] = _dot(mm, wv_ref[...]).astype(BF)


def _memkv(mem, g, wk, wv, kn):
    rows, d = mem.shape
    w = wk.shape[1]
    full = lambda shape: pl.BlockSpec(shape, lambda i: (0, 0))
    return pl.pallas_call(
        _memkv_kernel,
        grid=(rows // N_MEM,),
        in_specs=[pl.BlockSpec((N_MEM, d), lambda i: (i, 0)), full((1, d)), full((d, w)), full((d, w)),
                  full((1, MEM_HEAD_DIM))],
        out_specs=[pl.BlockSpec((N_MEM, w), lambda i: (i, 0))] * 2,
        out_shape=[jax.ShapeDtypeStruct((rows, w), BF)] * 2,
        compiler_params=_cp("parallel"),
        name="mem_kv",
    )(mem, g, wk, wv, kn)


def _memattn_kernel(x_ref, k_ref, v_ref, gm_ref, wq_ref, qn_ref, wo_ref, gf_ref, x_out, h_out):
    x = x_ref[...]
    h = _rms_rows(x, gm_ref[...]).astype(BF)
    q = _dot(h, wq_ref[...])
    scale = MEM_HEAD_DIM ** -0.5
    outs = []
    for hd in range(MEM_HEADS):
        hs = slice(hd * MEM_HEAD_DIM, (hd + 1) * MEM_HEAD_DIM)
        qh = _rms_rows(q[:, hs], qn_ref[...]).astype(BF)
        s = _dot_nt(qh, k_ref[:, hs]) * scale
        p = jnp.exp(s - jnp.max(s, axis=-1, keepdims=True))
        den = jnp.sum(p, axis=-1, keepdims=True)
        outs.append((_dot(p.astype(BF), v_ref[:, hs]) / den).astype(BF))
    o = jnp.concatenate(outs, axis=1)
    x2 = x + _dot(o, wo_ref[...])
    x_out[...] = x2
    h_out[...] = _rms_rows(x2, gf_ref[...]).astype(BF)


def _memattn(x, k, v, gm, wq, qn, wo, gf, segs, tm):
    m, d = x.shape
    w = wq.shape[1]
    full = lambda shape: pl.BlockSpec(shape, lambda i: (0, 0))
    kv = pl.BlockSpec((N_MEM, w), lambda i: (_batch_of(i * tm, segs), 0))
    row = pl.BlockSpec((tm, d), lambda i: (i, 0))
    return pl.pallas_call(
        _memattn_kernel,
        grid=(m // tm,),
        in_specs=[row, kv, kv, full((1, d)), full((d, w)), full((1, MEM_HEAD_DIM)), full((w, d)), full((1, d))],
        out_specs=[row, row],
        out_shape=[jax.ShapeDtypeStruct((m, d), F32), jax.ShapeDtypeStruct((m, d), BF)],
        compiler_params=_cp("parallel"),
        name="mem_attn",
    )(x, k, v, gm, wq, qn, wo, gf)


def _ffn_up_kernel(h_ref, wg_ref, wu_ref, o_ref):
    h = h_ref[...]
    gate = _dot(h, wg_ref[...])
    up = _dot(h, wu_ref[...])
    o_ref[...] = (gate * jax.nn.sigmoid(gate) * up).astype(o_ref.dtype)


def _ffn_up(h, wg, wu, tm, tn):
    m, d = h.shape
    f = wg.shape[1]
    wspec = pl.BlockSpec((d, tn), lambda i, j: (0, j))
    return pl.pallas_call(
        _ffn_up_kernel,
        grid=(m // tm, f // tn),
        in_specs=[pl.BlockSpec((tm, d), lambda i, j: (i, 0)), wspec, wspec],
        out_specs=pl.BlockSpec((tm, tn), lambda i, j: (i, j)),
        out_shape=jax.ShapeDtypeStruct((m, f), BF),
        compiler_params=_cp("parallel", "parallel"),
        name="ffn_up",
    )(h, wg, wu)


def _ffn_down_kernel(a_ref, w_ref, x_ref, o_ref):
    o_ref[...] = x_ref[...] + _dot(a_ref[...], w_ref[...])


def _ffn_down(act, wd, x, tm, tn, row0=0, rows=None):
    f = act.shape[1]
    n = wd.shape[1]
    rows = act.shape[0] if rows is None else rows
    assert row0 % tm == 0 and rows % tm == 0
    t0 = row0 // tm
    return pl.pallas_call(
        _ffn_down_kernel,
        grid=(rows // tm, n // tn),
        in_specs=[pl.BlockSpec((tm, f), lambda i, j: (t0 + i, 0)),
                  pl.BlockSpec((f, tn), lambda i, j: (0, j)),
                  pl.BlockSpec((tm, tn), lambda i, j: (t0 + i, j))],
        out_specs=pl.BlockSpec((tm, tn), lambda i, j: (i, j)),
        out_shape=jax.ShapeDtypeStruct((rows, n), F32),
        compiler_params=_cp("parallel", "parallel"),
        name="ffn_down",
    )(act, wd, x)


def _rel_bucket(rel):
    half = N_REL_BUCKETS // 2
    exact = half // 2
    n = np.abs(rel)
    large = exact + (np.log(np.maximum(n, 1) / exact) / np.log(REL_MAX_DIST / exact)
                     * (half - exact)).astype(np.int32)
    large = np.minimum(large, half - 1)
    return (rel > 0).astype(np.int32) * half + np.where(n < exact, n, large)


def _attn_bias(rel_bias):
    qi = np.arange(BLOCK)[:, None]
    kj = np.arange(3 * BLOCK)[None, :]
    rel = kj - BLOCK - qi
    onehot = (_rel_bucket(rel)[..., None] == np.arange(N_REL_BUCKETS)).astype(np.float32)
    bias = jnp.einsum("qkb,bh->hqk", jnp.asarray(onehot), rel_bias.astype(F32), precision=lax.Precision.HIGHEST)
    return jnp.where(jnp.asarray(np.abs(rel) <= BLOCK)[None], bias, NEG)


def _tri_consts(tt):
    t = np.arange(tt)[:, None]
    s = np.arange(tt)[None, :]
    same = (t // CHUNK) == (s // CHUNK)
    mats = [same & (s <= t), same & (s >= t)]
    return jnp.asarray(np.stack(mats).astype(np.float32), dtype=BF)


def _block_diag_ones(cb):
    i = np.arange(cb)
    return jnp.asarray(((i[:, None] // RWKV_HEAD) == (i[None, :] // RWKV_HEAD)).astype(np.float32), dtype=BF)


def _pick(n, pref):
    t = min(n, pref)
    assert n % t == 0, (n, t)
    return t


def kernel(x_prompt, x_sample, mem_prompt, mem_sample, rel_bias, norm_mix, w_in, q_norm, k_norm, sink,
           shift_prev, shift_next, w0, w2, a0, a2, g2, k_k, k_a, r_k, lnx_w, lnx_b, w_out,
           norm_mem, norm_memkv, wq_mem, wk_mem, wv_mem, wo_mem, qn_mem, kn_mem,
           norm_ffn, w_gate, w_up, w_down):
    b1, t1, d = x_prompt.shape
    b2, t2, _ = x_sample.shape
    segs = ((b1, t1), (b2, t2))
    m1 = b1 * t1
    m = m1 + b2 * t2
    xs = [x_prompt.reshape(m1, d), x_sample.reshape(b2 * t2, d)]
    mem = jnp.concatenate([mem_prompt.reshape(-1, d), mem_sample.reshape(-1, d)], axis=0)
    depth = w_in.shape[0]
    c = k_k.shape[1]
    att_cols = (N_Q_HEADS + 2 * N_KV_HEADS) * HEAD_DIM
    lora0 = 3 * c

    tm = _pick(m, 512)
    tt = _pick(min(t1, t2), 256)
    cb = _pick(c, 512)
    ts = _pick(min(t1, t2), 256)
    cs = _pick(c, 512)
    tm_up = _pick(m, 1024)
    tm_mem = _pick(min(t1, t2), 256)
    tn_ffn = 256

    bias = _attn_bias(rel_bias)
    tri = _tri_consts(tt)
    bd = _block_diag_ones(cb)
    row = lambda a: a.reshape(1, -1).astype(F32)

    for l in range(depth):
        w_in_p = jnp.pad(w_in[l], ((0, 0), (0, XTRA_W - LORA_W - GATE_LORA))).astype(BF)
        sp, sn = shift_prev[l], shift_next[l]
        padx = lambda a: jnp.pad(a[lora0:], (0, XTRA_W - LORA_W - GATE_LORA)).reshape(1, -1)
        wl = jnp.zeros((4, LORA_W, c), F32)
        wl = wl.at[0, 0:DECAY_LORA].set(w2[l, 0]).at[1, DECAY_LORA:2 * DECAY_LORA].set(w2[l, 1])
        wl = wl.at[2, 2 * DECAY_LORA:2 * DECAY_LORA + AAA_LORA].set(a2[l, 0])
        wl = wl.at[3, 2 * DECAY_LORA + AAA_LORA:].set(a2[l, 1])
        prm = dict(sp_r=row(sp[0:c]), sn_r=row(sn[0:c]), sp_k=row(sp[c:2 * c]), sn_k=row(sn[c:2 * c]),
                   sp_v=row(sp[2 * c:3 * c]), sn_v=row(sn[2 * c:3 * c]), sp_x=padx(sp), sn_x=padx(sn),
                   w0=w0[l], a0=a0[l], wl=wl.astype(BF), g2=g2[l].astype(BF),
                   k_k=row(k_k[l]), k_a=row(k_a[l]), r_k=row(r_k[l]), bd=bd, tri=tri)

        proj = _inproj(xs, row(norm_mix[l]), w_in_p, tm, 768, m)
        att = _attention(proj, bias, row(q_norm[l]), row(k_norm[l]), sink[l].astype(F32), segs)
        po = _prep(proj, prm, segs, tt, cb)
        v_b, bv, g = po[10], po[11], po[12]
        y0 = _scan(0, po[0:4], v_b, po[4], segs, ts, cs)
        y1 = _scan(1, po[5:9], v_b, po[9], segs, ts, cs)
        rw = _post(y0, y1, bv, g, row(lnx_w[l]), row(lnx_b[l]), bd, tt, cb)
        x = _outproj(att, rw, w_out[l].astype(BF), xs, tm, 1024)

        km, vm = _memkv(mem, row(norm_memkv[l]), wk_mem[l].astype(BF), wv_mem[l].astype(BF), row(kn_mem[l]))
        x, h3 = _memattn(x, km, vm, row(norm_mem[l]), wq_mem[l].astype(BF), row(qn_mem[l]),
                         wo_mem[l].astype(BF), row(norm_ffn[l]), segs, tm_mem)

        act = _ffn_up(h3, w_gate[l].astype(BF), w_up[l].astype(BF), tm_up, tn_ffn)
        wd = w_down[l].astype(BF)
        if l + 1 < depth:
            xs = [_ffn_down(act, wd, x, tm, tn_ffn)]
    y1 = _ffn_down(act, wd, x, tm, tn_ffn, 0, m1)
    y2 = _ffn_down(act, wd, x, tm, tn_ffn, m1, m - m1)
    return (y1.reshape(b1, t1, d), y2.reshape(b2, t2, d))
```

```python
import functools
import math

import numpy as np
import jax
import jax.numpy as jnp
from jax import lax
from jax.experimental import pallas as pl
from jax.experimental.pallas import tpu as pltpu

F32 = jnp.float32
BF = jnp.bfloat16

HEAD_DIM = 128
N_Q_HEADS = 16
N_KV_HEADS = 4
GQA = N_Q_HEADS // N_KV_HEADS
BLOCK = 128
N_REL_BUCKETS = 32
REL_MAX_DIST = 128
RWKV_HEAD = 64
DECAY_LORA = 96
AAA_LORA = 96
GATE_LORA = 256
LNX_EPS = 64e-5
MEM_HEADS = 4
MEM_HEAD_DIM = 128
N_MEM = 256
RMS_EPS = 1e-6
NEG = -1e30

LANE = 128
CHUNK = 64
LORA_W = 2 * DECAY_LORA + 2 * AAA_LORA
XTRA_W = 768
VMEM_LIMIT = 56 * 1024 * 1024


def _cp(*sem):
    return pltpu.CompilerParams(dimension_semantics=sem, vmem_limit_bytes=VMEM_LIMIT)


def _dot(a, b):
    return jnp.dot(a, b, preferred_element_type=F32)


def _dot_nt(a, b):
    return lax.dot_general(a, b, (((1,), (1,)), ((), ())), preferred_element_type=F32)


def _dot_tn(a, b):
    return lax.dot_general(a, b, (((0,), (0,)), ((), ())), preferred_element_type=F32)


def _split2(x):
    hi = x.astype(BF)
    lo = (x - hi.astype(F32)).astype(BF)
    return hi, lo


def _rms_rows(x, g):
    ms = jnp.mean(x * x, axis=-1, keepdims=True)
    return x * lax.rsqrt(ms + RMS_EPS) * g


def _seq_info(row, segs):
    off = 0
    pos = None
    tlen = None
    for n, t in segs:
        p = lax.rem(row - off, t)
        if pos is None:
            pos, tlen = p, jnp.int32(t)
        else:
            inside = row >= off
            pos = jnp.where(inside, p, pos)
            tlen = jnp.where(inside, t, tlen)
        off += n * t
    return pos, tlen


def _batch_of(row, segs):
    off = 0
    boff = 0
    res = None
    for n, t in segs:
        b = boff + (row - off) // t
        res = b if res is None else jnp.where(row >= off, b, res)
        off += n * t
        boff += n
    return res


def _over_parts(call, parts, tm):
    out, t0 = None, 0
    for part in parts:
        assert part.shape[0] % tm == 0
        out = call(part, t0, out)
        t0 += part.shape[0] // tm
    return out


def _inproj_kernel(x_ref, g_ref, w_ref, *rest):
    o_ref, h_ref = rest[-2:]

    @pl.when(pl.program_id(1) == 0)
    def _():
        rows = x_ref.shape[0]
        step = min(rows, 256)
        for r0 in range(0, rows, step):
            h_ref[r0:r0 + step, :] = _rms_rows(x_ref[r0:r0 + step, :], g_ref[...]).astype(BF)

    o_ref[...] = _dot(h_ref[...], w_ref[...])


def _inproj(xs, g, w, tm, tn, m):
    d = xs[0].shape[1]
    n = w.shape[1]

    def call(x, t0, prev):
        carried = [] if prev is None else [prev]
        return pl.pallas_call(
            _inproj_kernel,
            grid=(x.shape[0] // tm, n // tn),
            in_specs=[pl.BlockSpec((tm, d), lambda i, j: (i, 0), pipeline_mode=pl.Buffered(1)),
                      pl.BlockSpec((1, d), lambda i, j: (0, 0)),
                      pl.BlockSpec((d, tn), lambda i, j: (0, j))]
            + [pl.BlockSpec(memory_space=pl.ANY)] * len(carried),
            out_specs=pl.BlockSpec((tm, tn), lambda i, j: (t0 + i, j)),
            out_shape=jax.ShapeDtypeStruct((m, n), F32),
            input_output_aliases={3: 0} if carried else {},
            scratch_shapes=[pltpu.VMEM((tm, d), BF)],
            compiler_params=_cp("parallel", "arbitrary"),
            name="in_proj",
        )(x, g, w, *carried)

    return _over_parts(call, xs, tm)


def _attn_kernel(segs, q_ref, kp_ref, kc_ref, kn_ref, vp_ref, vc_ref, vn_ref,
                 bias_ref, qn_ref, kn_g_ref, sink_ref, o_ref):
    n = pl.program_id(0)
    pos, tlen = _seq_info(n * BLOCK, segs)
    first = pos == 0
    last = pos + BLOCK == tlen
    col = lax.broadcasted_iota(jnp.int32, (BLOCK, 3 * BLOCK), 1)
    dead = (first & (col < BLOCK)) | (last & (col >= 2 * BLOCK))
    scale = HEAD_DIM ** -0.5
    heads = range(N_Q_HEADS)
    hsl = [slice(h * HEAD_DIM, (h + 1) * HEAD_DIM) for h in heads]
    kw, vw = [], []
    for kh in range(N_KV_HEADS):
        ls = hsl[kh]
        kcat = jnp.concatenate([kp_ref[:, ls], kc_ref[:, ls], kn_ref[:, ls]], axis=0)
        kw.append(_rms_rows(kcat, kn_g_ref[...]).astype(BF))
        vw.append(jnp.concatenate([vp_ref[:, ls], vc_ref[:, ls], vn_ref[:, ls]], axis=0).astype(BF))
    q = [_rms_rows(q_ref[:, hsl[h]], qn_ref[...]).astype(BF) for h in heads]
    s = [_dot_nt(q[h], kw[h // GQA]) * scale + bias_ref[h] for h in heads]
    s = [jnp.where(dead, NEG, s[h]) for h in heads]
    mx = [jnp.maximum(jnp.max(s[h], axis=-1, keepdims=True), sink_ref[h]) for h in heads]
    p = [jnp.exp(s[h] - mx[h]) for h in heads]
    den = [jnp.sum(p[h], axis=-1, keepdims=True) + jnp.exp(sink_ref[h] - mx[h]) for h in heads]
    o = [_dot(p[h].astype(BF), vw[h // GQA]) for h in heads]
    for h in heads:
        o_ref[:, hsl[h]] = (o[h] / den[h]).astype(o_ref.dtype)


def _attention(proj, bias, qn, kn, sink, segs):
    m = proj.shape[0]
    nb = m // BLOCK
    aw = N_Q_HEADS * HEAD_DIM
    kvw = N_KV_HEADS * HEAD_DIM
    kcol = aw // kvw
    vcol = kcol + 1
    prev = lambda n: jnp.maximum(n - 1, 0)
    nxt = lambda n: jnp.minimum(n + 1, nb - 1)
    specs = [pl.BlockSpec((BLOCK, aw), lambda n: (n, 0))]
    for c in (kcol, vcol):
        specs += [pl.BlockSpec((BLOCK, kvw), lambda n, c=c: (prev(n), c)),
                  pl.BlockSpec((BLOCK, kvw), lambda n, c=c: (n, c)),
                  pl.BlockSpec((BLOCK, kvw), lambda n, c=c: (nxt(n), c))]
    specs += [pl.BlockSpec((N_Q_HEADS, BLOCK, 3 * BLOCK), lambda n: (0, 0, 0)),
              pl.BlockSpec((1, HEAD_DIM), lambda n: (0, 0)),
              pl.BlockSpec((1, HEAD_DIM), lambda n: (0, 0)),
              pl.BlockSpec(memory_space=pltpu.SMEM)]
    return pl.pallas_call(
        functools.partial(_attn_kernel, segs),
        grid=(nb,),
        in_specs=specs,
        out_specs=pl.BlockSpec((BLOCK, aw), lambda n: (n, 0)),
        out_shape=jax.ShapeDtypeStruct((m, aw), BF),
        compiler_params=_cp("parallel"),
        name="window_attn",
    )(proj, proj, proj, proj, proj, proj, proj, bias, qn, kn, sink)


def _prep_kernel(segs, tt, *refs):
    (pr, pr_p, pr_n, pk, pk_p, pk_n, pv, pv_p, pv_n, px, px_p, px_n,
     spr, snr, spk, snk, spv, snv, spx, snx, w0, a0, wl, g2, kk_g, ka_g, rk_g, bd, tri) = refs[:29]
    outs = refs[29:]
    zouts = (outs[0:5], outs[5:10])
    vo, bvo, go = outs[10:13]

    row0 = pl.program_id(0) * tt
    pos, tlen = _seq_info(row0, segs)
    first = pos == 0
    last = pos + tt == tlen
    rows = lax.broadcasted_iota(jnp.int32, (tt, 1), 0)

    def shifted(x_ref, p_ref, n_ref, sp, sn):
        x = x_ref[...]
        prow = jnp.where(first, 0.0, p_ref[7:8, :])
        nrow = jnp.where(last, 0.0, n_ref[0:1, :])
        prev = jnp.where(rows == 0, prow, pltpu.roll(x, 1, 0))
        nxt = jnp.where(rows == tt - 1, nrow, pltpu.roll(x, tt - 1, 0))
        return x + sp[...] * (prev - x) + sn[...] * (nxt - x)

    r = shifted(pr, pr_p, pr_n, spr, snr)
    k = shifted(pk, pk_p, pk_n, spk, snk)
    v = shifted(pv, pv_p, pv_n, spv, snv)
    x = shifted(px, px_p, px_n, spx, snx)

    xl = x[:, :LORA_W]
    lane = lax.broadcasted_iota(jnp.int32, xl.shape, 1)
    lx = jnp.where(lane < 2 * DECAY_LORA, jnp.tanh(xl), xl).astype(BF)
    sg = jax.nn.sigmoid(x[:, LORA_W:LORA_W + GATE_LORA]).astype(BF)
    go[...] = _dot(sg, g2[...]).astype(go.dtype)

    kkr = k * kk_g[...]
    n2 = _dot((kkr * kkr).astype(BF), bd[...])
    kk = kkr / jnp.maximum(jnp.sqrt(n2), 1e-12)

    kd_sum = None
    for z in (0, 1):
        at_o, rt_o, bt_o, kt_o, ct_o = zouts[z]
        wraw = w0[z:z + 1, :] + _dot(lx, wl[z])
        a = jax.nn.sigmoid(a0[z:z + 1, :] + _dot(lx, wl[2 + z]))
        lw = -math.exp(-0.5) * jax.nn.sigmoid(wraw)
        hi, lo = _split2(lw)
        cin = _dot(tri[z], hi) + _dot(tri[z], lo)
        kd = k * (1.0 + (a - 1.0) * ka_g[...])
        b = kk * a
        e_neg = jnp.exp(-cin)
        at_o[...] = (-kk * jnp.exp(cin - lw)).astype(BF)
        rt_o[...] = (r * jnp.exp(cin)).astype(BF)
        bt_o[...] = (b * e_neg).astype(BF)
        kt_o[...] = (kd * e_neg).astype(BF)
        end = CHUNK - 1 if z == 0 else 0
        ct_o[...] = jnp.concatenate(
            [jnp.broadcast_to(cin[c * CHUNK + end:c * CHUNK + end + 1], (8, cin.shape[1]))
             for c in range(tt // CHUNK)], axis=0)
        kd_sum = kd if kd_sum is None else kd_sum + kd

    bonus = _dot((r * kd_sum * rk_g[...]).astype(BF), bd[...])
    bvo[...] = (bonus * v).astype(bvo.dtype)
    vo[...] = v.astype(BF)


def _prep(proj, p, segs, tt, cb):
    m = proj.shape[0]
    c = p["k_k"].shape[1]
    rcol0 = (N_Q_HEADS + 2 * N_KV_HEADS) * HEAD_DIM
    nrow8 = m // 8
    t8 = tt // 8

    def trio(width, colfn):
        return [pl.BlockSpec((tt, width), lambda i, j: (i, colfn(j))),
                pl.BlockSpec((8, width), lambda i, j: (jnp.maximum(i * t8 - 1, 0), colfn(j))),
                pl.BlockSpec((8, width), lambda i, j: (jnp.minimum((i + 1) * t8, nrow8 - 1), colfn(j)))]

    specs = []
    for sec in range(3):
        base = (rcol0 + sec * c) // cb
        specs += trio(cb, lambda j, base=base: base + j)
    xblk = (rcol0 + 3 * c) // XTRA_W
    specs += trio(XTRA_W, lambda j: xblk)
    vec = lambda: pl.BlockSpec((1, cb), lambda i, j: (0, j))
    specs += [vec(), vec(), vec(), vec(), vec(), vec(),
              pl.BlockSpec((1, XTRA_W), lambda i, j: (0, 0)),
              pl.BlockSpec((1, XTRA_W), lambda i, j: (0, 0)),
              pl.BlockSpec((2, cb), lambda i, j: (0, j)),
              pl.BlockSpec((2, cb), lambda i, j: (0, j)),
              pl.BlockSpec((4, LORA_W, cb), lambda i, j: (0, 0, j)),
              pl.BlockSpec((GATE_LORA, cb), lambda i, j: (0, j)),
              vec(), vec(), vec(),
              pl.BlockSpec((cb, cb), lambda i, j: (0, 0)),
              pl.BlockSpec((2, tt, tt), lambda i, j: (0, 0, 0))]
    big = pl.BlockSpec((tt, cb), lambda i, j: (i, j))
    small = pl.BlockSpec((t8, cb), lambda i, j: (i, j))
    out_specs = ([big] * 4 + [small]) * 2 + [big] * 3
    big_s = jax.ShapeDtypeStruct((m, c), BF)
    small_s = jax.ShapeDtypeStruct((nrow8, c), F32)
    out_shape = ([big_s] * 4 + [small_s]) * 2 + [big_s] * 3
    args = [proj] * 12 + [p["sp_r"], p["sn_r"], p["sp_k"], p["sn_k"], p["sp_v"], p["sn_v"],
                          p["sp_x"], p["sn_x"], p["w0"], p["a0"], p["wl"], p["g2"],
                          p["k_k"], p["k_a"], p["r_k"], p["bd"], p["tri"]]
    return pl.pallas_call(
        functools.partial(_prep_kernel, segs, tt),
        grid=(m // tt, c // cb),
        in_specs=specs,
        out_specs=out_specs,
        out_shape=out_shape,
        compiler_params=_cp("parallel", "arbitrary"),
        name="rwkv_prep",
    )(*args)


def _scan_kernel(z, segs, ts, npair, at, rt, bt, kt, v, ct, y_ref, h_ref):
    i = pl.program_id(1)
    nblk = pl.num_programs(1)
    blk = i if z == 0 else nblk - 1 - i
    pos, tlen = _seq_info(blk * ts, segs)
    reset = (pos == 0) if z == 0 else (pos + ts == tlen)

    @pl.when(reset)
    def _():
        h_ref[...] = jnp.zeros_like(h_ref)

    n = 2 * CHUNK
    lane = lax.broadcasted_iota(jnp.int32, (CHUNK, LANE), 1)
    m_lo = jnp.where(lane < RWKV_HEAD, 1.0, 0.0).astype(BF)
    m_hi = jnp.where(lane < RWKV_HEAD, 0.0, 1.0).astype(BF)
    row = lax.broadcasted_iota(jnp.int32, (n, n), 0)
    col = lax.broadcasted_iota(jnp.int32, (n, n), 1)
    strict = (row > col) if z == 0 else (row < col)
    incl = (row >= col) if z == 0 else (row <= col)
    blk16 = (row // 16) == (col // 16)
    off32 = ((row // 32) == (col // 32)) & ((row // 16) != (col // 16))
    off64 = ((row // 64) == (col // 64)) & ((row // 32) != (col // 32))
    eye = jnp.where(row == col, 1.0, 0.0).astype(F32)

    nch = ts // CHUNK
    order = list(range(nch)) if z == 0 else list(range(nch - 1, -1, -1))
    items = [(p, c) for c in order for p in range(npair)]

    def tile(ref, p, c):
        return ref[c * CHUNK:(c + 1) * CHUNK, p * LANE:(p + 1) * LANE]

    def stack(x):
        return jnp.concatenate([x * m_lo, x * m_hi], axis=0)

    def each(fn, *lists):
        return [fn(*args) for args in zip(*lists)]

    xa = [stack(tile(at, p, c)) for p, c in items]
    xr = [stack(tile(rt, p, c)) for p, c in items]
    xb = [stack(tile(bt, p, c)) for p, c in items]
    xk = [stack(tile(kt, p, c)) for p, c in items]
    vs = [stack(tile(v, p, c)) for p, c in items]
    ctr = [ct[c * 8:c * 8 + 1, p * LANE:(p + 1) * LANE] for p, c in items]
    grow = [jnp.broadcast_to(jnp.exp(r), (n, LANE)).astype(BF) for r in ctr]
    xbg = each(lambda x, g: x * g, xb, grow)
    xkg = each(lambda x, g: x * g, xk, grow)
    gam = [jnp.exp(jnp.transpose(jnp.broadcast_to(r, (n, n)))) for r in ctr]

    pm = each(lambda a, r, b, k: _dot_nt(jnp.concatenate([a, r], axis=0), jnp.concatenate([b, k], axis=0)),
              xa, xr, xb, xk)
    a_ab = [jnp.where(strict, m[:n, :n], 0.0) for m in pm]
    a_ak = [jnp.where(strict, m[:n, n:], 0.0).astype(BF) for m in pm]
    a_rb = [jnp.where(incl, m[n:, :n], 0.0).astype(BF) for m in pm]
    a_rk = [jnp.where(incl, m[n:, n:], 0.0).astype(BF) for m in pm]

    ad = [jnp.where(blk16, a, 0.0) for a in a_ab]
    pk = [a.astype(BF) for a in ad]
    tinv = [eye + a for a in ad]
    pk = each(lambda a: _dot(a, a).astype(BF), pk)
    av = each(lambda a, w: _dot(a, w).astype(BF), a_ak, vs)
    for _ in range(2):
        both = each(lambda a, t: _dot(a, jnp.concatenate([a, t.astype(BF)], axis=1)), pk, tinv)
        pk = [m[:, :n].astype(BF) for m in both]
        tinv = each(lambda t, m: t + m[:, n:], tinv, both)
    tinv = each(lambda t, a: t + _dot(a, t.astype(BF)), tinv, pk)
    for off in (off32, off64):
        tb = [t.astype(BF) for t in tinv]
        ao = [jnp.where(off, a, 0.0).astype(BF) for a in a_ab]
        mid = each(lambda t, a: _dot(t, a).astype(BF), tb, ao)
        tinv = each(lambda t, m_, t_b: t + _dot(m_, t_b), tinv, mid, tb)

    wq = each(lambda t, a, q: _dot(t.astype(BF), jnp.concatenate([a, q], axis=1)).astype(BF), tinv, xa, av)
    zero = jnp.zeros((n, n), BF)
    wqv = each(lambda w, v_: jnp.concatenate([w, jnp.concatenate([zero, v_], axis=1)], axis=0), wq, vs)
    rb = each(lambda b, k, w: _dot(jnp.concatenate([b, k], axis=1), w), a_rb, a_rk, wqv)
    bgw = each(lambda b, k, w: _dot_tn(jnp.concatenate([b, k], axis=0), w), xbg, xkg, wqv)
    ry = each(lambda r, m_: (r.astype(F32) + m_[:, :n]).astype(BF), xr, rb)
    qy = [m_[:, n:] for m_ in rb]
    gm = [m_[:, :n].astype(BF) for m_ in bgw]
    jm = [m_[:, n:] for m_ in bgw]

    hs = [h_ref[p] for p in range(npair)]
    for idx, (p, c) in enumerate(items):
        hb = hs[p].astype(BF)
        yst = _dot(ry[idx], hb) + qy[idx]
        y_ref[c * CHUNK:(c + 1) * CHUNK, p * LANE:(p + 1) * LANE] = yst[:CHUNK] + yst[CHUNK:]
        hs[p] = gam[idx] * hs[p] + _dot(gm[idx], hb) + jm[idx]
    for p in range(npair):
        h_ref[p] = hs[p]


def _scan(z, ops, v, ct, segs, ts, cs):
    m, c = v.shape
    nblk = m // ts
    npair = cs // LANE
    rowmap = (lambda j, i: (i, j)) if z == 0 else (lambda j, i: (nblk - 1 - i, j))
    big = pl.BlockSpec((ts, cs), rowmap)
    small = pl.BlockSpec((ts // 8, cs), rowmap)
    return pl.pallas_call(
        functools.partial(_scan_kernel, z, segs, ts, npair),
        grid=(c // cs, nblk),
        in_specs=[big] * 5 + [small],
        out_specs=big,
        out_shape=jax.ShapeDtypeStruct((m, c), F32),
        scratch_shapes=[pltpu.VMEM((npair, LANE, LANE), F32)],
        compiler_params=_cp("parallel", "arbitrary"),
        name="rwkv_scan_fwd" if z == 0 else "rwkv_scan_bwd",
    )(*ops, v, ct)


def _post_kernel(y0, y1, bv, g, lw, lb, bd, o_ref):
    y = y0[...] + y1[...]
    inv_n = 1.0 / RWKV_HEAD
    hi, lo = _split2(y)
    mu = (_dot(hi, bd[...]) + _dot(lo, bd[...])) * inv_n
    d = y - mu
    var = _dot((d * d).astype(BF), bd[...]) * inv_n
    yn = d * lax.rsqrt(var + LNX_EPS) * lw[...] + lb[...]
    o_ref[...] = ((yn + bv[...].astype(F32)) * g[...].astype(F32)).astype(o_ref.dtype)


def _post(y0, y1, bv, g, lw, lb, bd, tt, cb):
    m, c = y0.shape
    big = pl.BlockSpec((tt, cb), lambda i, j: (i, j))
    vec = pl.BlockSpec((1, cb), lambda i, j: (0, j))
    return pl.pallas_call(
        _post_kernel,
        grid=(m // tt, c // cb),
        in_specs=[big, big, big, big, vec, vec, pl.BlockSpec((cb, cb), lambda i, j: (0, 0))],
        out_specs=big,
        out_shape=jax.ShapeDtypeStruct((m, c), BF),
        compiler_params=_cp("parallel", "parallel"),
        name="rwkv_post",
    )(y0, y1, bv, g, lw, lb, bd)


def _outproj_kernel(ka, a_ref, b_ref, w_ref, x_ref, *rest):
    o_ref = rest[-1]
    acc = _dot(a_ref[...], w_ref[:ka, :]) + _dot(b_ref[...], w_ref[ka:, :])
    o_ref[...] = x_ref[...] + acc


def _outproj(att, rw, w, xs, tm, tn):
    m, ka = att.shape
    kb = rw.shape[1]
    n = w.shape[1]

    def call(x, t0, prev):
        carried = [] if prev is None else [prev]
        return pl.pallas_call(
            functools.partial(_outproj_kernel, ka),
            grid=(x.shape[0] // tm, n // tn),
            in_specs=[pl.BlockSpec((tm, ka), lambda i, j: (t0 + i, 0)),
                      pl.BlockSpec((tm, kb), lambda i, j: (t0 + i, 0)),
                      pl.BlockSpec((ka + kb, tn), lambda i, j: (0, j)),
                      pl.BlockSpec((tm, tn), lambda i, j: (i, j))]
            + [pl.BlockSpec(memory_space=pl.ANY)] * len(carried),
            out_specs=pl.BlockSpec((tm, tn), lambda i, j: (t0 + i, j)),
            out_shape=jax.ShapeDtypeStruct((m, n), F32),
            input_output_aliases={4: 0} if carried else {},
            compiler_params=_cp("parallel", "parallel"),
            name="out_proj",
        )(att, rw, w, x, *carried)

    return _over_parts(call, xs, tm)


def _memkv_kernel(m_ref, g_ref, wk_ref, wv_ref, kn_ref, k_out, v_out):
    mm = _rms_rows(m_ref[...], g_ref[...]).astype(BF)
    k = _dot(mm, wk_ref[...])
    for h in range(MEM_HEADS):
        hs = slice(h * MEM_HEAD_DIM, (h + 1) * MEM_HEAD_DIM)
        k_out[:, hs] = _rms_rows(k[:, hs], kn_ref[...]).astype(BF)
    v_out[...] = _dot(mm, wv_ref[...]).astype(BF)


def _memkv(mem, g, wk, wv, kn):
    rows, d = mem.shape
    w = wk.shape[1]
    full = lambda shape: pl.BlockSpec(shape, lambda i: (0, 0))
    return pl.pallas_call(
        _memkv_kernel,
        grid=(rows // N_MEM,),
        in_specs=[pl.BlockSpec((N_MEM, d), lambda i: (i, 0)), full((1, d)), full((d, w)), full((d, w)),
                  full((1, MEM_HEAD_DIM))],
        out_specs=[pl.BlockSpec((N_MEM, w), lambda i: (i, 0))] * 2,
        out_shape=[jax.ShapeDtypeStruct((rows, w), BF)] * 2,
        compiler_params=_cp("parallel"),
        name="mem_kv",
    )(mem, g, wk, wv, kn)


def _memattn_kernel(x_ref, k_ref, v_ref, gm_ref, wq_ref, qn_ref, wo_ref, gf_ref, x_out, h_out):
    x = x_ref[...]
    h = _rms_rows(x, gm_ref[...]).astype(BF)
    q = _dot(h, wq_ref[...])
    scale = MEM_HEAD_DIM ** -0.5
    heads = range(MEM_HEADS)
    hsl = [slice(hd * MEM_HEAD_DIM, (hd + 1) * MEM_HEAD_DIM) for hd in heads]
    qh = [_rms_rows(q[:, hsl[hd]], qn_ref[...]).astype(BF) for hd in heads]
    s = [_dot_nt(qh[hd], k_ref[:, hsl[hd]]) * scale for hd in heads]
    p = [jnp.exp(s[hd] - jnp.max(s[hd], axis=-1, keepdims=True)) for hd in heads]
    den = [jnp.sum(p[hd], axis=-1, keepdims=True) for hd in heads]
    pv = [_dot(p[hd].astype(BF), v_ref[:, hsl[hd]]) for hd in heads]
    o = jnp.concatenate([(pv[hd] / den[hd]).astype(BF) for hd in heads], axis=1)
    x2 = x + _dot(o, wo_ref[...])
    x_out[...] = x2
    h_out[...] = _rms_rows(x2, gf_ref[...]).astype(BF)


def _memattn(x, k, v, gm, wq, qn, wo, gf, segs, tm):
    m, d = x.shape
    w = wq.shape[1]
    full = lambda shape: pl.BlockSpec(shape, lambda i: (0, 0))
    kv = pl.BlockSpec((N_MEM, w), lambda i: (_batch_of(i * tm, segs), 0))
    row = pl.BlockSpec((tm, d), lambda i: (i, 0))
    return pl.pallas_call(
        _memattn_kernel,
        grid=(m // tm,),
        in_specs=[row, kv, kv, full((1, d)), full((d, w)), full((1, MEM_HEAD_DIM)), full((w, d)), full((1, d))],
        out_specs=[row, row],
        out_shape=[jax.ShapeDtypeStruct((m, d), F32), jax.ShapeDtypeStruct((m, d), BF)],
        compiler_params=_cp("parallel"),
        name="mem_attn",
    )(x, k, v, gm, wq, qn, wo, gf)


def _ffn_up_kernel(h_ref, wg_ref, wu_ref, o_ref):
    h = h_ref[...]
    gate = _dot(h, wg_ref[...])
    up = _dot(h, wu_ref[...])
    o_ref[...] = (gate * jax.nn.sigmoid(gate) * up).astype(o_ref.dtype)


def _ffn_up(h, wg, wu, tm, tn):
    m, d = h.shape
    f = wg.shape[1]
    wspec = pl.BlockSpec((d, tn), lambda i, j: (0, j))
    return pl.pallas_call(
        _ffn_up_kernel,
        grid=(m // tm, f // tn),
        in_specs=[pl.BlockSpec((tm, d), lambda i, j: (i, 0)), wspec, wspec],
        out_specs=pl.BlockSpec((tm, tn), lambda i, j: (i, j)),
        out_shape=jax.ShapeDtypeStruct((m, f), BF),
        compiler_params=_cp("parallel", "parallel"),
        name="ffn_up",
    )(h, wg, wu)


def _ffn_down_kernel(a_ref, w_ref, x_ref, o_ref):
    o_ref[...] = x_ref[...] + _dot(a_ref[...], w_ref[...])


def _ffn_down(act, wd, x, tm, tn, row0=0, rows=None):
    f = act.shape[1]
    n = wd.shape[1]
    rows = act.shape[0] if rows is None else rows
    assert row0 % tm == 0 and rows % tm == 0
    t0 = row0 // tm
    return pl.pallas_call(
        _ffn_down_kernel,
        grid=(rows // tm, n // tn),
        in_specs=[pl.BlockSpec((tm, f), lambda i, j: (t0 + i, 0)),
                  pl.BlockSpec((f, tn), lambda i, j: (0, j)),
                  pl.BlockSpec((tm, tn), lambda i, j: (t0 + i, j))],
        out_specs=pl.BlockSpec((tm, tn), lambda i, j: (i, j)),
        out_shape=jax.ShapeDtypeStruct((rows, n), F32),
        compiler_params=_cp("parallel", "parallel"),
        name="ffn_down",
    )(act, wd, x)


def _rel_bucket(rel):
    half = N_REL_BUCKETS // 2
    exact = half // 2
    n = np.abs(rel)
    large = exact + (np.log(np.maximum(n, 1) / exact) / np.log(REL_MAX_DIST / exact)
                     * (half - exact)).astype(np.int32)
    large = np.minimum(large, half - 1)
    return (rel > 0).astype(np.int32) * half + np.where(n < exact, n, large)


def _attn_bias(rel_bias):
    qi = np.arange(BLOCK)[:, None]
    kj = np.arange(3 * BLOCK)[None, :]
    rel = kj - BLOCK - qi
    onehot = (_rel_bucket(rel)[..., None] == np.arange(N_REL_BUCKETS)).astype(np.float32)
    bias = jnp.einsum("qkb,bh->hqk", jnp.asarray(onehot), rel_bias.astype(F32), precision=lax.Precision.HIGHEST)
    return jnp.where(jnp.asarray(np.abs(rel) <= BLOCK)[None], bias, NEG)


def _tri_consts(tt):
    t = np.arange(tt)[:, None]
    s = np.arange(tt)[None, :]
    same = (t // CHUNK) == (s // CHUNK)
    mats = [same & (s <= t), same & (s >= t)]
    return jnp.asarray(np.stack(mats).astype(np.float32), dtype=BF)


def _block_diag_ones(cb):
    i = np.arange(cb)
    return jnp.asarray(((i[:, None] // RWKV_HEAD) == (i[None, :] // RWKV_HEAD)).astype(np.float32), dtype=BF)


def _pick(n, pref):
    t = min(n, pref)
    assert n % t == 0, (n, t)
    return t


def kernel(x_prompt, x_sample, mem_prompt, mem_sample, rel_bias, norm_mix, w_in, q_norm, k_norm, sink,
           shift_prev, shift_next, w0, w2, a0, a2, g2, k_k, k_a, r_k, lnx_w, lnx_b, w_out,
           norm_mem, norm_memkv, wq_mem, wk_mem, wv_mem, wo_mem, qn_mem, kn_mem,
           norm_ffn, w_gate, w_up, w_down):
    b1, t1, d = x_prompt.shape
    b2, t2, _ = x_sample.shape
    segs = ((b1, t1), (b2, t2))
    m1 = b1 * t1
    m = m1 + b2 * t2
    xs = [x_prompt.reshape(m1, d), x_sample.reshape(b2 * t2, d)]
    mem = jnp.concatenate([mem_prompt.reshape(-1, d), mem_sample.reshape(-1, d)], axis=0)
    depth = w_in.shape[0]
    c = k_k.shape[1]
    lora0 = 3 * c

    mg = math.gcd(m1, m - m1)
    tm_in = _pick(mg, 1024)
    tn_in = 768
    tm_out, tn_out = _pick(mg, 1024), 1024
    tt = _pick(min(t1, t2), 256)
    cb = _pick(c, 512)
    ts = _pick(min(t1, t2), 256)
    cs = _pick(c, 512)
    tm_mem = _pick(min(t1, t2), 256)
    tm_up, tn_up = _pick(m, 2048), 256
    tm_down, tn_down = _pick(mg, 512), 512

    bias = _attn_bias(rel_bias)
    tri = _tri_consts(tt)
    bd = _block_diag_ones(cb)
    row = lambda a: a.reshape(1, -1).astype(F32)

    for l in range(depth):
        w_in_p = jnp.pad(w_in[l], ((0, 0), (0, XTRA_W - LORA_W - GATE_LORA))).astype(BF)
        sp, sn = shift_prev[l], shift_next[l]
        padx = lambda a: jnp.pad(a[lora0:], (0, XTRA_W - LORA_W - GATE_LORA)).reshape(1, -1)
        wl = jnp.zeros((4, LORA_W, c), F32)
        wl = wl.at[0, 0:DECAY_LORA].set(w2[l, 0]).at[1, DECAY_LORA:2 * DECAY_LORA].set(w2[l, 1])
        wl = wl.at[2, 2 * DECAY_LORA:2 * DECAY_LORA + AAA_LORA].set(a2[l, 0])
        wl = wl.at[3, 2 * DECAY_LORA + AAA_LORA:].set(a2[l, 1])
        prm = dict(sp_r=row(sp[0:c]), sn_r=row(sn[0:c]), sp_k=row(sp[c:2 * c]), sn_k=row(sn[c:2 * c]),
                   sp_v=row(sp[2 * c:3 * c]), sn_v=row(sn[2 * c:3 * c]), sp_x=padx(sp), sn_x=padx(sn),
                   w0=w0[l], a0=a0[l], wl=wl.astype(BF), g2=g2[l].astype(BF),
                   k_k=row(k_k[l]), k_a=row(k_a[l]), r_k=row(r_k[l]), bd=bd, tri=tri)

        proj = _inproj(xs, row(norm_mix[l]), w_in_p, tm_in, tn_in, m)
        att = _attention(proj, bias, row(q_norm[l]), row(k_norm[l]), sink[l].astype(F32), segs)
        po = _prep(proj, prm, segs, tt, cb)
        v_b, bv, g = po[10], po[11], po[12]
        y0 = _scan(0, po[0:4], v_b, po[4], segs, ts, cs)
        y1 = _scan(1, po[5:9], v_b, po[9], segs, ts, cs)
        rw = _post(y0, y1, bv, g, row(lnx_w[l]), row(lnx_b[l]), bd, tt, cb)
        x = _outproj(att, rw, w_out[l].astype(BF), xs, tm_out, tn_out)

        km, vm = _memkv(mem, row(norm_memkv[l]), wk_mem[l].astype(BF), wv_mem[l].astype(BF), row(kn_mem[l]))
        x, h3 = _memattn(x, km, vm, row(norm_mem[l]), wq_mem[l].astype(BF), row(qn_mem[l]),
                         wo_mem[l].astype(BF), row(norm_ffn[l]), segs, tm_mem)

        act = _ffn_up(h3, w_gate[l].astype(BF), w_up[l].astype(BF), tm_up, tn_up)
        wd = w_down[l].astype(BF)
        if l + 1 < depth:
            xs = [_ffn_down(act, wd, x, tm_down, tn_down)]
    y1 = _ffn_down(act, wd, x, tm_down, tn_down, 0, m1)
    y2 = _ffn_down(act, wd, x, tm_down, tn_down, m1, m - m1)
    return (y1.reshape(b1, t1, d), y2.reshape(b2, t2, d))
```

```python
import functools
import math

import numpy as np
import jax
import jax.numpy as jnp
from jax import lax
from jax.experimental import pallas as pl
from jax.experimental.pallas import tpu as pltpu

F32 = jnp.float32
BF = jnp.bfloat16

HEAD_DIM = 128
N_Q_HEADS = 16
N_KV_HEADS = 4
GQA = N_Q_HEADS // N_KV_HEADS
BLOCK = 128
N_REL_BUCKETS = 32
REL_MAX_DIST = 128
RWKV_HEAD = 64
DECAY_LORA = 96
AAA_LORA = 96
GATE_LORA = 256
LNX_EPS = 64e-5
MEM_HEADS = 4
MEM_HEAD_DIM = 128
N_MEM = 256
RMS_EPS = 1e-6
NEG = -1e30

LANE = 128
CHUNK = 64
LORA_W = 2 * DECAY_LORA + 2 * AAA_LORA
XTRA_W = 768
VMEM_LIMIT = 56 * 1024 * 1024


def _cp(*sem):
    return pltpu.CompilerParams(dimension_semantics=sem, vmem_limit_bytes=VMEM_LIMIT)


def _dot(a, b):
    return jnp.dot(a, b, preferred_element_type=F32)


def _dot_nt(a, b):
    return lax.dot_general(a, b, (((1,), (1,)), ((), ())), preferred_element_type=F32)


def _dot_tn(a, b):
    return lax.dot_general(a, b, (((0,), (0,)), ((), ())), preferred_element_type=F32)


def _split2(x):
    hi = x.astype(BF)
    lo = (x - hi.astype(F32)).astype(BF)
    return hi, lo


def _rms_rows(x, g):
    ms = jnp.mean(x * x, axis=-1, keepdims=True)
    return x * lax.rsqrt(ms + RMS_EPS) * g


def _seq_info(row, segs):
    off = 0
    pos = None
    tlen = None
    for n, t in segs:
        p = lax.rem(row - off, t)
        if pos is None:
            pos, tlen = p, jnp.int32(t)
        else:
            inside = row >= off
            pos = jnp.where(inside, p, pos)
            tlen = jnp.where(inside, t, tlen)
        off += n * t
    return pos, tlen


def _batch_of(row, segs):
    off = 0
    boff = 0
    res = None
    for n, t in segs:
        b = boff + (row - off) // t
        res = b if res is None else jnp.where(row >= off, b, res)
        off += n * t
        boff += n
    return res


def _over_parts(call, parts, tm):
    out, t0 = None, 0
    for part in parts:
        assert part.shape[0] % tm == 0
        out = call(part, t0, out)
        t0 += part.shape[0] // tm
    return out


def _inproj_kernel(x_ref, g_ref, w_ref, *rest):
    o_ref, h_ref = rest[-2:]

    @pl.when(pl.program_id(1) == 0)
    def _():
        rows = x_ref.shape[0]
        step = min(rows, 256)
        for r0 in range(0, rows, step):
            h_ref[r0:r0 + step, :] = _rms_rows(x_ref[r0:r0 + step, :], g_ref[...]).astype(BF)

    o_ref[...] = _dot(h_ref[...], w_ref[...])


def _inproj(xs, g, w, layer, tm, tn, m):
    d = xs[0].shape[1]
    n = w.shape[2]

    def call(x, t0, prev):
        carried = [] if prev is None else [prev]
        return pl.pallas_call(
            _inproj_kernel,
            grid=(x.shape[0] // tm, n // tn),
            in_specs=[pl.BlockSpec((tm, d), lambda i, j: (i, 0), pipeline_mode=pl.Buffered(1)),
                      pl.BlockSpec((1, d), lambda i, j: (0, 0)),
                      pl.BlockSpec((None, d, tn), lambda i, j: (layer, 0, j))]
            + [pl.BlockSpec(memory_space=pl.ANY)] * len(carried),
            out_specs=pl.BlockSpec((tm, tn), lambda i, j: (t0 + i, j)),
            out_shape=jax.ShapeDtypeStruct((m, n), F32),
            input_output_aliases={3: 0} if carried else {},
            scratch_shapes=[pltpu.VMEM((tm, d), BF)],
            compiler_params=_cp("parallel", "arbitrary"),
            name="in_proj",
        )(x, g, w, *carried)

    return _over_parts(call, xs, tm)


def _attn_kernel(segs, q_ref, kp_ref, kc_ref, kn_ref, vp_ref, vc_ref, vn_ref,
                 bias_ref, qn_ref, kn_g_ref, sink_ref, o_ref):
    n = pl.program_id(0)
    pos, tlen = _seq_info(n * BLOCK, segs)
    first = pos == 0
    last = pos + BLOCK == tlen
    col = lax.broadcasted_iota(jnp.int32, (BLOCK, 3 * BLOCK), 1)
    dead = (first & (col < BLOCK)) | (last & (col >= 2 * BLOCK))
    scale = HEAD_DIM ** -0.5
    heads = range(N_Q_HEADS)
    hsl = [slice(h * HEAD_DIM, (h + 1) * HEAD_DIM) for h in heads]
    kw, vw = [], []
    for kh in range(N_KV_HEADS):
        ls = hsl[kh]
        kcat = jnp.concatenate([kp_ref[:, ls], kc_ref[:, ls], kn_ref[:, ls]], axis=0)
        kw.append(_rms_rows(kcat, kn_g_ref[...]).astype(BF))
        vw.append(jnp.concatenate([vp_ref[:, ls], vc_ref[:, ls], vn_ref[:, ls]], axis=0).astype(BF))
    q = [_rms_rows(q_ref[:, hsl[h]], qn_ref[...]).astype(BF) for h in heads]
    s = [_dot_nt(q[h], kw[h // GQA]) * scale + bias_ref[h] for h in heads]
    s = [jnp.where(dead, NEG, s[h]) for h in heads]
    mx = [jnp.maximum(jnp.max(s[h], axis=-1, keepdims=True), sink_ref[h]) for h in heads]
    p = [jnp.exp(s[h] - mx[h]) for h in heads]
    den = [jnp.sum(p[h], axis=-1, keepdims=True) + jnp.exp(sink_ref[h] - mx[h]) for h in heads]
    o = [_dot(p[h].astype(BF), vw[h // GQA]) for h in heads]
    for h in heads:
        o_ref[:, hsl[h]] = (o[h] / den[h]).astype(o_ref.dtype)


def _attention(proj, bias, qn, kn, sink, segs):
    m = proj.shape[0]
    nb = m // BLOCK
    aw = N_Q_HEADS * HEAD_DIM
    kvw = N_KV_HEADS * HEAD_DIM
    kcol = aw // kvw
    vcol = kcol + 1
    prev = lambda n: jnp.maximum(n - 1, 0)
    nxt = lambda n: jnp.minimum(n + 1, nb - 1)
    specs = [pl.BlockSpec((BLOCK, aw), lambda n: (n, 0))]
    for c in (kcol, vcol):
        specs += [pl.BlockSpec((BLOCK, kvw), lambda n, c=c: (prev(n), c)),
                  pl.BlockSpec((BLOCK, kvw), lambda n, c=c: (n, c)),
                  pl.BlockSpec((BLOCK, kvw), lambda n, c=c: (nxt(n), c))]
    specs += [pl.BlockSpec((N_Q_HEADS, BLOCK, 3 * BLOCK), lambda n: (0, 0, 0)),
              pl.BlockSpec((1, HEAD_DIM), lambda n: (0, 0)),
              pl.BlockSpec((1, HEAD_DIM), lambda n: (0, 0)),
              pl.BlockSpec(memory_space=pltpu.SMEM)]
    return pl.pallas_call(
        functools.partial(_attn_kernel, segs),
        grid=(nb,),
        in_specs=specs,
        out_specs=pl.BlockSpec((BLOCK, aw), lambda n: (n, 0)),
        out_shape=jax.ShapeDtypeStruct((m, aw), BF),
        compiler_params=_cp("parallel"),
        name="window_attn",
    )(proj, proj, proj, proj, proj, proj, proj, bias, qn, kn, sink)


def _prep_kernel(segs, tt, *refs):
    (pr, pr_p, pr_n, pk, pk_p, pk_n, pv, pv_p, pv_n, px, px_p, px_n,
     spr, snr, spk, snk, spv, snv, spx, snx, w0, a0, wl, g2, kk_g, ka_g, rk_g, bd, tri) = refs[:29]
    outs = refs[29:]
    zouts = (outs[0:5], outs[5:10])
    vo, bvo, go = outs[10:13]

    row0 = pl.program_id(0) * tt
    pos, tlen = _seq_info(row0, segs)
    first = pos == 0
    last = pos + tt == tlen
    rows = lax.broadcasted_iota(jnp.int32, (tt, 1), 0)

    def shifted(x_ref, p_ref, n_ref, sp, sn):
        x = x_ref[...]
        prow = jnp.where(first, 0.0, p_ref[7:8, :])
        nrow = jnp.where(last, 0.0, n_ref[0:1, :])
        prev = jnp.where(rows == 0, prow, pltpu.roll(x, 1, 0))
        nxt = jnp.where(rows == tt - 1, nrow, pltpu.roll(x, tt - 1, 0))
        return x + sp[...] * (prev - x) + sn[...] * (nxt - x)

    r = shifted(pr, pr_p, pr_n, spr, snr)
    k = shifted(pk, pk_p, pk_n, spk, snk)
    v = shifted(pv, pv_p, pv_n, spv, snv)
    x = shifted(px, px_p, px_n, spx, snx)

    xl = x[:, :LORA_W]
    lane = lax.broadcasted_iota(jnp.int32, xl.shape, 1)
    lx = jnp.where(lane < 2 * DECAY_LORA, jnp.tanh(xl), xl).astype(BF)
    sg = jax.nn.sigmoid(x[:, LORA_W:LORA_W + GATE_LORA]).astype(BF)
    go[...] = _dot(sg, g2[...]).astype(go.dtype)

    kkr = k * kk_g[...]
    n2 = _dot((kkr * kkr).astype(BF), bd[...])
    kk = kkr / jnp.maximum(jnp.sqrt(n2), 1e-12)

    kd_sum = None
    for z in (0, 1):
        at_o, rt_o, bt_o, kt_o, ct_o = zouts[z]
        wraw = w0[z:z + 1, :] + _dot(lx, wl[z])
        a = jax.nn.sigmoid(a0[z:z + 1, :] + _dot(lx, wl[2 + z]))
        lw = -math.exp(-0.5) * jax.nn.sigmoid(wraw)
        hi, lo = _split2(lw)
        cin = _dot(tri[z], hi) + _dot(tri[z], lo)
        kd = k * (1.0 + (a - 1.0) * ka_g[...])
        b = kk * a
        e_neg = jnp.exp(-cin)
        at_o[...] = (-kk * jnp.exp(cin - lw)).astype(BF)
        rt_o[...] = (r * jnp.exp(cin)).astype(BF)
        bt_o[...] = (b * e_neg).astype(BF)
        kt_o[...] = (kd * e_neg).astype(BF)
        end = CHUNK - 1 if z == 0 else 0
        ct_o[...] = jnp.concatenate(
            [jnp.broadcast_to(cin[c * CHUNK + end:c * CHUNK + end + 1], (8, cin.shape[1]))
             for c in range(tt // CHUNK)], axis=0)
        kd_sum = kd if kd_sum is None else kd_sum + kd

    bonus = _dot((r * kd_sum * rk_g[...]).astype(BF), bd[...])
    bvo[...] = (bonus * v).astype(bvo.dtype)
    vo[...] = v.astype(BF)


def _prep(proj, p, segs, tt, cb):
    m = proj.shape[0]
    c = p["k_k"].shape[1]
    rcol0 = (N_Q_HEADS + 2 * N_KV_HEADS) * HEAD_DIM
    nrow8 = m // 8
    t8 = tt // 8

    def trio(width, colfn):
        return [pl.BlockSpec((tt, width), lambda i, j: (i, colfn(j))),
                pl.BlockSpec((8, width), lambda i, j: (jnp.maximum(i * t8 - 1, 0), colfn(j))),
                pl.BlockSpec((8, width), lambda i, j: (jnp.minimum((i + 1) * t8, nrow8 - 1), colfn(j)))]

    specs = []
    for sec in range(3):
        base = (rcol0 + sec * c) // cb
        specs += trio(cb, lambda j, base=base: base + j)
    xblk = (rcol0 + 3 * c) // XTRA_W
    specs += trio(XTRA_W, lambda j: xblk)
    vec = lambda: pl.BlockSpec((1, cb), lambda i, j: (0, j))
    specs += [vec(), vec(), vec(), vec(), vec(), vec(),
              pl.BlockSpec((1, XTRA_W), lambda i, j: (0, 0)),
              pl.BlockSpec((1, XTRA_W), lambda i, j: (0, 0)),
              pl.BlockSpec((2, cb), lambda i, j: (0, j)),
              pl.BlockSpec((2, cb), lambda i, j: (0, j)),
              pl.BlockSpec((4, LORA_W, cb), lambda i, j: (0, 0, j)),
              pl.BlockSpec((GATE_LORA, cb), lambda i, j: (0, j)),
              vec(), vec(), vec(),
              pl.BlockSpec((cb, cb), lambda i, j: (0, 0)),
              pl.BlockSpec((2, tt, tt), lambda i, j: (0, 0, 0))]
    big = pl.BlockSpec((tt, cb), lambda i, j: (i, j))
    small = pl.BlockSpec((t8, cb), lambda i, j: (i, j))
    out_specs = ([big] * 4 + [small]) * 2 + [big] * 3
    big_s = jax.ShapeDtypeStruct((m, c), BF)
    small_s = jax.ShapeDtypeStruct((nrow8, c), F32)
    out_shape = ([big_s] * 4 + [small_s]) * 2 + [big_s] * 3
    args = [proj] * 12 + [p["sp_r"], p["sn_r"], p["sp_k"], p["sn_k"], p["sp_v"], p["sn_v"],
                          p["sp_x"], p["sn_x"], p["w0"], p["a0"], p["wl"], p["g2"],
                          p["k_k"], p["k_a"], p["r_k"], p["bd"], p["tri"]]
    return pl.pallas_call(
        functools.partial(_prep_kernel, segs, tt),
        grid=(m // tt, c // cb),
        in_specs=specs,
        out_specs=out_specs,
        out_shape=out_shape,
        compiler_params=_cp("parallel", "arbitrary"),
        name="rwkv_prep",
    )(*args)


def _scan_body(z, segs, ts, npair, at, rt, bt, kt, v, ct, y_ref, h_ref):
    i = pl.program_id(1)
    nblk = pl.num_programs(1)
    blk = i if z == 0 else nblk - 1 - i
    pos, tlen = _seq_info(blk * ts, segs)
    reset = (pos == 0) if z == 0 else (pos + ts == tlen)

    @pl.when(reset)
    def _():
        h_ref[...] = jnp.zeros_like(h_ref)

    n = 2 * CHUNK
    lane = lax.broadcasted_iota(jnp.int32, (CHUNK, LANE), 1)
    m_lo = jnp.where(lane < RWKV_HEAD, 1.0, 0.0).astype(BF)
    m_hi = jnp.where(lane < RWKV_HEAD, 0.0, 1.0).astype(BF)
    row = lax.broadcasted_iota(jnp.int32, (n, n), 0)
    col = lax.broadcasted_iota(jnp.int32, (n, n), 1)
    strict = (row > col) if z == 0 else (row < col)
    incl = (row >= col) if z == 0 else (row <= col)
    blk16 = (row // 16) == (col // 16)
    off32 = ((row // 32) == (col // 32)) & ((row // 16) != (col // 16))
    off64 = ((row // 64) == (col // 64)) & ((row // 32) != (col // 32))
    eye = jnp.where(row == col, 1.0, 0.0).astype(F32)

    nch = ts // CHUNK
    order = list(range(nch)) if z == 0 else list(range(nch - 1, -1, -1))
    items = [(p, c) for c in order for p in range(npair)]

    def tile(ref, p, c):
        return ref[c * CHUNK:(c + 1) * CHUNK, p * LANE:(p + 1) * LANE]

    def stack(x):
        return jnp.concatenate([x * m_lo, x * m_hi], axis=0)

    def each(fn, *lists):
        return [fn(*args) for args in zip(*lists)]

    xa = [stack(tile(at, p, c)) for p, c in items]
    xr = [stack(tile(rt, p, c)) for p, c in items]
    xb = [stack(tile(bt, p, c)) for p, c in items]
    xk = [stack(tile(kt, p, c)) for p, c in items]
    vs = [stack(tile(v, p, c)) for p, c in items]
    ctr = [ct[c * 8:c * 8 + 1, p * LANE:(p + 1) * LANE] for p, c in items]
    grow = [jnp.broadcast_to(jnp.exp(r), (n, LANE)).astype(BF) for r in ctr]
    xbg = each(lambda x, g: x * g, xb, grow)
    xkg = each(lambda x, g: x * g, xk, grow)
    gam = [jnp.exp(jnp.transpose(jnp.broadcast_to(r, (n, n)))) for r in ctr]

    pm = each(lambda a, r, b, k: _dot_nt(jnp.concatenate([a, r], axis=0), jnp.concatenate([b, k], axis=0)),
              xa, xr, xb, xk)
    a_ab = [jnp.where(strict, m[:n, :n], 0.0) for m in pm]
    a_ak = [jnp.where(strict, m[:n, n:], 0.0).astype(BF) for m in pm]
    a_rb = [jnp.where(incl, m[n:, :n], 0.0).astype(BF) for m in pm]
    a_rk = [jnp.where(incl, m[n:, n:], 0.0).astype(BF) for m in pm]

    ad = [jnp.where(blk16, a, 0.0) for a in a_ab]
    pk = [a.astype(BF) for a in ad]
    tinv = [eye + a for a in ad]
    pk = each(lambda a: _dot(a, a).astype(BF), pk)
    av = each(lambda a, w: _dot(a, w).astype(BF), a_ak, vs)
    for _ in range(2):
        both = each(lambda a, t: _dot(a, jnp.concatenate([a, t.astype(BF)], axis=1)), pk, tinv)
        pk = [m[:, :n].astype(BF) for m in both]
        tinv = each(lambda t, m: t + m[:, n:], tinv, both)
    tinv = each(lambda t, a: t + _dot(a, t.astype(BF)), tinv, pk)
    for off in (off32, off64):
        tb = [t.astype(BF) for t in tinv]
        ao = [jnp.where(off, a, 0.0).astype(BF) for a in a_ab]
        mid = each(lambda t, a: _dot(t, a).astype(BF), tb, ao)
        tinv = each(lambda t, m_, t_b: t + _dot(m_, t_b), tinv, mid, tb)

    wq = each(lambda t, a, q: _dot(t.astype(BF), jnp.concatenate([a, q], axis=1)).astype(BF), tinv, xa, av)
    zero = jnp.zeros((n, n), BF)
    wqv = each(lambda w, v_: jnp.concatenate([w, jnp.concatenate([zero, v_], axis=1)], axis=0), wq, vs)
    rb = each(lambda b, k, w: _dot(jnp.concatenate([b, k], axis=1), w), a_rb, a_rk, wqv)
    bgw = each(lambda b, k, w: _dot_tn(jnp.concatenate([b, k], axis=0), w), xbg, xkg, wqv)
    ry = each(lambda r, m_: (r.astype(F32) + m_[:, :n]).astype(BF), xr, rb)
    qy = [m_[:, n:] for m_ in rb]
    gm = [m_[:, :n].astype(BF) for m_ in bgw]
    jm = [m_[:, n:] for m_ in bgw]

    hs = [h_ref[p] for p in range(npair)]
    for idx, (p, c) in enumerate(items):
        hb = hs[p].astype(BF)
        yst = _dot(ry[idx], hb) + qy[idx]
        y_ref[c * CHUNK:(c + 1) * CHUNK, p * LANE:(p + 1) * LANE] = yst[:CHUNK] + yst[CHUNK:]
        hs[p] = gam[idx] * hs[p] + _dot(gm[idx], hb) + jm[idx]
    for p in range(npair):
        h_ref[p] = hs[p]


def _scan_kernel(z, segs, ts, npair, at, rt, bt, kt, v, ct, y_ref, h_ref):
    _scan_body(z, segs, ts, npair, at, rt, bt, kt, v, ct, y_ref, h_ref)


def _scan_finish_kernel(z, segs, ts, npair, at, rt, bt, kt, v, ct, y0, bv, g, lw, lb, bd, o_ref, h_ref, y_ref):
    _scan_body(z, segs, ts, npair, at, rt, bt, kt, v, ct, y_ref, h_ref)
    y = y0[...] + y_ref[...]
    inv_n = 1.0 / RWKV_HEAD
    hi, lo = _split2(y)
    mu = (_dot(hi, bd[...]) + _dot(lo, bd[...])) * inv_n
    d = y - mu
    var = _dot((d * d).astype(BF), bd[...]) * inv_n
    yn = d * lax.rsqrt(var + LNX_EPS) * lw[...] + lb[...]
    o_ref[...] = ((yn + bv[...].astype(F32)) * g[...].astype(F32)).astype(o_ref.dtype)


def _scan(z, ops, v, ct, segs, ts, cs, finish=None):
    m, c = v.shape
    nblk = m // ts
    npair = cs // LANE
    rowmap = (lambda j, i: (i, j)) if z == 0 else (lambda j, i: (nblk - 1 - i, j))
    big = pl.BlockSpec((ts, cs), rowmap)
    small = pl.BlockSpec((ts // 8, cs), rowmap)
    state = pltpu.VMEM((npair, LANE, LANE), F32)
    name = "rwkv_scan_fwd" if z == 0 else "rwkv_scan_bwd"
    if finish is None:
        return pl.pallas_call(
            functools.partial(_scan_kernel, z, segs, ts, npair),
            grid=(c // cs, nblk),
            in_specs=[big] * 5 + [small],
            out_specs=big,
            out_shape=jax.ShapeDtypeStruct((m, c), F32),
            scratch_shapes=[state],
            compiler_params=_cp("parallel", "arbitrary"),
            name=name,
        )(*ops, v, ct)
    vec = pl.BlockSpec((1, cs), lambda j, i: (0, j))
    return pl.pallas_call(
        functools.partial(_scan_finish_kernel, z, segs, ts, npair),
        grid=(c // cs, nblk),
        in_specs=[big] * 5 + [small] + [big] * 3 + [vec, vec, pl.BlockSpec((cs, cs), lambda j, i: (0, 0))],
        out_specs=big,
        out_shape=jax.ShapeDtypeStruct((m, c), BF),
        scratch_shapes=[state, pltpu.VMEM((ts, cs), F32)],
        compiler_params=_cp("parallel", "arbitrary"),
        name=name + "_finish",
    )(*ops, v, ct, *finish)


def _outproj_kernel(ka, a_ref, b_ref, w_ref, x_ref, *rest):
    o_ref = rest[-1]
    acc = _dot(a_ref[...], w_ref[:ka, :]) + _dot(b_ref[...], w_ref[ka:, :])
    o_ref[...] = x_ref[...] + acc


def _outproj(att, rw, w, layer, xs, tm, tn):
    m, ka = att.shape
    kb = rw.shape[1]
    n = w.shape[2]

    def call(x, t0, prev):
        carried = [] if prev is None else [prev]
        return pl.pallas_call(
            functools.partial(_outproj_kernel, ka),
            grid=(x.shape[0] // tm, n // tn),
            in_specs=[pl.BlockSpec((tm, ka), lambda i, j: (t0 + i, 0)),
                      pl.BlockSpec((tm, kb), lambda i, j: (t0 + i, 0)),
                      pl.BlockSpec((None, ka + kb, tn), lambda i, j: (layer, 0, j)),
                      pl.BlockSpec((tm, tn), lambda i, j: (i, j))]
            + [pl.BlockSpec(memory_space=pl.ANY)] * len(carried),
            out_specs=pl.BlockSpec((tm, tn), lambda i, j: (t0 + i, j)),
            out_shape=jax.ShapeDtypeStruct((m, n), F32),
            input_output_aliases={4: 0} if carried else {},
            compiler_params=_cp("parallel", "parallel"),
            name="out_proj",
        )(att, rw, w, x, *carried)

    return _over_parts(call, xs, tm)


def _memkv_kernel(m_ref, g_ref, wk_ref, wv_ref, kn_ref, k_out, v_out):
    mm = _rms_rows(m_ref[...], g_ref[...]).astype(BF)
    k = _dot(mm, wk_ref[...])
    for h in range(MEM_HEADS):
        hs = slice(h * MEM_HEAD_DIM, (h + 1) * MEM_HEAD_DIM)
        k_out[:, hs] = _rms_rows(k[:, hs], kn_ref[...]).astype(BF)
    v_out[...] = _dot(mm, wv_ref[...]).astype(BF)


def _memkv(mem, g, wk, wv, kn, layer):
    rows, d = mem.shape
    w = wk.shape[2]
    full = lambda shape: pl.BlockSpec(shape, lambda i: (0, 0))
    wfull = lambda shape: pl.BlockSpec((None,) + shape, lambda i: (layer, 0, 0))
    return pl.pallas_call(
        _memkv_kernel,
        grid=(rows // N_MEM,),
        in_specs=[pl.BlockSpec((N_MEM, d), lambda i: (i, 0)), full((1, d)), wfull((d, w)), wfull((d, w)),
                  full((1, MEM_HEAD_DIM))],
        out_specs=[pl.BlockSpec((N_MEM, w), lambda i: (i, 0))] * 2,
        out_shape=[jax.ShapeDtypeStruct((rows, w), BF)] * 2,
        compiler_params=_cp("parallel"),
        name="mem_kv",
    )(mem, g, wk, wv, kn)


def _memattn_kernel(x_ref, k_ref, v_ref, gm_ref, wq_ref, qn_ref, wo_ref, gf_ref, x_out, h_out):
    x = x_ref[...]
    h = _rms_rows(x, gm_ref[...]).astype(BF)
    q = _dot(h, wq_ref[...])
    scale = MEM_HEAD_DIM ** -0.5
    heads = range(MEM_HEADS)
    hsl = [slice(hd * MEM_HEAD_DIM, (hd + 1) * MEM_HEAD_DIM) for hd in heads]
    qh = [_rms_rows(q[:, hsl[hd]], qn_ref[...]).astype(BF) for hd in heads]
    s = [_dot_nt(qh[hd], k_ref[:, hsl[hd]]) * scale for hd in heads]
    p = [jnp.exp(s[hd] - jnp.max(s[hd], axis=-1, keepdims=True)) for hd in heads]
    den = [jnp.sum(p[hd], axis=-1, keepdims=True) for hd in heads]
    pv = [_dot(p[hd].astype(BF), v_ref[:, hsl[hd]]) for hd in heads]
    o = jnp.concatenate([(pv[hd] / den[hd]).astype(BF) for hd in heads], axis=1)
    x2 = x + _dot(o, wo_ref[...])
    x_out[...] = x2
    h_out[...] = _rms_rows(x2, gf_ref[...]).astype(BF)


def _memattn(x, k, v, gm, wq, qn, wo, gf, layer, segs, tm):
    m, d = x.shape
    w = wq.shape[2]
    full = lambda shape: pl.BlockSpec(shape, lambda i: (0, 0))
    wfull = lambda shape: pl.BlockSpec((None,) + shape, lambda i: (layer, 0, 0))
    kv = pl.BlockSpec((N_MEM, w), lambda i: (_batch_of(i * tm, segs), 0))
    row = pl.BlockSpec((tm, d), lambda i: (i, 0))
    return pl.pallas_call(
        _memattn_kernel,
        grid=(m // tm,),
        in_specs=[row, kv, kv, full((1, d)), wfull((d, w)), full((1, MEM_HEAD_DIM)), wfull((w, d)), full((1, d))],
        out_specs=[row, row],
        out_shape=[jax.ShapeDtypeStruct((m, d), F32), jax.ShapeDtypeStruct((m, d), BF)],
        compiler_params=_cp("parallel"),
        name="mem_attn",
    )(x, k, v, gm, wq, qn, wo, gf)


def _ffn_up_kernel(h_ref, wg_ref, wu_ref, o_ref):
    h = h_ref[...]
    gate = _dot(h, wg_ref[...])
    up = _dot(h, wu_ref[...])
    o_ref[...] = (gate * jax.nn.sigmoid(gate) * up).astype(o_ref.dtype)


def _ffn_up(h, wg, wu, layer, tm, tn):
    m, d = h.shape
    f = wg.shape[2]
    wspec = pl.BlockSpec((None, d, tn), lambda i, j: (layer, 0, j))
    return pl.pallas_call(
        _ffn_up_kernel,
        grid=(m // tm, f // tn),
        in_specs=[pl.BlockSpec((tm, d), lambda i, j: (i, 0)), wspec, wspec],
        out_specs=pl.BlockSpec((tm, tn), lambda i, j: (i, j)),
        out_shape=jax.ShapeDtypeStruct((m, f), BF),
        compiler_params=_cp("parallel", "parallel"),
        name="ffn_up",
    )(h, wg, wu)


def _ffn_down_kernel(a_ref, w_ref, x_ref, o_ref):
    o_ref[...] = x_ref[...] + _dot(a_ref[...], w_ref[...])


def _ffn_down(act, wd, layer, x, tm, tn, row0=0, rows=None):
    f = act.shape[1]
    n = wd.shape[2]
    rows = act.shape[0] if rows is None else rows
    assert row0 % tm == 0 and rows % tm == 0
    t0 = row0 // tm
    return pl.pallas_call(
        _ffn_down_kernel,
        grid=(rows // tm, n // tn),
        in_specs=[pl.BlockSpec((tm, f), lambda i, j: (t0 + i, 0)),
                  pl.BlockSpec((None, f, tn), lambda i, j: (layer, 0, j)),
                  pl.BlockSpec((tm, tn), lambda i, j: (t0 + i, j))],
        out_specs=pl.BlockSpec((tm, tn), lambda i, j: (i, j)),
        out_shape=jax.ShapeDtypeStruct((rows, n), F32),
        compiler_params=_cp("parallel", "parallel"),
        name="ffn_down",
    )(act, wd, x)


def _rel_bucket(rel):
    half = N_REL_BUCKETS // 2
    exact = half // 2
    n = np.abs(rel)
    large = exact + (np.log(np.maximum(n, 1) / exact) / np.log(REL_MAX_DIST / exact)
                     * (half - exact)).astype(np.int32)
    large = np.minimum(large, half - 1)
    return (rel > 0).astype(np.int32) * half + np.where(n < exact, n, large)


def _attn_bias(rel_bias):
    qi = np.arange(BLOCK)[:, None]
    kj = np.arange(3 * BLOCK)[None, :]
    rel = kj - BLOCK - qi
    onehot = (_rel_bucket(rel)[..., None] == np.arange(N_REL_BUCKETS)).astype(np.float32)
    bias = jnp.einsum("qkb,bh->hqk", jnp.asarray(onehot), rel_bias.astype(F32), precision=lax.Precision.HIGHEST)
    return jnp.where(jnp.asarray(np.abs(rel) <= BLOCK)[None], bias, NEG)


def _tri_consts(tt):
    t = np.arange(tt)[:, None]
    s = np.arange(tt)[None, :]
    same = (t // CHUNK) == (s // CHUNK)
    mats = [same & (s <= t), same & (s >= t)]
    return jnp.asarray(np.stack(mats).astype(np.float32), dtype=BF)


def _block_diag_ones(cb):
    i = np.arange(cb)
    return jnp.asarray(((i[:, None] // RWKV_HEAD) == (i[None, :] // RWKV_HEAD)).astype(np.float32), dtype=BF)


def _pick(n, pref):
    t = min(n, pref)
    assert n % t == 0, (n, t)
    return t


def kernel(x_prompt, x_sample, mem_prompt, mem_sample, rel_bias, norm_mix, w_in, q_norm, k_norm, sink,
           shift_prev, shift_next, w0, w2, a0, a2, g2, k_k, k_a, r_k, lnx_w, lnx_b, w_out,
           norm_mem, norm_memkv, wq_mem, wk_mem, wv_mem, wo_mem, qn_mem, kn_mem,
           norm_ffn, w_gate, w_up, w_down):
    b1, t1, d = x_prompt.shape
    b2, t2, _ = x_sample.shape
    segs = ((b1, t1), (b2, t2))
    m1 = b1 * t1
    m = m1 + b2 * t2
    xs = [x_prompt.reshape(m1, d), x_sample.reshape(b2 * t2, d)]
    mem = jnp.concatenate([mem_prompt.reshape(-1, d), mem_sample.reshape(-1, d)], axis=0)
    depth = w_in.shape[0]
    c = k_k.shape[1]
    lora0 = 3 * c

    mg = math.gcd(m1, m - m1)
    tm_in = _pick(mg, 1024)
    tn_in = 768
    tm_out, tn_out = _pick(mg, 1024), 1024
    tt = _pick(min(t1, t2), 256)
    cb = _pick(c, 512)
    ts = _pick(min(t1, t2), 256)
    cs = _pick(c, 512)
    tm_mem = _pick(min(t1, t2), 256)
    tm_up, tn_up = _pick(m, 2048), 256
    tm_down, tn_down = _pick(mg, 512), 512

    bias = _attn_bias(rel_bias)
    tri = _tri_consts(tt)
    bd = _block_diag_ones(cb)
    row = lambda a: a.reshape(1, -1).astype(F32)

    w_in_b = jnp.pad(w_in.astype(BF), ((0, 0), (0, 0), (0, XTRA_W - LORA_W - GATE_LORA)))
    w_out_b = w_out.astype(BF)
    wq_b, wk_b, wv_b, wo_b = (w.astype(BF) for w in (wq_mem, wk_mem, wv_mem, wo_mem))
    wg_b, wu_b, wd_b = (w.astype(BF) for w in (w_gate, w_up, w_down))

    for l in range(depth):
        sp, sn = shift_prev[l], shift_next[l]
        padx = lambda a: jnp.pad(a[lora0:], (0, XTRA_W - LORA_W - GATE_LORA)).reshape(1, -1)
        wl = jnp.zeros((4, LORA_W, c), F32)
        wl = wl.at[0, 0:DECAY_LORA].set(w2[l, 0]).at[1, DECAY_LORA:2 * DECAY_LORA].set(w2[l, 1])
        wl = wl.at[2, 2 * DECAY_LORA:2 * DECAY_LORA + AAA_LORA].set(a2[l, 0])
        wl = wl.at[3, 2 * DECAY_LORA + AAA_LORA:].set(a2[l, 1])
        prm = dict(sp_r=row(sp[0:c]), sn_r=row(sn[0:c]), sp_k=row(sp[c:2 * c]), sn_k=row(sn[c:2 * c]),
                   sp_v=row(sp[2 * c:3 * c]), sn_v=row(sn[2 * c:3 * c]), sp_x=padx(sp), sn_x=padx(sn),
                   w0=w0[l], a0=a0[l], wl=wl.astype(BF), g2=g2[l].astype(BF),
                   k_k=row(k_k[l]), k_a=row(k_a[l]), r_k=row(r_k[l]), bd=bd, tri=tri)

        proj = _inproj(xs, row(norm_mix[l]), w_in_b, l, tm_in, tn_in, m)
        att = _attention(proj, bias, row(q_norm[l]), row(k_norm[l]), sink[l].astype(F32), segs)
        po = _prep(proj, prm, segs, tt, cb)
        v_b, bv, g = po[10], po[11], po[12]
        y0 = _scan(0, po[0:4], v_b, po[4], segs, ts, cs)
        rw = _scan(1, po[5:9], v_b, po[9], segs, ts, cs,
                   finish=(y0, bv, g, row(lnx_w[l]), row(lnx_b[l]), bd))
        x = _outproj(att, rw, w_out_b, l, xs, tm_out, tn_out)

        km, vm = _memkv(mem, row(norm_memkv[l]), wk_b, wv_b, row(kn_mem[l]), l)
        x, h3 = _memattn(x, km, vm, row(norm_mem[l]), wq_b, row(qn_mem[l]), wo_b, row(norm_ffn[l]),
                         l, segs, tm_mem)

        act = _ffn_up(h3, wg_b, wu_b, l, tm_up, tn_up)
        if l + 1 < depth:
            xs = [_ffn_down(act, wd_b, l, x, tm_down, tn_down)]
    y1 = _ffn_down(act, wd_b, depth - 1, x, tm_down, tn_down, 0, m1)
    y2 = _ffn_down(act, wd_b, depth - 1, x, tm_down, tn_down, m1, m - m1)
    return (y1.reshape(b1, t1, d), y2.reshape(b2, t2, d))
```

```python
import functools
import math

import numpy as np
import jax
import jax.numpy as jnp
from jax import lax
from jax.experimental import pallas as pl
from jax.experimental.pallas import tpu as pltpu

F32 = jnp.float32
BF = jnp.bfloat16

HEAD_DIM = 128
N_Q_HEADS = 16
N_KV_HEADS = 4
GQA = N_Q_HEADS // N_KV_HEADS
BLOCK = 128
N_REL_BUCKETS = 32
REL_MAX_DIST = 128
RWKV_HEAD = 64
DECAY_LORA = 96
AAA_LORA = 96
GATE_LORA = 256
LNX_EPS = 64e-5
MEM_HEADS = 4
MEM_HEAD_DIM = 128
N_MEM = 256
RMS_EPS = 1e-6
NEG = -1e30
LOG2E = math.log2(math.e)

LANE = 128
CHUNK = 64
LORA_W = 2 * DECAY_LORA + 2 * AAA_LORA
XTRA_W = 768
VMEM_LIMIT = 56 * 1024 * 1024


def _cp(*sem):
    return pltpu.CompilerParams(dimension_semantics=sem, vmem_limit_bytes=VMEM_LIMIT)


def _dot(a, b):
    return jnp.dot(a, b, preferred_element_type=F32)


def _dot_nt(a, b):
    return lax.dot_general(a, b, (((1,), (1,)), ((), ())), preferred_element_type=F32)


def _dot_tn(a, b):
    return lax.dot_general(a, b, (((0,), (0,)), ((), ())), preferred_element_type=F32)


def _split2(x):
    hi = x.astype(BF)
    lo = (x - hi.astype(F32)).astype(BF)
    return hi, lo


def _rms_rows(x, g):
    ms = jnp.mean(x * x, axis=-1, keepdims=True)
    return x * lax.rsqrt(ms + RMS_EPS) * g


def _seq_info(row, segs):
    off = 0
    pos = None
    tlen = None
    for n, t in segs:
        p = lax.rem(row - off, t)
        if pos is None:
            pos, tlen = p, jnp.int32(t)
        else:
            inside = row >= off
            pos = jnp.where(inside, p, pos)
            tlen = jnp.where(inside, t, tlen)
        off += n * t
    return pos, tlen


def _batch_of(row, segs):
    off = 0
    boff = 0
    res = None
    for n, t in segs:
        b = boff + (row - off) // t
        res = b if res is None else jnp.where(row >= off, b, res)
        off += n * t
        boff += n
    return res


def _over_parts(call, parts, tm):
    out, t0 = None, 0
    for part in parts:
        assert part.shape[0] % tm == 0
        out = call(part, t0, out)
        t0 += part.shape[0] // tm
    return out


def _inproj_kernel(x_ref, g_ref, w_ref, *rest):
    o_ref, h_ref = rest[-2:]

    @pl.when(pl.program_id(1) == 0)
    def _():
        rows = x_ref.shape[0]
        step = min(rows, 256)
        for r0 in range(0, rows, step):
            h_ref[r0:r0 + step, :] = _rms_rows(x_ref[r0:r0 + step, :], g_ref[...]).astype(BF)

    o_ref[...] = _dot(h_ref[...], w_ref[...])


def _inproj(xs, g, w, layer, tm, tn, m):
    d = xs[0].shape[1]
    n = w.shape[2]

    def call(x, t0, prev):
        carried = [] if prev is None else [prev]
        return pl.pallas_call(
            _inproj_kernel,
            grid=(x.shape[0] // tm, n // tn),
            in_specs=[pl.BlockSpec((tm, d), lambda i, j: (i, 0), pipeline_mode=pl.Buffered(1)),
                      pl.BlockSpec((1, d), lambda i, j: (0, 0)),
                      pl.BlockSpec((None, d, tn), lambda i, j: (layer, 0, j))]
            + [pl.BlockSpec(memory_space=pl.ANY)] * len(carried),
            out_specs=pl.BlockSpec((tm, tn), lambda i, j: (t0 + i, j)),
            out_shape=jax.ShapeDtypeStruct((m, n), F32),
            input_output_aliases={3: 0} if carried else {},
            scratch_shapes=[pltpu.VMEM((tm, d), BF)],
            compiler_params=_cp("parallel", "arbitrary"),
            name="in_proj",
        )(x, g, w, *carried)

    return _over_parts(call, xs, tm)


def _attn_kernel(segs, q_ref, kp_ref, kc_ref, kn_ref, vp_ref, vc_ref, vn_ref,
                 bias_ref, qn_ref, kn_g_ref, sink_ref, o_ref):
    n = pl.program_id(0)
    pos, tlen = _seq_info(n * BLOCK, segs)
    first = pos == 0
    last = pos + BLOCK == tlen
    col = lax.broadcasted_iota(jnp.int32, (BLOCK, 3 * BLOCK), 1)
    dead = (first & (col < BLOCK)) | (last & (col >= 2 * BLOCK))
    scale = HEAD_DIM ** -0.5
    heads = range(N_Q_HEADS)
    hsl = [slice(h * HEAD_DIM, (h + 1) * HEAD_DIM) for h in heads]
    kw, vw = [], []
    for kh in range(N_KV_HEADS):
        ls = hsl[kh]
        kcat = jnp.concatenate([kp_ref[:, ls], kc_ref[:, ls], kn_ref[:, ls]], axis=0)
        kw.append(_rms_rows(kcat, kn_g_ref[...]).astype(BF))
        vw.append(jnp.concatenate([vp_ref[:, ls], vc_ref[:, ls], vn_ref[:, ls]], axis=0).astype(BF))
    q = [_rms_rows(q_ref[:, hsl[h]], qn_ref[...]).astype(BF) for h in heads]
    s = [_dot_nt(q[h], kw[h // GQA]) * scale + bias_ref[h] for h in heads]
    s = [jnp.where(dead, NEG, s[h]) for h in heads]
    mx = [jnp.maximum(jnp.max(s[h], axis=-1, keepdims=True), sink_ref[h]) for h in heads]
    p = [jnp.exp(s[h] - mx[h]) for h in heads]
    den = [jnp.sum(p[h], axis=-1, keepdims=True) + jnp.exp(sink_ref[h] - mx[h]) for h in heads]
    o = [_dot(p[h].astype(BF), vw[h // GQA]) for h in heads]
    for h in heads:
        o_ref[:, hsl[h]] = (o[h] / den[h]).astype(o_ref.dtype)


def _attention(proj, bias, qn, kn, sink, segs):
    m = proj.shape[0]
    nb = m // BLOCK
    aw = N_Q_HEADS * HEAD_DIM
    kvw = N_KV_HEADS * HEAD_DIM
    kcol = aw // kvw
    vcol = kcol + 1
    prev = lambda n: jnp.maximum(n - 1, 0)
    nxt = lambda n: jnp.minimum(n + 1, nb - 1)
    specs = [pl.BlockSpec((BLOCK, aw), lambda n: (n, 0))]
    for c in (kcol, vcol):
        specs += [pl.BlockSpec((BLOCK, kvw), lambda n, c=c: (prev(n), c)),
                  pl.BlockSpec((BLOCK, kvw), lambda n, c=c: (n, c)),
                  pl.BlockSpec((BLOCK, kvw), lambda n, c=c: (nxt(n), c))]
    specs += [pl.BlockSpec((N_Q_HEADS, BLOCK, 3 * BLOCK), lambda n: (0, 0, 0)),
              pl.BlockSpec((1, HEAD_DIM), lambda n: (0, 0)),
              pl.BlockSpec((1, HEAD_DIM), lambda n: (0, 0)),
              pl.BlockSpec(memory_space=pltpu.SMEM)]
    return pl.pallas_call(
        functools.partial(_attn_kernel, segs),
        grid=(nb,),
        in_specs=specs,
        out_specs=pl.BlockSpec((BLOCK, aw), lambda n: (n, 0)),
        out_shape=jax.ShapeDtypeStruct((m, aw), BF),
        compiler_params=_cp("parallel"),
        name="window_attn",
    )(proj, proj, proj, proj, proj, proj, proj, bias, qn, kn, sink)


def _prep_kernel(segs, tt, *refs):
    (pr, pr_p, pr_n, pk, pk_p, pk_n, pv, pv_p, pv_n, px, px_p, px_n,
     spr, snr, spk, snk, spv, snv, spx, snx, w0, a0, wl, g2, kk_g, ka_g, rk_g, bd, tri) = refs[:29]
    outs = refs[29:]
    zouts = (outs[0:5], outs[5:10])
    vo, bvo, go = outs[10:13]

    row0 = pl.program_id(0) * tt
    pos, tlen = _seq_info(row0, segs)
    first = pos == 0
    last = pos + tt == tlen
    rows = lax.broadcasted_iota(jnp.int32, (tt, 1), 0)

    def shifted(x_ref, p_ref, n_ref, sp, sn):
        x = x_ref[...]
        prow = jnp.where(first, 0.0, p_ref[7:8, :])
        nrow = jnp.where(last, 0.0, n_ref[0:1, :])
        prev = jnp.where(rows == 0, prow, pltpu.roll(x, 1, 0))
        nxt = jnp.where(rows == tt - 1, nrow, pltpu.roll(x, tt - 1, 0))
        return x + sp[...] * (prev - x) + sn[...] * (nxt - x)

    r = shifted(pr, pr_p, pr_n, spr, snr)
    k = shifted(pk, pk_p, pk_n, spk, snk)
    v = shifted(pv, pv_p, pv_n, spv, snv)
    x = shifted(px, px_p, px_n, spx, snx)

    xl = x[:, :LORA_W]
    lane = lax.broadcasted_iota(jnp.int32, xl.shape, 1)
    lx = jnp.where(lane < 2 * DECAY_LORA, jnp.tanh(xl), xl).astype(BF)
    sg = jax.nn.sigmoid(x[:, LORA_W:LORA_W + GATE_LORA]).astype(BF)
    go[...] = _dot(sg, g2[...]).astype(go.dtype)

    kkr = k * kk_g[...]
    n2 = _dot((kkr * kkr).astype(BF), bd[...])
    kk = kkr / jnp.maximum(jnp.sqrt(n2), 1e-12)

    kka = k * ka_g[...]
    kd_sum = None
    for z in (0, 1):
        at_o, rt_o, bt_o, kt_o, ct_o = zouts[z]
        sw = 1.0 / (1.0 + jnp.exp2(w0[z:z + 1, :] + _dot(lx, wl[z])))
        a = 1.0 / (1.0 + jnp.exp2(a0[z:z + 1, :] + _dot(lx, wl[2 + z])))
        lw = -(math.exp(-0.5) * LOG2E) * sw
        hi, lo = _split2(lw)
        cin = _dot(tri[z], hi) + _dot(tri[z], lo)
        kd = k + kka * (a - 1.0)
        b = kk * a
        e_neg = jnp.exp2(-cin)
        at_o[...] = (-kk * jnp.exp2(cin - lw)).astype(BF)
        rt_o[...] = (r * jnp.exp2(cin)).astype(BF)
        bt_o[...] = (b * e_neg).astype(BF)
        kt_o[...] = (kd * e_neg).astype(BF)
        end = CHUNK - 1 if z == 0 else 0
        ct_o[...] = jnp.concatenate(
            [jnp.broadcast_to(cin[c * CHUNK + end:c * CHUNK + end + 1], (8, cin.shape[1]))
             for c in range(tt // CHUNK)], axis=0)
        kd_sum = kd if kd_sum is None else kd_sum + kd

    bonus = _dot((r * kd_sum * rk_g[...]).astype(BF), bd[...])
    bvo[...] = (bonus * v).astype(bvo.dtype)
    vo[...] = v.astype(BF)


def _prep(proj, p, segs, tt, cb):
    m = proj.shape[0]
    c = p["k_k"].shape[1]
    rcol0 = (N_Q_HEADS + 2 * N_KV_HEADS) * HEAD_DIM
    nrow8 = m // 8
    t8 = tt // 8

    def trio(width, colfn):
        return [pl.BlockSpec((tt, width), lambda i, j: (i, colfn(j))),
                pl.BlockSpec((8, width), lambda i, j: (jnp.maximum(i * t8 - 1, 0), colfn(j))),
                pl.BlockSpec((8, width), lambda i, j: (jnp.minimum((i + 1) * t8, nrow8 - 1), colfn(j)))]

    specs = []
    for sec in range(3):
        base = (rcol0 + sec * c) // cb
        specs += trio(cb, lambda j, base=base: base + j)
    xblk = (rcol0 + 3 * c) // XTRA_W
    specs += trio(XTRA_W, lambda j: xblk)
    vec = lambda: pl.BlockSpec((1, cb), lambda i, j: (0, j))
    specs += [vec(), vec(), vec(), vec(), vec(), vec(),
              pl.BlockSpec((1, XTRA_W), lambda i, j: (0, 0)),
              pl.BlockSpec((1, XTRA_W), lambda i, j: (0, 0)),
              pl.BlockSpec((2, cb), lambda i, j: (0, j)),
              pl.BlockSpec((2, cb), lambda i, j: (0, j)),
              pl.BlockSpec((4, LORA_W, cb), lambda i, j: (0, 0, j)),
              pl.BlockSpec((GATE_LORA, cb), lambda i, j: (0, j)),
              vec(), vec(), vec(),
              pl.BlockSpec((cb, cb), lambda i, j: (0, 0)),
              pl.BlockSpec((2, tt, tt), lambda i, j: (0, 0, 0))]
    big = pl.BlockSpec((tt, cb), lambda i, j: (i, j))
    small = pl.BlockSpec((t8, cb), lambda i, j: (i, j))
    out_specs = ([big] * 4 + [small]) * 2 + [big] * 3
    big_s = jax.ShapeDtypeStruct((m, c), BF)
    small_s = jax.ShapeDtypeStruct((nrow8, c), F32)
    out_shape = ([big_s] * 4 + [small_s]) * 2 + [big_s] * 3
    args = [proj] * 12 + [p["sp_r"], p["sn_r"], p["sp_k"], p["sn_k"], p["sp_v"], p["sn_v"],
                          p["sp_x"], p["sn_x"], p["w0"], p["a0"], p["wl"], p["g2"],
                          p["k_k"], p["k_a"], p["r_k"], p["bd"], p["tri"]]
    return pl.pallas_call(
        functools.partial(_prep_kernel, segs, tt),
        grid=(m // tt, c // cb),
        in_specs=specs,
        out_specs=out_specs,
        out_shape=out_shape,
        compiler_params=_cp("parallel", "arbitrary"),
        name="rwkv_prep",
    )(*args)


def _scan_body(z, segs, ts, npair, at, rt, bt, kt, v, ct, y_ref, h_ref):
    i = pl.program_id(1)
    nblk = pl.num_programs(1)
    blk = i if z == 0 else nblk - 1 - i
    pos, tlen = _seq_info(blk * ts, segs)
    reset = (pos == 0) if z == 0 else (pos + ts == tlen)

    @pl.when(reset)
    def _():
        h_ref[...] = jnp.zeros_like(h_ref)

    n = 2 * CHUNK
    lane = lax.broadcasted_iota(jnp.int32, (CHUNK, LANE), 1)
    m_lo = jnp.where(lane < RWKV_HEAD, 1.0, 0.0).astype(BF)
    m_hi = jnp.where(lane < RWKV_HEAD, 0.0, 1.0).astype(BF)
    row = lax.broadcasted_iota(jnp.int32, (n, n), 0)
    col = lax.broadcasted_iota(jnp.int32, (n, n), 1)
    strict = (row > col) if z == 0 else (row < col)
    incl = (row >= col) if z == 0 else (row <= col)
    blk16 = (row // 16) == (col // 16)
    off32 = ((row // 32) == (col // 32)) & ((row // 16) != (col // 16))
    off64 = ((row // 64) == (col // 64)) & ((row // 32) != (col // 32))
    eye = jnp.where(row == col, 1.0, 0.0).astype(F32)

    nch = ts // CHUNK
    order = list(range(nch)) if z == 0 else list(range(nch - 1, -1, -1))
    items = [(p, c) for c in order for p in range(npair)]

    def tile(ref, p, c):
        return ref[c * CHUNK:(c + 1) * CHUNK, p * LANE:(p + 1) * LANE]

    def stack(x):
        return jnp.concatenate([x * m_lo, x * m_hi], axis=0)

    def each(fn, *lists):
        return [fn(*args) for args in zip(*lists)]

    xa = [stack(tile(at, p, c)) for p, c in items]
    xr = [stack(tile(rt, p, c)) for p, c in items]
    xb = [stack(tile(bt, p, c)) for p, c in items]
    xk = [stack(tile(kt, p, c)) for p, c in items]
    vs = [stack(tile(v, p, c)) for p, c in items]
    ctr = [ct[c * 8:c * 8 + 1, p * LANE:(p + 1) * LANE] for p, c in items]
    grow = [jnp.broadcast_to(jnp.exp2(r), (n, LANE)).astype(BF) for r in ctr]
    xbg = each(lambda x, g: x * g, xb, grow)
    xkg = each(lambda x, g: x * g, xk, grow)
    gam = [jnp.exp2(jnp.transpose(jnp.broadcast_to(r, (n, n)))) for r in ctr]

    pm = each(lambda a, r, b, k: _dot_nt(jnp.concatenate([a, r], axis=0), jnp.concatenate([b, k], axis=0)),
              xa, xr, xb, xk)
    a_ab = [jnp.where(strict, m[:n, :n], 0.0) for m in pm]
    a_ak = [jnp.where(strict, m[:n, n:], 0.0).astype(BF) for m in pm]
    a_rb = [jnp.where(incl, m[n:, :n], 0.0).astype(BF) for m in pm]
    a_rk = [jnp.where(incl, m[n:, n:], 0.0).astype(BF) for m in pm]

    ad = [jnp.where(blk16, a, 0.0) for a in a_ab]
    pk = [a.astype(BF) for a in ad]
    tinv = [eye + a for a in ad]
    pk = each(lambda a: _dot(a, a).astype(BF), pk)
    av = each(lambda a, w: _dot(a, w).astype(BF), a_ak, vs)
    for _ in range(2):
        both = each(lambda a, t: _dot(a, jnp.concatenate([a, t.astype(BF)], axis=1)), pk, tinv)
        pk = [m[:, :n].astype(BF) for m in both]
        tinv = each(lambda t, m: t + m[:, n:], tinv, both)
    tinv = each(lambda t, a: t + _dot(a, t.astype(BF)), tinv, pk)
    for off in (off32, off64):
        tb = [t.astype(BF) for t in tinv]
        ao = [jnp.where(off, a, 0.0).astype(BF) for a in a_ab]
        mid = each(lambda t, a: _dot(t, a).astype(BF), tb, ao)
        tinv = each(lambda t, m_, t_b: t + _dot(m_, t_b), tinv, mid, tb)

    wq = each(lambda t, a, q: _dot(t.astype(BF), jnp.concatenate([a, q], axis=1)).astype(BF), tinv, xa, av)
    zero = jnp.zeros((n, n), BF)
    wqv = each(lambda w, v_: jnp.concatenate([w, jnp.concatenate([zero, v_], axis=1)], axis=0), wq, vs)
    rb = each(lambda b, k, w: _dot(jnp.concatenate([b, k], axis=1), w), a_rb, a_rk, wqv)
    bgw = each(lambda b, k, w: _dot_tn(jnp.concatenate([b, k], axis=0), w), xbg, xkg, wqv)
    ry = each(lambda r, m_: (r.astype(F32) + m_[:, :n]).astype(BF), xr, rb)
    qy = [m_[:, n:] for m_ in rb]
    gm = [m_[:, :n].astype(BF) for m_ in bgw]
    jm = [m_[:, n:] for m_ in bgw]

    hs = [h_ref[p] for p in range(npair)]
    for idx, (p, c) in enumerate(items):
        hb = hs[p].astype(BF)
        yst = _dot(ry[idx], hb) + qy[idx]
        y_ref[c * CHUNK:(c + 1) * CHUNK, p * LANE:(p + 1) * LANE] = yst[:CHUNK] + yst[CHUNK:]
        hs[p] = gam[idx] * hs[p] + _dot(gm[idx], hb) + jm[idx]
    for p in range(npair):
        h_ref[p] = hs[p]


def _scan_kernel(z, segs, ts, npair, at, rt, bt, kt, v, ct, y_ref, h_ref):
    _scan_body(z, segs, ts, npair, at, rt, bt, kt, v, ct, y_ref, h_ref)


def _scan_finish_kernel(z, segs, ts, npair, at, rt, bt, kt, v, ct, y0, bv, g, lw, lb, bd, o_ref, h_ref, y_ref):
    _scan_body(z, segs, ts, npair, at, rt, bt, kt, v, ct, y_ref, h_ref)
    inv_n = 1.0 / RWKV_HEAD
    wb = bd.shape[0]
    for c0 in range(0, y_ref.shape[1], wb):
        ls = slice(c0, c0 + wb)
        y = y0[:, ls] + y_ref[:, ls]
        hi, lo = _split2(y)
        mu = (_dot(hi, bd[...]) + _dot(lo, bd[...])) * inv_n
        d = y - mu
        var = _dot((d * d).astype(BF), bd[...]) * inv_n
        yn = d * lax.rsqrt(var + LNX_EPS) * lw[:, ls] + lb[:, ls]
        o_ref[:, ls] = ((yn + bv[:, ls].astype(F32)) * g[:, ls].astype(F32)).astype(o_ref.dtype)


def _scan(z, ops, v, ct, segs, ts, cs, finish=None):
    m, c = v.shape
    nblk = m // ts
    npair = cs // LANE
    rowmap = (lambda j, i: (i, j)) if z == 0 else (lambda j, i: (nblk - 1 - i, j))
    big = pl.BlockSpec((ts, cs), rowmap)
    small = pl.BlockSpec((ts // 8, cs), rowmap)
    state = pltpu.VMEM((npair, LANE, LANE), F32)
    name = "rwkv_scan_fwd" if z == 0 else "rwkv_scan_bwd"
    if finish is None:
        return pl.pallas_call(
            functools.partial(_scan_kernel, z, segs, ts, npair),
            grid=(c // cs, nblk),
            in_specs=[big] * 5 + [small],
            out_specs=big,
            out_shape=jax.ShapeDtypeStruct((m, c), F32),
            scratch_shapes=[state],
            compiler_params=_cp("parallel", "arbitrary"),
            name=name,
        )(*ops, v, ct)
    vec = pl.BlockSpec((1, cs), lambda j, i: (0, j))
    return pl.pallas_call(
        functools.partial(_scan_finish_kernel, z, segs, ts, npair),
        grid=(c // cs, nblk),
        in_specs=[big] * 5 + [small] + [big] * 3 + [vec, vec, pl.BlockSpec(finish[5].shape, lambda j, i: (0, 0))],
        out_specs=big,
        out_shape=jax.ShapeDtypeStruct((m, c), BF),
        scratch_shapes=[state, pltpu.VMEM((ts, cs), F32)],
        compiler_params=_cp("parallel", "arbitrary"),
        name=name + "_finish",
    )(*ops, v, ct, *finish)


def _outproj_kernel(ka, a_ref, b_ref, w_ref, x_ref, *rest):
    o_ref = rest[-1]
    acc = _dot(a_ref[...], w_ref[:ka, :]) + _dot(b_ref[...], w_ref[ka:, :])
    o_ref[...] = x_ref[...] + acc


def _outproj(att, rw, w, layer, xs, tm, tn):
    m, ka = att.shape
    kb = rw.shape[1]
    n = w.shape[2]

    def call(x, t0, prev):
        carried = [] if prev is None else [prev]
        return pl.pallas_call(
            functools.partial(_outproj_kernel, ka),
            grid=(x.shape[0] // tm, n // tn),
            in_specs=[pl.BlockSpec((tm, ka), lambda i, j: (t0 + i, 0)),
                      pl.BlockSpec((tm, kb), lambda i, j: (t0 + i, 0)),
                      pl.BlockSpec((None, ka + kb, tn), lambda i, j: (layer, 0, j)),
                      pl.BlockSpec((tm, tn), lambda i, j: (i, j))]
            + [pl.BlockSpec(memory_space=pl.ANY)] * len(carried),
            out_specs=pl.BlockSpec((tm, tn), lambda i, j: (t0 + i, j)),
            out_shape=jax.ShapeDtypeStruct((m, n), F32),
            input_output_aliases={4: 0} if carried else {},
            compiler_params=_cp("parallel", "parallel"),
            name="out_proj",
        )(att, rw, w, x, *carried)

    return _over_parts(call, xs, tm)


def _memkv_kernel(m_ref, g_ref, wk_ref, wv_ref, kn_ref, k_out, v_out):
    mm = _rms_rows(m_ref[...], g_ref[...]).astype(BF)
    k = _dot(mm, wk_ref[...])
    for h in range(MEM_HEADS):
        hs = slice(h * MEM_HEAD_DIM, (h + 1) * MEM_HEAD_DIM)
        k_out[:, hs] = _rms_rows(k[:, hs], kn_ref[...]).astype(BF)
    v_out[...] = _dot(mm, wv_ref[...]).astype(BF)


def _memkv(mem, g, wk, wv, kn, layer):
    rows, d = mem.shape
    w = wk.shape[2]
    full = lambda shape: pl.BlockSpec(shape, lambda i: (0, 0))
    wfull = lambda shape: pl.BlockSpec((None,) + shape, lambda i: (layer, 0, 0))
    return pl.pallas_call(
        _memkv_kernel,
        grid=(rows // N_MEM,),
        in_specs=[pl.BlockSpec((N_MEM, d), lambda i: (i, 0)), full((1, d)), wfull((d, w)), wfull((d, w)),
                  full((1, MEM_HEAD_DIM))],
        out_specs=[pl.BlockSpec((N_MEM, w), lambda i: (i, 0))] * 2,
        out_shape=[jax.ShapeDtypeStruct((rows, w), BF)] * 2,
        compiler_params=_cp("parallel"),
        name="mem_kv",
    )(mem, g, wk, wv, kn)


def _memattn_kernel(x_ref, k_ref, v_ref, gm_ref, wq_ref, qn_ref, wo_ref, gf_ref, x_out, h_out):
    x = x_ref[...]
    h = _rms_rows(x, gm_ref[...]).astype(BF)
    q = _dot(h, wq_ref[...])
    scale = MEM_HEAD_DIM ** -0.5
    heads = range(MEM_HEADS)
    hsl = [slice(hd * MEM_HEAD_DIM, (hd + 1) * MEM_HEAD_DIM) for hd in heads]
    qh = [_rms_rows(q[:, hsl[hd]], qn_ref[...]).astype(BF) for hd in heads]
    s = [_dot_nt(qh[hd], k_ref[:, hsl[hd]]) * scale for hd in heads]
    p = [jnp.exp(s[hd] - jnp.max(s[hd], axis=-1, keepdims=True)) for hd in heads]
    den = [jnp.sum(p[hd], axis=-1, keepdims=True) for hd in heads]
    pv = [_dot(p[hd].astype(BF), v_ref[:, hsl[hd]]) for hd in heads]
    o = jnp.concatenate([(pv[hd] / den[hd]).astype(BF) for hd in heads], axis=1)
    x2 = x + _dot(o, wo_ref[...])
    x_out[...] = x2
    h_out[...] = _rms_rows(x2, gf_ref[...]).astype(BF)


def _memattn(x, k, v, gm, wq, qn, wo, gf, layer, segs, tm):
    m, d = x.shape
    w = wq.shape[2]
    full = lambda shape: pl.BlockSpec(shape, lambda i: (0, 0))
    wfull = lambda shape: pl.BlockSpec((None,) + shape, lambda i: (layer, 0, 0))
    kv = pl.BlockSpec((N_MEM, w), lambda i: (_batch_of(i * tm, segs), 0))
    row = pl.BlockSpec((tm, d), lambda i: (i, 0))
    return pl.pallas_call(
        _memattn_kernel,
        grid=(m // tm,),
        in_specs=[row, kv, kv, full((1, d)), wfull((d, w)), full((1, MEM_HEAD_DIM)), wfull((w, d)), full((1, d))],
        out_specs=[row, row],
        out_shape=[jax.ShapeDtypeStruct((m, d), F32), jax.ShapeDtypeStruct((m, d), BF)],
        compiler_params=_cp("parallel"),
        name="mem_attn",
    )(x, k, v, gm, wq, qn, wo, gf)


def _ffn_up_kernel(h_ref, wg_ref, wu_ref, o_ref):
    h = h_ref[...]
    gate = _dot(h, wg_ref[...])
    up = _dot(h, wu_ref[...])
    o_ref[...] = (gate * jax.nn.sigmoid(gate) * up).astype(o_ref.dtype)


def _ffn_up(h, wg, wu, layer, tm, tn):
    m, d = h.shape
    f = wg.shape[2]
    wspec = pl.BlockSpec((None, d, tn), lambda i, j: (layer, 0, j))
    return pl.pallas_call(
        _ffn_up_kernel,
        grid=(m // tm, f // tn),
        in_specs=[pl.BlockSpec((tm, d), lambda i, j: (i, 0)), wspec, wspec],
        out_specs=pl.BlockSpec((tm, tn), lambda i, j: (i, j)),
        out_shape=jax.ShapeDtypeStruct((m, f), BF),
        compiler_params=_cp("parallel", "parallel"),
        name="ffn_up",
    )(h, wg, wu)


def _ffn_down_kernel(a_ref, w_ref, x_ref, o_ref):
    o_ref[...] = x_ref[...] + _dot(a_ref[...], w_ref[...])


def _ffn_down(act, wd, layer, x, tm, tn, row0=0, rows=None):
    f = act.shape[1]
    n = wd.shape[2]
    rows = act.shape[0] if rows is None else rows
    assert row0 % tm == 0 and rows % tm == 0
    t0 = row0 // tm
    return pl.pallas_call(
        _ffn_down_kernel,
        grid=(rows // tm, n // tn),
        in_specs=[pl.BlockSpec((tm, f), lambda i, j: (t0 + i, 0)),
                  pl.BlockSpec((None, f, tn), lambda i, j: (layer, 0, j)),
                  pl.BlockSpec((tm, tn), lambda i, j: (t0 + i, j))],
        out_specs=pl.BlockSpec((tm, tn), lambda i, j: (i, j)),
        out_shape=jax.ShapeDtypeStruct((rows, n), F32),
        compiler_params=_cp("parallel", "parallel"),
        name="ffn_down",
    )(act, wd, x)


def _rel_bucket(rel):
    half = N_REL_BUCKETS // 2
    exact = half // 2
    n = np.abs(rel)
    large = exact + (np.log(np.maximum(n, 1) / exact) / np.log(REL_MAX_DIST / exact)
                     * (half - exact)).astype(np.int32)
    large = np.minimum(large, half - 1)
    return (rel > 0).astype(np.int32) * half + np.where(n < exact, n, large)


def _attn_bias(rel_bias):
    qi = np.arange(BLOCK)[:, None]
    kj = np.arange(3 * BLOCK)[None, :]
    rel = kj - BLOCK - qi
    onehot = (_rel_bucket(rel)[..., None] == np.arange(N_REL_BUCKETS)).astype(np.float32)
    bias = jnp.einsum("qkb,bh->hqk", jnp.asarray(onehot), rel_bias.astype(F32), precision=lax.Precision.HIGHEST)
    return jnp.where(jnp.asarray(np.abs(rel) <= BLOCK)[None], bias, NEG)


def _tri_consts(tt):
    t = np.arange(tt)[:, None]
    s = np.arange(tt)[None, :]
    same = (t // CHUNK) == (s // CHUNK)
    mats = [same & (s <= t), same & (s >= t)]
    return jnp.asarray(np.stack(mats).astype(np.float32), dtype=BF)


def _block_diag_ones(cb):
    i = np.arange(cb)
    return jnp.asarray(((i[:, None] // RWKV_HEAD) == (i[None, :] // RWKV_HEAD)).astype(np.float32), dtype=BF)


def _pick(n, pref):
    t = min(n, pref)
    assert n % t == 0, (n, t)
    return t


def kernel(x_prompt, x_sample, mem_prompt, mem_sample, rel_bias, norm_mix, w_in, q_norm, k_norm, sink,
           shift_prev, shift_next, w0, w2, a0, a2, g2, k_k, k_a, r_k, lnx_w, lnx_b, w_out,
           norm_mem, norm_memkv, wq_mem, wk_mem, wv_mem, wo_mem, qn_mem, kn_mem,
           norm_ffn, w_gate, w_up, w_down):
    b1, t1, d = x_prompt.shape
    b2, t2, _ = x_sample.shape
    segs = ((b1, t1), (b2, t2))
    m1 = b1 * t1
    m = m1 + b2 * t2
    xs = [x_prompt.reshape(m1, d), x_sample.reshape(b2 * t2, d)]
    mem = jnp.concatenate([mem_prompt.reshape(-1, d), mem_sample.reshape(-1, d)], axis=0)
    depth = w_in.shape[0]
    c = k_k.shape[1]
    lora0 = 3 * c

    mg = math.gcd(m1, m - m1)
    tm_in = _pick(mg, 1024)
    tn_in = 768
    tm_out, tn_out = _pick(mg, 1024), 1024
    tt = _pick(min(t1, t2), 256)
    cb = _pick(c, 512)
    ts = _pick(min(t1, t2), 256)
    cs = _pick(c, 1024)
    tm_mem = _pick(min(t1, t2), 256)
    tm_up, tn_up = _pick(m, 2048), 256
    tm_down, tn_down = _pick(mg, 512), 512

    bias = _attn_bias(rel_bias)
    tri = _tri_consts(tt)
    bd = _block_diag_ones(cb)
    row = lambda a: a.reshape(1, -1).astype(F32)

    w_in_b = jnp.pad(w_in.astype(BF), ((0, 0), (0, 0), (0, XTRA_W - LORA_W - GATE_LORA)))
    w_out_b = w_out.astype(BF)
    wq_b, wk_b, wv_b, wo_b = (w.astype(BF) for w in (wq_mem, wk_mem, wv_mem, wo_mem))
    wg_b, wu_b, wd_b = (w.astype(BF) for w in (w_gate, w_up, w_down))

    for l in range(depth):
        sp, sn = shift_prev[l], shift_next[l]
        padx = lambda a: jnp.pad(a[lora0:], (0, XTRA_W - LORA_W - GATE_LORA)).reshape(1, -1)
        wl = jnp.zeros((4, LORA_W, c), F32)
        wl = wl.at[0, 0:DECAY_LORA].set(w2[l, 0]).at[1, DECAY_LORA:2 * DECAY_LORA].set(w2[l, 1])
        wl = wl.at[2, 2 * DECAY_LORA:2 * DECAY_LORA + AAA_LORA].set(a2[l, 0])
        wl = wl.at[3, 2 * DECAY_LORA + AAA_LORA:].set(a2[l, 1])
        prm = dict(sp_r=row(sp[0:c]), sn_r=row(sn[0:c]), sp_k=row(sp[c:2 * c]), sn_k=row(sn[c:2 * c]),
                   sp_v=row(sp[2 * c:3 * c]), sn_v=row(sn[2 * c:3 * c]), sp_x=padx(sp), sn_x=padx(sn),
                   w0=-LOG2E * w0[l], a0=-LOG2E * a0[l], wl=(-LOG2E * wl).astype(BF), g2=g2[l].astype(BF),
                   k_k=row(k_k[l]), k_a=row(k_a[l]), r_k=row(r_k[l]), bd=bd, tri=tri)

        proj = _inproj(xs, row(norm_mix[l]), w_in_b, l, tm_in, tn_in, m)
        att = _attention(proj, bias, row(q_norm[l]), row(k_norm[l]), sink[l].astype(F32), segs)
        po = _prep(proj, prm, segs, tt, cb)
        v_b, bv, g = po[10], po[11], po[12]
        y0 = _scan(0, po[0:4], v_b, po[4], segs, ts, cs)
        rw = _scan(1, po[5:9], v_b, po[9], segs, ts, cs,
                   finish=(y0, bv, g, row(lnx_w[l]), row(lnx_b[l]), bd))
        x = _outproj(att, rw, w_out_b, l, xs, tm_out, tn_out)

        km, vm = _memkv(mem, row(norm_memkv[l]), wk_b, wv_b, row(kn_mem[l]), l)
        x, h3 = _memattn(x, km, vm, row(norm_mem[l]), wq_b, row(qn_mem[l]), wo_b, row(norm_ffn[l]),
                         l, segs, tm_mem)

        act = _ffn_up(h3, wg_b, wu_b, l, tm_up, tn_up)
        if l + 1 < depth:
            xs = [_ffn_down(act, wd_b, l, x, tm_down, tn_down)]
    y1 = _ffn_down(act, wd_b, depth - 1, x, tm_down, tn_down, 0, m1)
    y2 = _ffn_down(act, wd_b, depth - 1, x, tm_down, tn_down, m1, m - m1)
    return (y1.reshape(b1, t1, d), y2.reshape(b2, t2, d))
```

```python
import functools
import math

import numpy as np
import jax
import jax.numpy as jnp
from jax import lax
from jax.experimental import pallas as pl
from jax.experimental.pallas import tpu as pltpu

F32 = jnp.float32
BF = jnp.bfloat16

HEAD_DIM = 128
N_Q_HEADS = 16
N_KV_HEADS = 4
GQA = N_Q_HEADS // N_KV_HEADS
BLOCK = 128
N_REL_BUCKETS = 32
REL_MAX_DIST = 128
RWKV_HEAD = 64
DECAY_LORA = 96
AAA_LORA = 96
GATE_LORA = 256
LNX_EPS = 64e-5
MEM_HEADS = 4
MEM_HEAD_DIM = 128
N_MEM = 256
RMS_EPS = 1e-6
NEG = -1e30
LOG2E = math.log2(math.e)

LANE = 128
CHUNK = 64
LORA_W = 2 * DECAY_LORA + 2 * AAA_LORA
XTRA_W = 768
VMEM_LIMIT = 56 * 1024 * 1024


def _cp(*sem):
    return pltpu.CompilerParams(dimension_semantics=sem, vmem_limit_bytes=VMEM_LIMIT)


def _dot(a, b):
    return jnp.dot(a, b, preferred_element_type=F32)


def _dot_nt(a, b):
    return lax.dot_general(a, b, (((1,), (1,)), ((), ())), preferred_element_type=F32)


def _dot_tn(a, b):
    return lax.dot_general(a, b, (((0,), (0,)), ((), ())), preferred_element_type=F32)


def _split2(x):
    hi = x.astype(BF)
    lo = (x - hi.astype(F32)).astype(BF)
    return hi, lo


def _rms_rows(x, g):
    ms = jnp.mean(x * x, axis=-1, keepdims=True)
    return x * lax.rsqrt(ms + RMS_EPS) * g


def _seq_info(row, segs):
    off = 0
    pos = None
    tlen = None
    for n, t in segs:
        p = lax.rem(row - off, t)
        if pos is None:
            pos, tlen = p, jnp.int32(t)
        else:
            inside = row >= off
            pos = jnp.where(inside, p, pos)
            tlen = jnp.where(inside, t, tlen)
        off += n * t
    return pos, tlen


def _batch_of(row, segs):
    off = 0
    boff = 0
    res = None
    for n, t in segs:
        b = boff + (row - off) // t
        res = b if res is None else jnp.where(row >= off, b, res)
        off += n * t
        boff += n
    return res


def _over_parts(call, parts, tm):
    out, t0 = None, 0
    for part in parts:
        assert part.shape[0] % tm == 0
        out = call(part, t0, out)
        t0 += part.shape[0] // tm
    return out


def _inproj_kernel(x_ref, g_ref, w_ref, *rest):
    o_ref, h_ref = rest[-2:]

    @pl.when(pl.program_id(1) == 0)
    def _():
        rows = x_ref.shape[0]
        step = min(rows, 256)
        for r0 in range(0, rows, step):
            h_ref[r0:r0 + step, :] = _rms_rows(x_ref[r0:r0 + step, :], g_ref[...]).astype(BF)

    o_ref[...] = _dot(h_ref[...], w_ref[...])


def _inproj(xs, g, w, layer, tm, tn, m):
    d = xs[0].shape[1]
    n = w.shape[2]

    def call(x, t0, prev):
        carried = [] if prev is None else [prev]
        return pl.pallas_call(
            _inproj_kernel,
            grid=(x.shape[0] // tm, n // tn),
            in_specs=[pl.BlockSpec((tm, d), lambda i, j: (i, 0), pipeline_mode=pl.Buffered(1)),
                      pl.BlockSpec((1, d), lambda i, j: (0, 0)),
                      pl.BlockSpec((None, d, tn), lambda i, j: (layer, 0, j))]
            + [pl.BlockSpec(memory_space=pl.ANY)] * len(carried),
            out_specs=pl.BlockSpec((tm, tn), lambda i, j: (t0 + i, j)),
            out_shape=jax.ShapeDtypeStruct((m, n), F32),
            input_output_aliases={3: 0} if carried else {},
            scratch_shapes=[pltpu.VMEM((tm, d), BF)],
            compiler_params=_cp("parallel", "arbitrary"),
            name="in_proj",
        )(x, g, w, *carried)

    return _over_parts(call, xs, tm)


def _attn_kernel(segs, q_ref, kp_ref, kc_ref, kn_ref, vp_ref, vc_ref, vn_ref,
                 bias_ref, qn_ref, kn_g_ref, sink_ref, o_ref):
    n = pl.program_id(0)
    pos, tlen = _seq_info(n * BLOCK, segs)
    first = pos == 0
    last = pos + BLOCK == tlen
    col = lax.broadcasted_iota(jnp.int32, (BLOCK, 3 * BLOCK), 1)
    dead = (first & (col < BLOCK)) | (last & (col >= 2 * BLOCK))
    scale = HEAD_DIM ** -0.5
    heads = range(N_Q_HEADS)
    hsl = [slice(h * HEAD_DIM, (h + 1) * HEAD_DIM) for h in heads]
    kw, vw = [], []
    for kh in range(N_KV_HEADS):
        ls = hsl[kh]
        kcat = jnp.concatenate([kp_ref[:, ls], kc_ref[:, ls], kn_ref[:, ls]], axis=0)
        kw.append(_rms_rows(kcat, kn_g_ref[...]).astype(BF))
        vw.append(jnp.concatenate([vp_ref[:, ls], vc_ref[:, ls], vn_ref[:, ls]], axis=0).astype(BF))
    q = [_rms_rows(q_ref[:, hsl[h]], qn_ref[...]).astype(BF) for h in heads]
    s = [_dot_nt(q[h], kw[h // GQA]) * scale + bias_ref[h] for h in heads]
    s = [jnp.where(dead, NEG, s[h]) for h in heads]
    mx = [jnp.maximum(jnp.max(s[h], axis=-1, keepdims=True), sink_ref[h]) for h in heads]
    p = [jnp.exp(s[h] - mx[h]) for h in heads]
    den = [jnp.sum(p[h], axis=-1, keepdims=True) + jnp.exp(sink_ref[h] - mx[h]) for h in heads]
    o = [_dot(p[h].astype(BF), vw[h // GQA]) for h in heads]
    for h in heads:
        o_ref[:, hsl[h]] = (o[h] / den[h]).astype(o_ref.dtype)


def _attention(proj, bias, qn, kn, sink, segs):
    m = proj.shape[0]
    nb = m // BLOCK
    aw = N_Q_HEADS * HEAD_DIM
    kvw = N_KV_HEADS * HEAD_DIM
    kcol = aw // kvw
    vcol = kcol + 1
    prev = lambda n: jnp.maximum(n - 1, 0)
    nxt = lambda n: jnp.minimum(n + 1, nb - 1)
    specs = [pl.BlockSpec((BLOCK, aw), lambda n: (n, 0))]
    for c in (kcol, vcol):
        specs += [pl.BlockSpec((BLOCK, kvw), lambda n, c=c: (prev(n), c)),
                  pl.BlockSpec((BLOCK, kvw), lambda n, c=c: (n, c)),
                  pl.BlockSpec((BLOCK, kvw), lambda n, c=c: (nxt(n), c))]
    specs += [pl.BlockSpec((N_Q_HEADS, BLOCK, 3 * BLOCK), lambda n: (0, 0, 0)),
              pl.BlockSpec((1, HEAD_DIM), lambda n: (0, 0)),
              pl.BlockSpec((1, HEAD_DIM), lambda n: (0, 0)),
              pl.BlockSpec(memory_space=pltpu.SMEM)]
    return pl.pallas_call(
        functools.partial(_attn_kernel, segs),
        grid=(nb,),
        in_specs=specs,
        out_specs=pl.BlockSpec((BLOCK, aw), lambda n: (n, 0)),
        out_shape=jax.ShapeDtypeStruct((m, aw), BF),
        compiler_params=_cp("parallel"),
        name="window_attn",
    )(proj, proj, proj, proj, proj, proj, proj, bias, qn, kn, sink)


def _prep_kernel(segs, tt, *refs):
    (pr, pr_p, pr_n, pk, pk_p, pk_n, pv, pv_p, pv_n, px, px_p, px_n,
     spr, snr, spk, snk, spv, snv, spx, snx, w0, a0, wl, g2, kk_g, ka_g, rk_g, bd, tri) = refs[:29]
    outs = refs[29:]
    zouts = (outs[0:5], outs[5:10])
    vo, bvo, go = outs[10:13]

    row0 = pl.program_id(0) * tt
    pos, tlen = _seq_info(row0, segs)
    first = pos == 0
    last = pos + tt == tlen
    rows = lax.broadcasted_iota(jnp.int32, (tt, 1), 0)

    def shifted(x_ref, p_ref, n_ref, sp, sn):
        x = x_ref[...]
        prow = jnp.where(first, 0.0, p_ref[7:8, :])
        nrow = jnp.where(last, 0.0, n_ref[0:1, :])
        prev = jnp.where(rows == 0, prow, pltpu.roll(x, 1, 0))
        nxt = jnp.where(rows == tt - 1, nrow, pltpu.roll(x, tt - 1, 0))
        return x + sp[...] * (prev - x) + sn[...] * (nxt - x)

    r = shifted(pr, pr_p, pr_n, spr, snr)
    k = shifted(pk, pk_p, pk_n, spk, snk)
    v = shifted(pv, pv_p, pv_n, spv, snv)
    x = shifted(px, px_p, px_n, spx, snx)

    xl = x[:, :LORA_W]
    lane = lax.broadcasted_iota(jnp.int32, xl.shape, 1)
    lx = jnp.where(lane < 2 * DECAY_LORA, jnp.tanh(xl), xl).astype(BF)
    sg = jax.nn.sigmoid(x[:, LORA_W:LORA_W + GATE_LORA]).astype(BF)
    go[...] = _dot(sg, g2[...]).astype(go.dtype)

    kkr = k * kk_g[...]
    def head_sums(t):
        wb = bd.shape[0]
        tb = t.astype(BF)
        return jnp.concatenate([_dot(tb[:, c0:c0 + wb], bd[...]) for c0 in range(0, t.shape[1], wb)], axis=1)

    n2 = head_sums(kkr * kkr)
    kk = kkr / jnp.maximum(jnp.sqrt(n2), 1e-12)

    kka = k * ka_g[...]
    kd_sum = None
    for z in (0, 1):
        at_o, rt_o, bt_o, kt_o, ct_o = zouts[z]
        sw = 1.0 / (1.0 + jnp.exp2(w0[z:z + 1, :] + _dot(lx, wl[z])))
        a = 1.0 / (1.0 + jnp.exp2(a0[z:z + 1, :] + _dot(lx, wl[2 + z])))
        lw = -(math.exp(-0.5) * LOG2E) * sw
        hi, lo = _split2(lw)
        cin = _dot(tri[z], hi) + _dot(tri[z], lo)
        kd = k + kka * (a - 1.0)
        b = kk * a
        e_neg = jnp.exp2(-cin)
        at_o[...] = (-kk * jnp.exp2(cin - lw)).astype(BF)
        rt_o[...] = (r * jnp.exp2(cin)).astype(BF)
        bt_o[...] = (b * e_neg).astype(BF)
        kt_o[...] = (kd * e_neg).astype(BF)
        end = CHUNK - 1 if z == 0 else 0
        ct_o[...] = jnp.concatenate(
            [jnp.broadcast_to(cin[c * CHUNK + end:c * CHUNK + end + 1], (8, cin.shape[1]))
             for c in range(tt // CHUNK)], axis=0)
        kd_sum = kd if kd_sum is None else kd_sum + kd

    bonus = head_sums(r * kd_sum * rk_g[...])
    bvo[...] = (bonus * v).astype(bvo.dtype)
    vo[...] = v.astype(BF)


def _prep(proj, p, segs, tt, cb):
    m = proj.shape[0]
    c = p["k_k"].shape[1]
    rcol0 = (N_Q_HEADS + 2 * N_KV_HEADS) * HEAD_DIM
    nrow8 = m // 8
    t8 = tt // 8

    def trio(width, colfn):
        return [pl.BlockSpec((tt, width), lambda i, j: (i, colfn(j))),
                pl.BlockSpec((8, width), lambda i, j: (jnp.maximum(i * t8 - 1, 0), colfn(j))),
                pl.BlockSpec((8, width), lambda i, j: (jnp.minimum((i + 1) * t8, nrow8 - 1), colfn(j)))]

    specs = []
    for sec in range(3):
        base = (rcol0 + sec * c) // cb
        specs += trio(cb, lambda j, base=base: base + j)
    xblk = (rcol0 + 3 * c) // XTRA_W
    specs += trio(XTRA_W, lambda j: xblk)
    vec = lambda: pl.BlockSpec((1, cb), lambda i, j: (0, j))
    specs += [vec(), vec(), vec(), vec(), vec(), vec(),
              pl.BlockSpec((1, XTRA_W), lambda i, j: (0, 0)),
              pl.BlockSpec((1, XTRA_W), lambda i, j: (0, 0)),
              pl.BlockSpec((2, cb), lambda i, j: (0, j)),
              pl.BlockSpec((2, cb), lambda i, j: (0, j)),
              pl.BlockSpec((4, LORA_W, cb), lambda i, j: (0, 0, j)),
              pl.BlockSpec((GATE_LORA, cb), lambda i, j: (0, j)),
              vec(), vec(), vec(),
              pl.BlockSpec(p["bd"].shape, lambda i, j: (0, 0)),
              pl.BlockSpec((2, tt, tt), lambda i, j: (0, 0, 0))]
    big = pl.BlockSpec((tt, cb), lambda i, j: (i, j))
    small = pl.BlockSpec((t8, cb), lambda i, j: (i, j))
    out_specs = ([big] * 4 + [small]) * 2 + [big] * 3
    big_s = jax.ShapeDtypeStruct((m, c), BF)
    small_s = jax.ShapeDtypeStruct((nrow8, c), F32)
    out_shape = ([big_s] * 4 + [small_s]) * 2 + [big_s] * 3
    args = [proj] * 12 + [p["sp_r"], p["sn_r"], p["sp_k"], p["sn_k"], p["sp_v"], p["sn_v"],
                          p["sp_x"], p["sn_x"], p["w0"], p["a0"], p["wl"], p["g2"],
                          p["k_k"], p["k_a"], p["r_k"], p["bd"], p["tri"]]
    return pl.pallas_call(
        functools.partial(_prep_kernel, segs, tt),
        grid=(m // tt, c // cb),
        in_specs=specs,
        out_specs=out_specs,
        out_shape=out_shape,
        compiler_params=_cp("parallel", "arbitrary"),
        name="rwkv_prep",
    )(*args)


def _scan_body(z, segs, ts, npair, at, rt, bt, kt, v, ct, y_ref, h_ref):
    i = pl.program_id(1)
    nblk = pl.num_programs(1)
    blk = i if z == 0 else nblk - 1 - i
    pos, tlen = _seq_info(blk * ts, segs)
    reset = (pos == 0) if z == 0 else (pos + ts == tlen)

    @pl.when(reset)
    def _():
        h_ref[...] = jnp.zeros_like(h_ref)

    n = 2 * CHUNK
    lane = lax.broadcasted_iota(jnp.int32, (CHUNK, LANE), 1)
    m_lo = jnp.where(lane < RWKV_HEAD, 1.0, 0.0).astype(BF)
    m_hi = jnp.where(lane < RWKV_HEAD, 0.0, 1.0).astype(BF)
    row = lax.broadcasted_iota(jnp.int32, (n, n), 0)
    col = lax.broadcasted_iota(jnp.int32, (n, n), 1)
    strict = (row > col) if z == 0 else (row < col)
    incl = (row >= col) if z == 0 else (row <= col)
    blk16 = (row // 16) == (col // 16)
    off32 = ((row // 32) == (col // 32)) & ((row // 16) != (col // 16))
    off64 = ((row // 64) == (col // 64)) & ((row // 32) != (col // 32))
    eye = jnp.where(row == col, 1.0, 0.0).astype(F32)

    nch = ts // CHUNK
    order = list(range(nch)) if z == 0 else list(range(nch - 1, -1, -1))
    pairs = list(range(npair))

    def tile(ref, p, c):
        return ref[c * CHUNK:(c + 1) * CHUNK, p * LANE:(p + 1) * LANE]

    def stack(x):
        return jnp.concatenate([x * m_lo, x * m_hi], axis=0)

    def each(fn, *lists):
        return [fn(*args) for args in zip(*lists)]

    def state_free(c, out):
        xa = [stack(tile(at, p, c)) for p in pairs]
        xr = [stack(tile(rt, p, c)) for p in pairs]
        xb = [stack(tile(bt, p, c)) for p in pairs]
        xk = [stack(tile(kt, p, c)) for p in pairs]
        vs = [stack(tile(v, p, c)) for p in pairs]
        ctr = [ct[c * 8:c * 8 + 1, p * LANE:(p + 1) * LANE] for p in pairs]
        grow = [jnp.broadcast_to(jnp.exp2(r), (n, LANE)).astype(BF) for r in ctr]
        xbg = each(lambda x, g: x * g, xb, grow)
        xkg = each(lambda x, g: x * g, xk, grow)
        gam = [jnp.exp2(jnp.transpose(jnp.broadcast_to(r, (n, n)))) for r in ctr]
        yield
        pm = each(lambda a, r, b, k: _dot_nt(jnp.concatenate([a, r], axis=0), jnp.concatenate([b, k], axis=0)),
                  xa, xr, xb, xk)
        a_ab = [jnp.where(strict, m[:n, :n], 0.0) for m in pm]
        a_ak = [jnp.where(strict, m[:n, n:], 0.0).astype(BF) for m in pm]
        a_rb = [jnp.where(incl, m[n:, :n], 0.0).astype(BF) for m in pm]
        a_rk = [jnp.where(incl, m[n:, n:], 0.0).astype(BF) for m in pm]
        yield
        ad = [jnp.where(blk16, a, 0.0) for a in a_ab]
        pk = [a.astype(BF) for a in ad]
        tinv = [eye + a for a in ad]
        pk = each(lambda a: _dot(a, a).astype(BF), pk)
        av = each(lambda a, w: _dot(a, w).astype(BF), a_ak, vs)
        yield
        for _ in range(2):
            both = each(lambda a, t: _dot(a, jnp.concatenate([a, t.astype(BF)], axis=1)), pk, tinv)
            pk = [m[:, :n].astype(BF) for m in both]
            tinv = each(lambda t, m: t + m[:, n:], tinv, both)
            yield
        tinv = each(lambda t, a: t + _dot(a, t.astype(BF)), tinv, pk)
        yield
        for off in (off32, off64):
            tb = [t.astype(BF) for t in tinv]
            ao = [jnp.where(off, a, 0.0).astype(BF) for a in a_ab]
            mid = each(lambda t, a: _dot(t, a).astype(BF), tb, ao)
            yield
            tinv = each(lambda t, m_, t_b: t + _dot(m_, t_b), tinv, mid, tb)
            yield
        wq = each(lambda t, a, q: _dot(t.astype(BF), jnp.concatenate([a, q], axis=1)).astype(BF), tinv, xa, av)
        yield
        zero = jnp.zeros((n, n), BF)
        wqv = each(lambda w, v_: jnp.concatenate([w, jnp.concatenate([zero, v_], axis=1)], axis=0), wq, vs)
        rb = each(lambda b, k, w: _dot(jnp.concatenate([b, k], axis=1), w), a_rb, a_rk, wqv)
        bgw = each(lambda b, k, w: _dot_tn(jnp.concatenate([b, k], axis=0), w), xbg, xkg, wqv)
        out["ry"] = each(lambda r, m_: (r.astype(F32) + m_[:, :n]).astype(BF), xr, rb)
        out["qy"] = [m_[:, n:] for m_ in rb]
        out["gm"] = [m_[:, :n].astype(BF) for m_ in bgw]
        out["jm"] = [m_[:, n:] for m_ in bgw]
        out["gam"] = gam

    hs = [h_ref[p] for p in pairs]

    def state_step(c, res):
        for p in pairs:
            hb = hs[p].astype(BF)
            yst = _dot(res["ry"][p], hb) + res["qy"][p]
            y_ref[c * CHUNK:(c + 1) * CHUNK, p * LANE:(p + 1) * LANE] = yst[:CHUNK] + yst[CHUNK:]
            hs[p] = res["gam"][p] * hs[p] + _dot(res["gm"][p], hb) + res["jm"][p]

    results = [dict() for _ in order]
    gens = [state_free(c, results[k]) for k, c in enumerate(order)]
    live = list(range(len(order)))
    while live:
        for k in list(live):
            try:
                next(gens[k])
            except StopIteration:
                live.remove(k)
    for k, c in enumerate(order):
        state_step(c, results[k])
    for p in pairs:
        h_ref[p] = hs[p]


def _scan_kernel(z, segs, ts, npair, at, rt, bt, kt, v, ct, y_ref, h_ref):
    _scan_body(z, segs, ts, npair, at, rt, bt, kt, v, ct, y_ref, h_ref)


def _scan_finish_kernel(z, segs, ts, npair, at, rt, bt, kt, v, ct, y0, bv, g, lw, lb, bd, o_ref, h_ref, y_ref):
    _scan_body(z, segs, ts, npair, at, rt, bt, kt, v, ct, y_ref, h_ref)
    inv_n = 1.0 / RWKV_HEAD
    wb = bd.shape[0]
    for c0 in range(0, y_ref.shape[1], wb):
        ls = slice(c0, c0 + wb)
        y = y0[:, ls] + y_ref[:, ls]
        hi, lo = _split2(y)
        mu = (_dot(hi, bd[...]) + _dot(lo, bd[...])) * inv_n
        d = y - mu
        var = _dot((d * d).astype(BF), bd[...]) * inv_n
        yn = d * lax.rsqrt(var + LNX_EPS) * lw[:, ls] + lb[:, ls]
        o_ref[:, ls] = ((yn + bv[:, ls].astype(F32)) * g[:, ls].astype(F32)).astype(o_ref.dtype)


def _scan(z, ops, v, ct, segs, ts, cs, finish=None):
    m, c = v.shape
    nblk = m // ts
    npair = cs // LANE
    rowmap = (lambda j, i: (i, j)) if z == 0 else (lambda j, i: (nblk - 1 - i, j))
    big = pl.BlockSpec((ts, cs), rowmap)
    small = pl.BlockSpec((ts // 8, cs), rowmap)
    state = pltpu.VMEM((npair, LANE, LANE), F32)
    name = "rwkv_scan_fwd" if z == 0 else "rwkv_scan_bwd"
    if finish is None:
        return pl.pallas_call(
            functools.partial(_scan_kernel, z, segs, ts, npair),
            grid=(c // cs, nblk),
            in_specs=[big] * 5 + [small],
            out_specs=big,
            out_shape=jax.ShapeDtypeStruct((m, c), F32),
            scratch_shapes=[state],
            compiler_params=_cp("parallel", "arbitrary"),
            name=name,
        )(*ops, v, ct)
    vec = pl.BlockSpec((1, cs), lambda j, i: (0, j))
    return pl.pallas_call(
        functools.partial(_scan_finish_kernel, z, segs, ts, npair),
        grid=(c // cs, nblk),
        in_specs=[big] * 5 + [small] + [big] * 3 + [vec, vec, pl.BlockSpec(finish[5].shape, lambda j, i: (0, 0))],
        out_specs=big,
        out_shape=jax.ShapeDtypeStruct((m, c), BF),
        scratch_shapes=[state, pltpu.VMEM((ts, cs), F32)],
        compiler_params=_cp("parallel", "arbitrary"),
        name=name + "_finish",
    )(*ops, v, ct, *finish)


def _outproj_kernel(ka, a_ref, b_ref, w_ref, x_ref, *rest):
    o_ref = rest[-1]
    acc = _dot(a_ref[...], w_ref[:ka, :]) + _dot(b_ref[...], w_ref[ka:, :])
    o_ref[...] = x_ref[...] + acc


def _outproj(att, rw, w, layer, xs, tm, tn):
    m, ka = att.shape
    kb = rw.shape[1]
    n = w.shape[2]

    def call(x, t0, prev):
        carried = [] if prev is None else [prev]
        return pl.pallas_call(
            functools.partial(_outproj_kernel, ka),
            grid=(x.shape[0] // tm, n // tn),
            in_specs=[pl.BlockSpec((tm, ka), lambda i, j: (t0 + i, 0)),
                      pl.BlockSpec((tm, kb), lambda i, j: (t0 + i, 0)),
                      pl.BlockSpec((None, ka + kb, tn), lambda i, j: (layer, 0, j)),
                      pl.BlockSpec((tm, tn), lambda i, j: (i, j))]
            + [pl.BlockSpec(memory_space=pl.ANY)] * len(carried),
            out_specs=pl.BlockSpec((tm, tn), lambda i, j: (t0 + i, j)),
            out_shape=jax.ShapeDtypeStruct((m, n), F32),
            input_output_aliases={4: 0} if carried else {},
            compiler_params=_cp("parallel", "parallel"),
            name="out_proj",
        )(att, rw, w, x, *carried)

    return _over_parts(call, xs, tm)


def _memkv_kernel(m_ref, g_ref, wk_ref, wv_ref, kn_ref, k_out, v_out):
    mm = _rms_rows(m_ref[...], g_ref[...]).astype(BF)
    k = _dot(mm, wk_ref[...])
    for h in range(MEM_HEADS):
        hs = slice(h * MEM_HEAD_DIM, (h + 1) * MEM_HEAD_DIM)
        k_out[:, hs] = _rms_rows(k[:, hs], kn_ref[...]).astype(BF)
    v_out[...] = _dot(mm, wv_ref[...]).astype(BF)


def _memkv(mem, g, wk, wv, kn, layer):
    rows, d = mem.shape
    w = wk.shape[2]
    full = lambda shape: pl.BlockSpec(shape, lambda i: (0, 0))
    wfull = lambda shape: pl.BlockSpec((None,) + shape, lambda i: (layer, 0, 0))
    return pl.pallas_call(
        _memkv_kernel,
        grid=(rows // N_MEM,),
        in_specs=[pl.BlockSpec((N_MEM, d), lambda i: (i, 0)), full((1, d)), wfull((d, w)), wfull((d, w)),
                  full((1, MEM_HEAD_DIM))],
        out_specs=[pl.BlockSpec((N_MEM, w), lambda i: (i, 0))] * 2,
        out_shape=[jax.ShapeDtypeStruct((rows, w), BF)] * 2,
        compiler_params=_cp("parallel"),
        name="mem_kv",
    )(mem, g, wk, wv, kn)


def _memattn_kernel(x_ref, k_ref, v_ref, gm_ref, wq_ref, qn_ref, wo_ref, gf_ref, x_out, h_out):
    x = x_ref[...]
    h = _rms_rows(x, gm_ref[...]).astype(BF)
    q = _dot(h, wq_ref[...])
    scale = MEM_HEAD_DIM ** -0.5
    heads = range(MEM_HEADS)
    hsl = [slice(hd * MEM_HEAD_DIM, (hd + 1) * MEM_HEAD_DIM) for hd in heads]
    qh = [_rms_rows(q[:, hsl[hd]], qn_ref[...]).astype(BF) for hd in heads]
    s = [_dot_nt(qh[hd], k_ref[:, hsl[hd]]) * scale for hd in heads]
    p = [jnp.exp(s[hd] - jnp.max(s[hd], axis=-1, keepdims=True)) for hd in heads]
    den = [jnp.sum(p[hd], axis=-1, keepdims=True) for hd in heads]
    pv = [_dot(p[hd].astype(BF), v_ref[:, hsl[hd]]) for hd in heads]
    o = jnp.concatenate([(pv[hd] / den[hd]).astype(BF) for hd in heads], axis=1)
    x2 = x + _dot(o, wo_ref[...])
    x_out[...] = x2
    h_out[...] = _rms_rows(x2, gf_ref[...]).astype(BF)


def _memattn(x, k, v, gm, wq, qn, wo, gf, layer, segs, tm):
    m, d = x.shape
    w = wq.shape[2]
    full = lambda shape: pl.BlockSpec(shape, lambda i: (0, 0))
    wfull = lambda shape: pl.BlockSpec((None,) + shape, lambda i: (layer, 0, 0))
    kv = pl.BlockSpec((N_MEM, w), lambda i: (_batch_of(i * tm, segs), 0))
    row = pl.BlockSpec((tm, d), lambda i: (i, 0))
    return pl.pallas_call(
        _memattn_kernel,
        grid=(m // tm,),
        in_specs=[row, kv, kv, full((1, d)), wfull((d, w)), full((1, MEM_HEAD_DIM)), wfull((w, d)), full((1, d))],
        out_specs=[row, row],
        out_shape=[jax.ShapeDtypeStruct((m, d), F32), jax.ShapeDtypeStruct((m, d), BF)],
        compiler_params=_cp("parallel"),
        name="mem_attn",
    )(x, k, v, gm, wq, qn, wo, gf)


def _ffn_up_kernel(h_ref, wg_ref, wu_ref, o_ref):
    h = h_ref[...]
    gate = _dot(h, wg_ref[...])
    up = _dot(h, wu_ref[...])
    o_ref[...] = (gate * jax.nn.sigmoid(gate) * up).astype(o_ref.dtype)


def _ffn_up(h, wg, wu, layer, tm, tn):
    m, d = h.shape
    f = wg.shape[2]
    wspec = pl.BlockSpec((None, d, tn), lambda i, j: (layer, 0, j))
    return pl.pallas_call(
        _ffn_up_kernel,
        grid=(m // tm, f // tn),
        in_specs=[pl.BlockSpec((tm, d), lambda i, j: (i, 0)), wspec, wspec],
        out_specs=pl.BlockSpec((tm, tn), lambda i, j: (i, j)),
        out_shape=jax.ShapeDtypeStruct((m, f), BF),
        compiler_params=_cp("parallel", "parallel"),
        name="ffn_up",
    )(h, wg, wu)


def _ffn_down_kernel(a_ref, w_ref, x_ref, o_ref):
    o_ref[...] = x_ref[...] + _dot(a_ref[...], w_ref[...])


def _ffn_down(act, wd, layer, x, tm, tn, row0=0, rows=None):
    f = act.shape[1]
    n = wd.shape[2]
    rows = act.shape[0] if rows is None else rows
    assert row0 % tm == 0 and rows % tm == 0
    t0 = row0 // tm
    return pl.pallas_call(
        _ffn_down_kernel,
        grid=(rows // tm, n // tn),
        in_specs=[pl.BlockSpec((tm, f), lambda i, j: (t0 + i, 0)),
                  pl.BlockSpec((None, f, tn), lambda i, j: (layer, 0, j)),
                  pl.BlockSpec((tm, tn), lambda i, j: (t0 + i, j))],
        out_specs=pl.BlockSpec((tm, tn), lambda i, j: (i, j)),
        out_shape=jax.ShapeDtypeStruct((rows, n), F32),
        compiler_params=_cp("parallel", "parallel"),
        name="ffn_down",
    )(act, wd, x)


def _rel_bucket(rel):
    half = N_REL_BUCKETS // 2
    exact = half // 2
    n = np.abs(rel)
    large = exact + (np.log(np.maximum(n, 1) / exact) / np.log(REL_MAX_DIST / exact)
                     * (half - exact)).astype(np.int32)
    large = np.minimum(large, half - 1)
    return (rel > 0).astype(np.int32) * half + np.where(n < exact, n, large)


def _attn_bias(rel_bias):
    qi = np.arange(BLOCK)[:, None]
    kj = np.arange(3 * BLOCK)[None, :]
    rel = kj - BLOCK - qi
    onehot = (_rel_bucket(rel)[..., None] == np.arange(N_REL_BUCKETS)).astype(np.float32)
    bias = jnp.einsum("qkb,bh->hqk", jnp.asarray(onehot), rel_bias.astype(F32), precision=lax.Precision.HIGHEST)
    return jnp.where(jnp.asarray(np.abs(rel) <= BLOCK)[None], bias, NEG)


def _tri_consts(tt):
    t = np.arange(tt)[:, None]
    s = np.arange(tt)[None, :]
    same = (t // CHUNK) == (s // CHUNK)
    mats = [same & (s <= t), same & (s >= t)]
    return jnp.asarray(np.stack(mats).astype(np.float32), dtype=BF)


def _block_diag_ones(cb):
    i = np.arange(cb)
    return jnp.asarray(((i[:, None] // RWKV_HEAD) == (i[None, :] // RWKV_HEAD)).astype(np.float32), dtype=BF)


def _pick(n, pref):
    t = min(n, pref)
    assert n % t == 0, (n, t)
    return t


def kernel(x_prompt, x_sample, mem_prompt, mem_sample, rel_bias, norm_mix, w_in, q_norm, k_norm, sink,
           shift_prev, shift_next, w0, w2, a0, a2, g2, k_k, k_a, r_k, lnx_w, lnx_b, w_out,
           norm_mem, norm_memkv, wq_mem, wk_mem, wv_mem, wo_mem, qn_mem, kn_mem,
           norm_ffn, w_gate, w_up, w_down):
    b1, t1, d = x_prompt.shape
    b2, t2, _ = x_sample.shape
    segs = ((b1, t1), (b2, t2))
    m1 = b1 * t1
    m = m1 + b2 * t2
    xs = [x_prompt.reshape(m1, d), x_sample.reshape(b2 * t2, d)]
    mem = jnp.concatenate([mem_prompt.reshape(-1, d), mem_sample.reshape(-1, d)], axis=0)
    depth = w_in.shape[0]
    c = k_k.shape[1]
    lora0 = 3 * c

    mg = math.gcd(m1, m - m1)
    tm_in = _pick(mg, 1024)
    tn_in = 768
    tm_out, tn_out = _pick(mg, 1024), 1024
    tt = _pick(min(t1, t2), 256)
    cb = _pick(c, 1024)
    ts = _pick(min(t1, t2), 256)
    cs = _pick(c, 1024)
    tm_mem = _pick(min(t1, t2), 256)
    tm_up, tn_up = _pick(m, 2048), 256
    tm_down, tn_down = _pick(mg, 512), 512

    bias = _attn_bias(rel_bias)
    tri = _tri_consts(tt)
    bd = _block_diag_ones(_pick(c, 512))
    row = lambda a: a.reshape(1, -1).astype(F32)

    w_in_b = jnp.pad(w_in.astype(BF), ((0, 0), (0, 0), (0, XTRA_W - LORA_W - GATE_LORA)))
    w_out_b = w_out.astype(BF)
    wq_b, wk_b, wv_b, wo_b = (w.astype(BF) for w in (wq_mem, wk_mem, wv_mem, wo_mem))
    wg_b, wu_b, wd_b = (w.astype(BF) for w in (w_gate, w_up, w_down))

    for l in range(depth):
        sp, sn = shift_prev[l], shift_next[l]
        padx = lambda a: jnp.pad(a[lora0:], (0, XTRA_W - LORA_W - GATE_LORA)).reshape(1, -1)
        wl = jnp.zeros((4, LORA_W, c), F32)
        wl = wl.at[0, 0:DECAY_LORA].set(w2[l, 0]).at[1, DECAY_LORA:2 * DECAY_LORA].set(w2[l, 1])
        wl = wl.at[2, 2 * DECAY_LORA:2 * DECAY_LORA + AAA_LORA].set(a2[l, 0])
        wl = wl.at[3, 2 * DECAY_LORA + AAA_LORA:].set(a2[l, 1])
        prm = dict(sp_r=row(sp[0:c]), sn_r=row(sn[0:c]), sp_k=row(sp[c:2 * c]), sn_k=row(sn[c:2 * c]),
                   sp_v=row(sp[2 * c:3 * c]), sn_v=row(sn[2 * c:3 * c]), sp_x=padx(sp), sn_x=padx(sn),
                   w0=-LOG2E * w0[l], a0=-LOG2E * a0[l], wl=(-LOG2E * wl).astype(BF), g2=g2[l].astype(BF),
                   k_k=row(k_k[l]), k_a=row(k_a[l]), r_k=row(r_k[l]), bd=bd, tri=tri)

        proj = _inproj(xs, row(norm_mix[l]), w_in_b, l, tm_in, tn_in, m)
        att = _attention(proj, bias, row(q_norm[l]), row(k_norm[l]), sink[l].astype(F32), segs)
        po = _prep(proj, prm, segs, tt, cb)
        v_b, bv, g = po[10], po[11], po[12]
        y0 = _scan(0, po[0:4], v_b, po[4], segs, ts, cs)
        rw = _scan(1, po[5:9], v_b, po[9], segs, ts, cs,
                   finish=(y0, bv, g, row(lnx_w[l]), row(lnx_b[l]), bd))
        x = _outproj(att, rw, w_out_b, l, xs, tm_out, tn_out)

        km, vm = _memkv(mem, row(norm_memkv[l]), wk_b, wv_b, row(kn_mem[l]), l)
        x, h3 = _memattn(x, km, vm, row(norm_mem[l]), wq_b, row(qn_mem[l]), wo_b, row(norm_ffn[l]),
                         l, segs, tm_mem)

        act = _ffn_up(h3, wg_b, wu_b, l, tm_up, tn_up)
        if l + 1 < depth:
            xs = [_ffn_down(act, wd_b, l, x, tm_down, tn_down)]
    y1 = _ffn_down(act, wd_b, depth - 1, x, tm_down, tn_down, 0, m1)
    y2 = _ffn_down(act, wd_b, depth - 1, x, tm_down, tn_down, m1, m - m1)
    return (y1.reshape(b1, t1, d), y2.reshape(b2, t2, d))
```

```python
import functools
import math

import numpy as np
import jax
import jax.numpy as jnp
from jax import lax
from jax.experimental import pallas as pl
from jax.experimental.pallas import tpu as pltpu

F32 = jnp.float32
BF = jnp.bfloat16

HEAD_DIM = 128
N_Q_HEADS = 16
N_KV_HEADS = 4
GQA = N_Q_HEADS // N_KV_HEADS
BLOCK = 128
N_REL_BUCKETS = 32
REL_MAX_DIST = 128
RWKV_HEAD = 64
DECAY_LORA = 96
AAA_LORA = 96
GATE_LORA = 256
LNX_EPS = 64e-5
MEM_HEADS = 4
MEM_HEAD_DIM = 128
N_MEM = 256
RMS_EPS = 1e-6
NEG = -1e30
LOG2E = math.log2(math.e)

LANE = 128
CHUNK = 64
LORA_W = 2 * DECAY_LORA + 2 * AAA_LORA
XTRA_W = 768
VMEM_LIMIT = 56 * 1024 * 1024


def _cp(*sem):
    return pltpu.CompilerParams(dimension_semantics=sem, vmem_limit_bytes=VMEM_LIMIT)


def _dot(a, b):
    return jnp.dot(a, b, preferred_element_type=F32)


def _dot_nt(a, b):
    return lax.dot_general(a, b, (((1,), (1,)), ((), ())), preferred_element_type=F32)


def _dot_tn(a, b):
    return lax.dot_general(a, b, (((0,), (0,)), ((), ())), preferred_element_type=F32)


def _split2(x):
    hi = x.astype(BF)
    lo = (x - hi.astype(F32)).astype(BF)
    return hi, lo


def _rms_rows(x, g):
    ms = jnp.mean(x * x, axis=-1, keepdims=True)
    return x * lax.rsqrt(ms + RMS_EPS) * g


def _seq_info(row, segs):
    off = 0
    pos = None
    tlen = None
    for n, t in segs:
        p = lax.rem(row - off, t)
        if pos is None:
            pos, tlen = p, jnp.int32(t)
        else:
            inside = row >= off
            pos = jnp.where(inside, p, pos)
            tlen = jnp.where(inside, t, tlen)
        off += n * t
    return pos, tlen


def _batch_of(row, segs):
    off = 0
    boff = 0
    res = None
    for n, t in segs:
        b = boff + (row - off) // t
        res = b if res is None else jnp.where(row >= off, b, res)
        off += n * t
        boff += n
    return res


def _over_parts(call, parts, tm):
    out, t0 = None, 0
    for part in parts:
        assert part.shape[0] % tm == 0
        out = call(part, t0, out)
        t0 += part.shape[0] // tm
    return out


def _inproj_kernel(x_ref, g_ref, w_ref, *rest):
    o_ref, h_ref = rest[-2:]

    @pl.when(pl.program_id(1) == 0)
    def _():
        rows = x_ref.shape[0]
        step = min(rows, 256)
        for r0 in range(0, rows, step):
            h_ref[r0:r0 + step, :] = _rms_rows(x_ref[r0:r0 + step, :], g_ref[...]).astype(BF)

    o_ref[...] = _dot(h_ref[...], w_ref[...])


def _inproj(xs, g, w, layer, tm, tn, m):
    d = xs[0].shape[1]
    n = w.shape[2]

    def call(x, t0, prev):
        carried = [] if prev is None else [prev]
        return pl.pallas_call(
            _inproj_kernel,
            grid=(x.shape[0] // tm, n // tn),
            in_specs=[pl.BlockSpec((tm, d), lambda i, j: (i, 0), pipeline_mode=pl.Buffered(1)),
                      pl.BlockSpec((1, d), lambda i, j: (0, 0)),
                      pl.BlockSpec((None, d, tn), lambda i, j: (layer, 0, j))]
            + [pl.BlockSpec(memory_space=pl.ANY)] * len(carried),
            out_specs=pl.BlockSpec((tm, tn), lambda i, j: (t0 + i, j)),
            out_shape=jax.ShapeDtypeStruct((m, n), F32),
            input_output_aliases={3: 0} if carried else {},
            scratch_shapes=[pltpu.VMEM((tm, d), BF)],
            compiler_params=_cp("parallel", "arbitrary"),
            name="in_proj",
        )(x, g, w, *carried)

    return _over_parts(call, xs, tm)


def _attn_kernel(segs, q_ref, kp_ref, kc_ref, kn_ref, vp_ref, vc_ref, vn_ref,
                 bias_ref, qn_ref, kn_g_ref, sink_ref, o_ref):
    n = pl.program_id(0)
    pos, tlen = _seq_info(n * BLOCK, segs)
    first = pos == 0
    last = pos + BLOCK == tlen
    col = lax.broadcasted_iota(jnp.int32, (BLOCK, 3 * BLOCK), 1)
    dead = (first & (col < BLOCK)) | (last & (col >= 2 * BLOCK))
    scale = HEAD_DIM ** -0.5
    heads = range(N_Q_HEADS)
    hsl = [slice(h * HEAD_DIM, (h + 1) * HEAD_DIM) for h in heads]
    kw, vw = [], []
    for kh in range(N_KV_HEADS):
        ls = hsl[kh]
        kcat = jnp.concatenate([kp_ref[:, ls], kc_ref[:, ls], kn_ref[:, ls]], axis=0)
        kw.append(_rms_rows(kcat, kn_g_ref[...]).astype(BF))
        vw.append(jnp.concatenate([vp_ref[:, ls], vc_ref[:, ls], vn_ref[:, ls]], axis=0).astype(BF))
    q = [_rms_rows(q_ref[:, hsl[h]], qn_ref[...]).astype(BF) for h in heads]
    s = [_dot_nt(q[h], kw[h // GQA]) * scale + bias_ref[h] for h in heads]
    s = [jnp.where(dead, NEG, s[h]) for h in heads]
    mx = [jnp.maximum(jnp.max(s[h], axis=-1, keepdims=True), sink_ref[h]) for h in heads]
    p = [jnp.exp(s[h] - mx[h]) for h in heads]
    den = [jnp.sum(p[h], axis=-1, keepdims=True) + jnp.exp(sink_ref[h] - mx[h]) for h in heads]
    o = [_dot(p[h].astype(BF), vw[h // GQA]) for h in heads]
    for h in heads:
        o_ref[:, hsl[h]] = (o[h] / den[h]).astype(o_ref.dtype)


def _attention(proj, bias, qn, kn, sink, segs):
    m = proj.shape[0]
    nb = m // BLOCK
    aw = N_Q_HEADS * HEAD_DIM
    kvw = N_KV_HEADS * HEAD_DIM
    kcol = aw // kvw
    vcol = kcol + 1
    prev = lambda n: jnp.maximum(n - 1, 0)
    nxt = lambda n: jnp.minimum(n + 1, nb - 1)
    specs = [pl.BlockSpec((BLOCK, aw), lambda n: (n, 0))]
    for c in (kcol, vcol):
        specs += [pl.BlockSpec((BLOCK, kvw), lambda n, c=c: (prev(n), c)),
                  pl.BlockSpec((BLOCK, kvw), lambda n, c=c: (n, c)),
                  pl.BlockSpec((BLOCK, kvw), lambda n, c=c: (nxt(n), c))]
    specs += [pl.BlockSpec((N_Q_HEADS, BLOCK, 3 * BLOCK), lambda n: (0, 0, 0)),
              pl.BlockSpec((1, HEAD_DIM), lambda n: (0, 0)),
              pl.BlockSpec((1, HEAD_DIM), lambda n: (0, 0)),
              pl.BlockSpec(memory_space=pltpu.SMEM)]
    return pl.pallas_call(
        functools.partial(_attn_kernel, segs),
        grid=(nb,),
        in_specs=specs,
        out_specs=pl.BlockSpec((BLOCK, aw), lambda n: (n, 0)),
        out_shape=jax.ShapeDtypeStruct((m, aw), BF),
        compiler_params=_cp("parallel"),
        name="window_attn",
    )(proj, proj, proj, proj, proj, proj, proj, bias, qn, kn, sink)


def _prep_kernel(segs, tt, *refs):
    (pr, pr_p, pr_n, pk, pk_p, pk_n, pv, pv_p, pv_n, px, px_p, px_n,
     spr, snr, spk, snk, spv, snv, spx, snx, w0, a0, wl, g2, kk_g, ka_g, rk_g, bd, tri) = refs[:29]
    outs = refs[29:]
    zouts = (outs[0:5], outs[5:10])
    vo, bvo, go = outs[10:13]

    row0 = pl.program_id(0) * tt
    pos, tlen = _seq_info(row0, segs)
    first = pos == 0
    last = pos + tt == tlen
    rows = lax.broadcasted_iota(jnp.int32, (tt, 1), 0)

    def shifted(x_ref, p_ref, n_ref, sp, sn):
        x = x_ref[...]
        prow = jnp.where(first, 0.0, p_ref[7:8, :])
        nrow = jnp.where(last, 0.0, n_ref[0:1, :])
        prev = jnp.where(rows == 0, prow, pltpu.roll(x, 1, 0))
        nxt = jnp.where(rows == tt - 1, nrow, pltpu.roll(x, tt - 1, 0))
        return x + sp[...] * (prev - x) + sn[...] * (nxt - x)

    r = shifted(pr, pr_p, pr_n, spr, snr)
    k = shifted(pk, pk_p, pk_n, spk, snk)
    v = shifted(pv, pv_p, pv_n, spv, snv)
    x = shifted(px, px_p, px_n, spx, snx)

    xl = x[:, :LORA_W]
    lane = lax.broadcasted_iota(jnp.int32, xl.shape, 1)
    lx = jnp.where(lane < 2 * DECAY_LORA, jnp.tanh(xl), xl).astype(BF)
    sg = jax.nn.sigmoid(x[:, LORA_W:LORA_W + GATE_LORA]).astype(BF)
    go[...] = _dot(sg, g2[...]).astype(go.dtype)

    kkr = k * kk_g[...]
    def head_sums(t):
        wb = bd.shape[0]
        tb = t.astype(BF)
        return jnp.concatenate([_dot(tb[:, c0:c0 + wb], bd[...]) for c0 in range(0, t.shape[1], wb)], axis=1)

    n2 = head_sums(kkr * kkr)
    kk = kkr / jnp.maximum(jnp.sqrt(n2), 1e-12)

    kka = k * ka_g[...]
    kd_sum = None
    for z in (0, 1):
        at_o, rt_o, bt_o, kt_o, ct_o = zouts[z]
        sw = 1.0 / (1.0 + jnp.exp2(w0[z:z + 1, :] + _dot(lx, wl[z])))
        a = 1.0 / (1.0 + jnp.exp2(a0[z:z + 1, :] + _dot(lx, wl[2 + z])))
        lw = -(math.exp(-0.5) * LOG2E) * sw
        hi, lo = _split2(lw)
        cin = _dot(tri[z], hi) + _dot(tri[z], lo)
        kd = k + kka * (a - 1.0)
        b = kk * a
        e_neg = jnp.exp2(-cin)
        at_o[...] = (-kk * jnp.exp2(cin - lw)).astype(BF)
        rt_o[...] = (r * jnp.exp2(cin)).astype(BF)
        bt_o[...] = (b * e_neg).astype(BF)
        kt_o[...] = (kd * e_neg).astype(BF)
        end = CHUNK - 1 if z == 0 else 0
        ct_o[...] = jnp.concatenate(
            [jnp.broadcast_to(cin[c * CHUNK + end:c * CHUNK + end + 1], (8, cin.shape[1]))
             for c in range(tt // CHUNK)], axis=0)
        kd_sum = kd if kd_sum is None else kd_sum + kd

    bonus = head_sums(r * kd_sum * rk_g[...])
    bvo[...] = (bonus * v).astype(bvo.dtype)
    vo[...] = v.astype(BF)


def _prep(proj, p, segs, tt, cb):
    m = proj.shape[0]
    c = p["k_k"].shape[1]
    rcol0 = (N_Q_HEADS + 2 * N_KV_HEADS) * HEAD_DIM
    nrow8 = m // 8
    t8 = tt // 8

    def trio(width, colfn):
        return [pl.BlockSpec((tt, width), lambda i, j: (i, colfn(j))),
                pl.BlockSpec((8, width), lambda i, j: (jnp.maximum(i * t8 - 1, 0), colfn(j))),
                pl.BlockSpec((8, width), lambda i, j: (jnp.minimum((i + 1) * t8, nrow8 - 1), colfn(j)))]

    specs = []
    for sec in range(3):
        base = (rcol0 + sec * c) // cb
        specs += trio(cb, lambda j, base=base: base + j)
    xblk = (rcol0 + 3 * c) // XTRA_W
    specs += trio(XTRA_W, lambda j: xblk)
    vec = lambda: pl.BlockSpec((1, cb), lambda i, j: (0, j))
    specs += [vec(), vec(), vec(), vec(), vec(), vec(),
              pl.BlockSpec((1, XTRA_W), lambda i, j: (0, 0)),
              pl.BlockSpec((1, XTRA_W), lambda i, j: (0, 0)),
              pl.BlockSpec((2, cb), lambda i, j: (0, j)),
              pl.BlockSpec((2, cb), lambda i, j: (0, j)),
              pl.BlockSpec((4, LORA_W, cb), lambda i, j: (0, 0, j)),
              pl.BlockSpec((GATE_LORA, cb), lambda i, j: (0, j)),
              vec(), vec(), vec(),
              pl.BlockSpec(p["bd"].shape, lambda i, j: (0, 0)),
              pl.BlockSpec((2, tt, tt), lambda i, j: (0, 0, 0))]
    big = pl.BlockSpec((tt, cb), lambda i, j: (i, j))
    small = pl.BlockSpec((t8, cb), lambda i, j: (i, j))
    out_specs = ([big] * 4 + [small]) * 2 + [big] * 3
    big_s = jax.ShapeDtypeStruct((m, c), BF)
    small_s = jax.ShapeDtypeStruct((nrow8, c), F32)
    out_shape = ([big_s] * 4 + [small_s]) * 2 + [big_s] * 3
    args = [proj] * 12 + [p["sp_r"], p["sn_r"], p["sp_k"], p["sn_k"], p["sp_v"], p["sn_v"],
                          p["sp_x"], p["sn_x"], p["w0"], p["a0"], p["wl"], p["g2"],
                          p["k_k"], p["k_a"], p["r_k"], p["bd"], p["tri"]]
    return pl.pallas_call(
        functools.partial(_prep_kernel, segs, tt),
        grid=(m // tt, c // cb),
        in_specs=specs,
        out_specs=out_specs,
        out_shape=out_shape,
        compiler_params=_cp("parallel", "arbitrary"),
        name="rwkv_prep",
    )(*args)


def _scan_body(z, segs, ts, npair, at, rt, bt, kt, v, ct, y_ref, h_ref):
    i = pl.program_id(1)
    nblk = pl.num_programs(1)
    blk = i if z == 0 else nblk - 1 - i
    pos, tlen = _seq_info(blk * ts, segs)
    reset = (pos == 0) if z == 0 else (pos + ts == tlen)

    @pl.when(reset)
    def _():
        h_ref[...] = jnp.zeros_like(h_ref)

    n = 2 * CHUNK
    lane = lax.broadcasted_iota(jnp.int32, (CHUNK, LANE), 1)
    m_lo = jnp.where(lane < RWKV_HEAD, 1.0, 0.0).astype(BF)
    m_hi = jnp.where(lane < RWKV_HEAD, 0.0, 1.0).astype(BF)
    row = lax.broadcasted_iota(jnp.int32, (n, n), 0)
    col = lax.broadcasted_iota(jnp.int32, (n, n), 1)
    strict = (row > col) if z == 0 else (row < col)
    incl = (row >= col) if z == 0 else (row <= col)
    blk16 = (row // 16) == (col // 16)
    off32 = ((row // 32) == (col // 32)) & ((row // 16) != (col // 16))
    off64 = ((row // 64) == (col // 64)) & ((row // 32) != (col // 32))
    eye = jnp.where(row == col, 1.0, 0.0).astype(F32)

    nch = ts // CHUNK
    order = list(range(nch)) if z == 0 else list(range(nch - 1, -1, -1))
    pairs = list(range(npair))

    def tile(ref, p, c):
        return ref[c * CHUNK:(c + 1) * CHUNK, p * LANE:(p + 1) * LANE]

    def stack(x):
        return jnp.concatenate([x * m_lo, x * m_hi], axis=0)

    def each(fn, *lists):
        return [fn(*args) for args in zip(*lists)]

    def state_free(c, out):
        xa = [stack(tile(at, p, c)) for p in pairs]
        xr = [stack(tile(rt, p, c)) for p in pairs]
        xb = [stack(tile(bt, p, c)) for p in pairs]
        xk = [stack(tile(kt, p, c)) for p in pairs]
        vs = [stack(tile(v, p, c)) for p in pairs]
        ctr = [ct[c * 8:c * 8 + 1, p * LANE:(p + 1) * LANE] for p in pairs]
        grow = [jnp.broadcast_to(jnp.exp2(r), (n, LANE)).astype(BF) for r in ctr]
        xbg = each(lambda x, g: x * g, xb, grow)
        xkg = each(lambda x, g: x * g, xk, grow)
        gam = [jnp.exp2(jnp.transpose(jnp.broadcast_to(r, (n, n)))) for r in ctr]
        yield
        pm = each(lambda a, r, b, k: _dot_nt(jnp.concatenate([a, r], axis=0), jnp.concatenate([b, k], axis=0)),
                  xa, xr, xb, xk)
        a_ab = [jnp.where(strict, m[:n, :n], 0.0) for m in pm]
        a_ak = [jnp.where(strict, m[:n, n:], 0.0).astype(BF) for m in pm]
        a_rb = [jnp.where(incl, m[n:, :n], 0.0).astype(BF) for m in pm]
        a_rk = [jnp.where(incl, m[n:, n:], 0.0).astype(BF) for m in pm]
        yield
        ad = [jnp.where(blk16, a, 0.0) for a in a_ab]
        pk = [a.astype(BF) for a in ad]
        tinv = [eye + a for a in ad]
        pk = each(lambda a: _dot(a, a).astype(BF), pk)
        av = each(lambda a, w: _dot(a, w).astype(BF), a_ak, vs)
        yield
        for _ in range(2):
            both = each(lambda a, t: _dot(a, jnp.concatenate([a, t.astype(BF)], axis=1)), pk, tinv)
            pk = [m[:, :n].astype(BF) for m in both]
            tinv = each(lambda t, m: t + m[:, n:], tinv, both)
            yield
        tinv = each(lambda t, a: t + _dot(a, t.astype(BF)), tinv, pk)
        yield
        for off in (off32, off64):
            tb = [t.astype(BF) for t in tinv]
            ao = [jnp.where(off, a, 0.0).astype(BF) for a in a_ab]
            mid = each(lambda t, a: _dot(t, a).astype(BF), tb, ao)
            yield
            tinv = each(lambda t, m_, t_b: t + _dot(m_, t_b), tinv, mid, tb)
            yield
        wq = each(lambda t, a, q: _dot(t.astype(BF), jnp.concatenate([a, q], axis=1)).astype(BF), tinv, xa, av)
        yield
        zero = jnp.zeros((n, n), BF)
        wqv = each(lambda w, v_: jnp.concatenate([w, jnp.concatenate([zero, v_], axis=1)], axis=0), wq, vs)
        rb = each(lambda b, k, w: _dot(jnp.concatenate([b, k], axis=1), w), a_rb, a_rk, wqv)
        bgw = each(lambda b, k, w: _dot_tn(jnp.concatenate([b, k], axis=0), w), xbg, xkg, wqv)
        out["ry"] = each(lambda r, m_: (r.astype(F32) + m_[:, :n]).astype(BF), xr, rb)
        out["qy"] = [m_[:, n:] for m_ in rb]
        out["gm"] = [m_[:, :n].astype(BF) for m_ in bgw]
        out["jm"] = [m_[:, n:] for m_ in bgw]
        out["gam"] = gam

    hs = [h_ref[p] for p in pairs]

    def state_step(c, res):
        for p in pairs:
            hb = hs[p].astype(BF)
            yst = _dot(res["ry"][p], hb) + res["qy"][p]
            y_ref[c * CHUNK:(c + 1) * CHUNK, p * LANE:(p + 1) * LANE] = yst[:CHUNK] + yst[CHUNK:]
            hs[p] = res["gam"][p] * hs[p] + _dot(res["gm"][p], hb) + res["jm"][p]

    results = [dict() for _ in order]
    gens = [state_free(c, results[k]) for k, c in enumerate(order)]
    live = list(range(len(order)))
    while live:
        for k in list(live):
            try:
                next(gens[k])
            except StopIteration:
                live.remove(k)
    for k, c in enumerate(order):
        state_step(c, results[k])
    for p in pairs:
        h_ref[p] = hs[p]


def _scan_kernel(z, segs, ts, npair, at, rt, bt, kt, v, ct, y_ref, h_ref):
    _scan_body(z, segs, ts, npair, at, rt, bt, kt, v, ct, y_ref, h_ref)


def _scan_finish_kernel(z, segs, ts, npair, at, rt, bt, kt, v, ct, y0, bv, g, lw, lb, bd, o_ref, h_ref, y_ref):
    _scan_body(z, segs, ts, npair, at, rt, bt, kt, v, ct, y_ref, h_ref)
    inv_n = 1.0 / RWKV_HEAD
    wb = bd.shape[0]
    for c0 in range(0, y_ref.shape[1], wb):
        ls = slice(c0, c0 + wb)
        y = y0[:, ls] + y_ref[:, ls]
        hi, lo = _split2(y)
        mu = (_dot(hi, bd[...]) + _dot(lo, bd[...])) * inv_n
        d = y - mu
        var = _dot((d * d).astype(BF), bd[...]) * inv_n
        yn = d * lax.rsqrt(var + LNX_EPS) * lw[:, ls] + lb[:, ls]
        o_ref[:, ls] = ((yn + bv[:, ls].astype(F32)) * g[:, ls].astype(F32)).astype(o_ref.dtype)


def _scan(z, ops, v, ct, segs, ts, cs, finish=None):
    m, c = v.shape
    nblk = m // ts
    npair = cs // LANE
    rowmap = (lambda j, i: (i, j)) if z == 0 else (lambda j, i: (nblk - 1 - i, j))
    big = pl.BlockSpec((ts, cs), rowmap)
    small = pl.BlockSpec((ts // 8, cs), rowmap)
    state = pltpu.VMEM((npair, LANE, LANE), F32)
    name = "rwkv_scan_fwd" if z == 0 else "rwkv_scan_bwd"
    if finish is None:
        return pl.pallas_call(
            functools.partial(_scan_kernel, z, segs, ts, npair),
            grid=(c // cs, nblk),
            in_specs=[big] * 5 + [small],
            out_specs=big,
            out_shape=jax.ShapeDtypeStruct((m, c), F32),
            scratch_shapes=[state],
            compiler_params=_cp("parallel", "arbitrary"),
            name=name,
        )(*ops, v, ct)
    vec = pl.BlockSpec((1, cs), lambda j, i: (0, j))
    return pl.pallas_call(
        functools.partial(_scan_finish_kernel, z, segs, ts, npair),
        grid=(c // cs, nblk),
        in_specs=[big] * 5 + [small] + [big] * 3 + [vec, vec, pl.BlockSpec(finish[5].shape, lambda j, i: (0, 0))],
        out_specs=big,
        out_shape=jax.ShapeDtypeStruct((m, c), BF),
        scratch_shapes=[state, pltpu.VMEM((ts, cs), F32)],
        compiler_params=_cp("parallel", "arbitrary"),
        name=name + "_finish",
    )(*ops, v, ct, *finish)


def _outproj_kernel(ka, a_ref, b_ref, w_ref, x_ref, *rest):
    o_ref = rest[-1]
    acc = _dot(a_ref[...], w_ref[:ka, :]) + _dot(b_ref[...], w_ref[ka:, :])
    o_ref[...] = x_ref[...] + acc


def _outproj(att, rw, w, layer, xs, tm, tn):
    m, ka = att.shape
    kb = rw.shape[1]
    n = w.shape[2]

    def call(x, t0, prev):
        carried = [] if prev is None else [prev]
        return pl.pallas_call(
            functools.partial(_outproj_kernel, ka),
            grid=(x.shape[0] // tm, n // tn),
            in_specs=[pl.BlockSpec((tm, ka), lambda i, j: (t0 + i, 0)),
                      pl.BlockSpec((tm, kb), lambda i, j: (t0 + i, 0)),
                      pl.BlockSpec((None, ka + kb, tn), lambda i, j: (layer, 0, j)),
                      pl.BlockSpec((tm, tn), lambda i, j: (i, j))]
            + [pl.BlockSpec(memory_space=pl.ANY)] * len(carried),
            out_specs=pl.BlockSpec((tm, tn), lambda i, j: (t0 + i, j)),
            out_shape=jax.ShapeDtypeStruct((m, n), F32),
            input_output_aliases={4: 0} if carried else {},
            compiler_params=_cp("parallel", "parallel"),
            name="out_proj",
        )(att, rw, w, x, *carried)

    return _over_parts(call, xs, tm)


def _memkv_kernel(m_ref, g_ref, wk_ref, wv_ref, kn_ref, k_out, v_out):
    mm = _rms_rows(m_ref[...], g_ref[...]).astype(BF)
    k = _dot(mm, wk_ref[...])
    for h in range(MEM_HEADS):
        hs = slice(h * MEM_HEAD_DIM, (h + 1) * MEM_HEAD_DIM)
        k_out[:, hs] = _rms_rows(k[:, hs], kn_ref[...]).astype(BF)
    v_out[...] = _dot(mm, wv_ref[...]).astype(BF)


def _memkv(mem, g, wk, wv, kn, layer):
    rows, d = mem.shape
    w = wk.shape[2]
    full = lambda shape: pl.BlockSpec(shape, lambda i: (0, 0))
    wfull = lambda shape: pl.BlockSpec((None,) + shape, lambda i: (layer, 0, 0))
    return pl.pallas_call(
        _memkv_kernel,
        grid=(rows // N_MEM,),
        in_specs=[pl.BlockSpec((N_MEM, d), lambda i: (i, 0)), full((1, d)), wfull((d, w)), wfull((d, w)),
                  full((1, MEM_HEAD_DIM))],
        out_specs=[pl.BlockSpec((N_MEM, w), lambda i: (i, 0))] * 2,
        out_shape=[jax.ShapeDtypeStruct((rows, w), BF)] * 2,
        compiler_params=_cp("parallel"),
        name="mem_kv",
    )(mem, g, wk, wv, kn)


def _memattn_kernel(x_ref, k_ref, v_ref, gm_ref, wq_ref, qn_ref, wo_ref, gf_ref, x_out, h_out):
    x = x_ref[...]
    h = _rms_rows(x, gm_ref[...]).astype(BF)
    q = _dot(h, wq_ref[...])
    scale = MEM_HEAD_DIM ** -0.5
    heads = range(MEM_HEADS)
    hsl = [slice(hd * MEM_HEAD_DIM, (hd + 1) * MEM_HEAD_DIM) for hd in heads]
    qh = [_rms_rows(q[:, hsl[hd]], qn_ref[...]).astype(BF) for hd in heads]
    s = [_dot_nt(qh[hd], k_ref[:, hsl[hd]]) * scale for hd in heads]
    p = [jnp.exp(s[hd] - jnp.max(s[hd], axis=-1, keepdims=True)) for hd in heads]
    den = [jnp.sum(p[hd], axis=-1, keepdims=True) for hd in heads]
    pv = [_dot(p[hd].astype(BF), v_ref[:, hsl[hd]]) for hd in heads]
    o = jnp.concatenate([(pv[hd] / den[hd]).astype(BF) for hd in heads], axis=1)
    x2 = x + _dot(o, wo_ref[...])
    x_out[...] = x2
    h_out[...] = _rms_rows(x2, gf_ref[...]).astype(BF)


def _memattn(x, k, v, gm, wq, qn, wo, gf, layer, segs, tm):
    m, d = x.shape
    w = wq.shape[2]
    full = lambda shape: pl.BlockSpec(shape, lambda i: (0, 0))
    wfull = lambda shape: pl.BlockSpec((None,) + shape, lambda i: (layer, 0, 0))
    kv = pl.BlockSpec((N_MEM, w), lambda i: (_batch_of(i * tm, segs), 0))
    row = pl.BlockSpec((tm, d), lambda i: (i, 0))
    return pl.pallas_call(
        _memattn_kernel,
        grid=(m // tm,),
        in_specs=[row, kv, kv, full((1, d)), wfull((d, w)), full((1, MEM_HEAD_DIM)), wfull((w, d)), full((1, d))],
        out_specs=[row, row],
        out_shape=[jax.ShapeDtypeStruct((m, d), F32), jax.ShapeDtypeStruct((m, d), BF)],
        compiler_params=_cp("parallel"),
        name="mem_attn",
    )(x, k, v, gm, wq, qn, wo, gf)


def _ffn_up_kernel(h_ref, wg_ref, wu_ref, o_ref):
    h = h_ref[...]
    gate = _dot(h, wg_ref[...].astype(BF))
    up = _dot(h, wu_ref[...].astype(BF))
    o_ref[...] = (gate * jax.nn.sigmoid(gate) * up).astype(o_ref.dtype)


def _ffn_up(h, wg, wu, layer, tm, tn):
    m, d = h.shape
    f = wg.shape[2]
    wspec = pl.BlockSpec((None, d, tn), lambda i, j: (layer, 0, j))
    return pl.pallas_call(
        _ffn_up_kernel,
        grid=(m // tm, f // tn),
        in_specs=[pl.BlockSpec((tm, d), lambda i, j: (i, 0)), wspec, wspec],
        out_specs=pl.BlockSpec((tm, tn), lambda i, j: (i, j)),
        out_shape=jax.ShapeDtypeStruct((m, f), BF),
        compiler_params=_cp("parallel", "parallel"),
        name="ffn_up",
    )(h, wg, wu)


def _ffn_down_kernel(a_ref, w_ref, x_ref, o_ref):
    o_ref[...] = x_ref[...] + _dot(a_ref[...], w_ref[...])


def _ffn_down(act, wd, layer, x, tm, tn, row0=0, rows=None):
    f = act.shape[1]
    n = wd.shape[2]
    rows = act.shape[0] if rows is None else rows
    assert row0 % tm == 0 and rows % tm == 0
    t0 = row0 // tm
    return pl.pallas_call(
        _ffn_down_kernel,
        grid=(rows // tm, n // tn),
        in_specs=[pl.BlockSpec((tm, f), lambda i, j: (t0 + i, 0)),
                  pl.BlockSpec((None, f, tn), lambda i, j: (layer, 0, j)),
                  pl.BlockSpec((tm, tn), lambda i, j: (t0 + i, j))],
        out_specs=pl.BlockSpec((tm, tn), lambda i, j: (i, j)),
        out_shape=jax.ShapeDtypeStruct((rows, n), F32),
        compiler_params=_cp("parallel", "parallel"),
        name="ffn_down",
    )(act, wd, x)


def _rel_bucket(rel):
    half = N_REL_BUCKETS // 2
    exact = half // 2
    n = np.abs(rel)
    large = exact + (np.log(np.maximum(n, 1) / exact) / np.log(REL_MAX_DIST / exact)
                     * (half - exact)).astype(np.int32)
    large = np.minimum(large, half - 1)
    return (rel > 0).astype(np.int32) * half + np.where(n < exact, n, large)


def _attn_bias(rel_bias):
    qi = np.arange(BLOCK)[:, None]
    kj = np.arange(3 * BLOCK)[None, :]
    rel = kj - BLOCK - qi
    onehot = (_rel_bucket(rel)[..., None] == np.arange(N_REL_BUCKETS)).astype(np.float32)
    bias = jnp.einsum("qkb,bh->hqk", jnp.asarray(onehot), rel_bias.astype(F32), precision=lax.Precision.HIGHEST)
    return jnp.where(jnp.asarray(np.abs(rel) <= BLOCK)[None], bias, NEG)


def _tri_consts(tt):
    t = np.arange(tt)[:, None]
    s = np.arange(tt)[None, :]
    same = (t // CHUNK) == (s // CHUNK)
    mats = [same & (s <= t), same & (s >= t)]
    return jnp.asarray(np.stack(mats).astype(np.float32), dtype=BF)


def _block_diag_ones(cb):
    i = np.arange(cb)
    return jnp.asarray(((i[:, None] // RWKV_HEAD) == (i[None, :] // RWKV_HEAD)).astype(np.float32), dtype=BF)


def _pick(n, pref):
    t = min(n, pref)
    assert n % t == 0, (n, t)
    return t


def kernel(x_prompt, x_sample, mem_prompt, mem_sample, rel_bias, norm_mix, w_in, q_norm, k_norm, sink,
           shift_prev, shift_next, w0, w2, a0, a2, g2, k_k, k_a, r_k, lnx_w, lnx_b, w_out,
           norm_mem, norm_memkv, wq_mem, wk_mem, wv_mem, wo_mem, qn_mem, kn_mem,
           norm_ffn, w_gate, w_up, w_down):
    b1, t1, d = x_prompt.shape
    b2, t2, _ = x_sample.shape
    segs = ((b1, t1), (b2, t2))
    m1 = b1 * t1
    m = m1 + b2 * t2
    xs = [x_prompt.reshape(m1, d), x_sample.reshape(b2 * t2, d)]
    mem = jnp.concatenate([mem_prompt.reshape(-1, d), mem_sample.reshape(-1, d)], axis=0)
    depth = w_in.shape[0]
    c = k_k.shape[1]
    lora0 = 3 * c

    mg = math.gcd(m1, m - m1)
    tm_in = _pick(mg, 1024)
    tn_in = 768
    tm_out, tn_out = _pick(mg, 1024), 1024
    tt = _pick(min(t1, t2), 256)
    cb = _pick(c, 1024)
    ts = _pick(min(t1, t2), 256)
    cs = _pick(c, 1024)
    tm_mem = _pick(min(t1, t2), 256)
    tm_up, tn_up = _pick(m, 2048), 256
    tm_down, tn_down = _pick(mg, 512), 512

    bias = _attn_bias(rel_bias)
    tri = _tri_consts(tt)
    bd = _block_diag_ones(_pick(c, 512))
    row = lambda a: a.reshape(1, -1).astype(F32)

    w_in_b = jnp.pad(w_in.astype(BF), ((0, 0), (0, 0), (0, XTRA_W - LORA_W - GATE_LORA)))
    w_out_b = w_out.astype(BF)
    wq_b, wk_b, wv_b, wo_b = (w.astype(BF) for w in (wq_mem, wk_mem, wv_mem, wo_mem))
    wd_b = w_down.astype(BF)

    for l in range(depth):
        sp, sn = shift_prev[l], shift_next[l]
        padx = lambda a: jnp.pad(a[lora0:], (0, XTRA_W - LORA_W - GATE_LORA)).reshape(1, -1)
        wl = jnp.zeros((4, LORA_W, c), F32)
        wl = wl.at[0, 0:DECAY_LORA].set(w2[l, 0]).at[1, DECAY_LORA:2 * DECAY_LORA].set(w2[l, 1])
        wl = wl.at[2, 2 * DECAY_LORA:2 * DECAY_LORA + AAA_LORA].set(a2[l, 0])
        wl = wl.at[3, 2 * DECAY_LORA + AAA_LORA:].set(a2[l, 1])
        prm = dict(sp_r=row(sp[0:c]), sn_r=row(sn[0:c]), sp_k=row(sp[c:2 * c]), sn_k=row(sn[c:2 * c]),
                   sp_v=row(sp[2 * c:3 * c]), sn_v=row(sn[2 * c:3 * c]), sp_x=padx(sp), sn_x=padx(sn),
                   w0=-LOG2E * w0[l], a0=-LOG2E * a0[l], wl=(-LOG2E * wl).astype(BF), g2=g2[l].astype(BF),
                   k_k=row(k_k[l]), k_a=row(k_a[l]), r_k=row(r_k[l]), bd=bd, tri=tri)

        proj = _inproj(xs, row(norm_mix[l]), w_in_b, l, tm_in, tn_in, m)
        att = _attention(proj, bias, row(q_norm[l]), row(k_norm[l]), sink[l].astype(F32), segs)
        po = _prep(proj, prm, segs, tt, cb)
        v_b, bv, g = po[10], po[11], po[12]
        y0 = _scan(0, po[0:4], v_b, po[4], segs, ts, cs)
        rw = _scan(1, po[5:9], v_b, po[9], segs, ts, cs,
                   finish=(y0, bv, g, row(lnx_w[l]), row(lnx_b[l]), bd))
        x = _outproj(att, rw, w_out_b, l, xs, tm_out, tn_out)

        km, vm = _memkv(mem, row(norm_memkv[l]), wk_b, wv_b, row(kn_mem[l]), l)
        x, h3 = _memattn(x, km, vm, row(norm_mem[l]), wq_b, row(qn_mem[l]), wo_b, row(norm_ffn[l]),
                         l, segs, tm_mem)

        act = _ffn_up(h3, w_gate, w_up, l, tm_up, tn_up)
        if l + 1 < depth:
            xs = [_ffn_down(act, wd_b, l, x, tm_down, tn_down)]
    y1 = _ffn_down(act, wd_b, depth - 1, x, tm_down, tn_down, 0, m1)
    y2 = _ffn_down(act, wd_b, depth - 1, x, tm_down, tn_down, m1, m - m1)
    return (y1.reshape(b1, t1, d), y2.reshape(b2, t2, d))
```

```python
import functools
import math

import numpy as np
import jax
import jax.numpy as jnp
from jax import lax
from jax.experimental import pallas as pl
from jax.experimental.pallas import tpu as pltpu

F32 = jnp.float32
BF = jnp.bfloat16

HEAD_DIM = 128
N_Q_HEADS = 16
N_KV_HEADS = 4
GQA = N_Q_HEADS // N_KV_HEADS
BLOCK = 128
N_REL_BUCKETS = 32
REL_MAX_DIST = 128
RWKV_HEAD = 64
DECAY_LORA = 96
AAA_LORA = 96
GATE_LORA = 256
LNX_EPS = 64e-5
MEM_HEADS = 4
MEM_HEAD_DIM = 128
N_MEM = 256
RMS_EPS = 1e-6
NEG = -1e30
LOG2E = math.log2(math.e)

LANE = 128
CHUNK = 64
ATTN_GROUP = 8
LORA_W = 2 * DECAY_LORA + 2 * AAA_LORA
XTRA_W = 768
VMEM_LIMIT = 56 * 1024 * 1024


def _cp(*sem):
    return pltpu.CompilerParams(dimension_semantics=sem, vmem_limit_bytes=VMEM_LIMIT)


def _dot(a, b):
    return jnp.dot(a, b, preferred_element_type=F32)


def _dot_nt(a, b):
    return lax.dot_general(a, b, (((1,), (1,)), ((), ())), preferred_element_type=F32)


def _dot_tn(a, b):
    return lax.dot_general(a, b, (((0,), (0,)), ((), ())), preferred_element_type=F32)


def _split2(x):
    hi = x.astype(BF)
    lo = (x - hi.astype(F32)).astype(BF)
    return hi, lo


def _rms_rows(x, g):
    ms = jnp.mean(x * x, axis=-1, keepdims=True)
    return x * lax.rsqrt(ms + RMS_EPS) * g


def _seq_info(row, segs):
    off = 0
    pos = None
    tlen = None
    for n, t in segs:
        p = lax.rem(row - off, t)
        if pos is None:
            pos, tlen = p, jnp.int32(t)
        else:
            inside = row >= off
            pos = jnp.where(inside, p, pos)
            tlen = jnp.where(inside, t, tlen)
        off += n * t
    return pos, tlen


def _batch_of(row, segs):
    off = 0
    boff = 0
    res = None
    for n, t in segs:
        b = boff + (row - off) // t
        res = b if res is None else jnp.where(row >= off, b, res)
        off += n * t
        boff += n
    return res


def _over_parts(call, parts, tm):
    out, t0 = None, 0
    for part in parts:
        assert part.shape[0] % tm == 0
        out = call(part, t0, out)
        t0 += part.shape[0] // tm
    return out


def _inproj_kernel(x_ref, g_ref, w_ref, *rest):
    o_ref, h_ref = rest[-2:]

    @pl.when(pl.program_id(1) == 0)
    def _():
        rows = x_ref.shape[0]
        step = min(rows, 256)
        for r0 in range(0, rows, step):
            h_ref[r0:r0 + step, :] = _rms_rows(x_ref[r0:r0 + step, :], g_ref[...]).astype(BF)

    o_ref[...] = _dot(h_ref[...], w_ref[...])


def _inproj(xs, g, w, layer, tm, tn, m):
    d = xs[0].shape[1]
    n = w.shape[2]

    def call(x, t0, prev):
        carried = [] if prev is None else [prev]
        return pl.pallas_call(
            _inproj_kernel,
            grid=(x.shape[0] // tm, n // tn),
            in_specs=[pl.BlockSpec((tm, d), lambda i, j: (i, 0), pipeline_mode=pl.Buffered(1)),
                      pl.BlockSpec((1, d), lambda i, j: (0, 0)),
                      pl.BlockSpec((None, d, tn), lambda i, j: (layer, 0, j))]
            + [pl.BlockSpec(memory_space=pl.ANY)] * len(carried),
            out_specs=pl.BlockSpec((tm, tn), lambda i, j: (t0 + i, j)),
            out_shape=jax.ShapeDtypeStruct((m, n), F32),
            input_output_aliases={3: 0} if carried else {},
            scratch_shapes=[pltpu.VMEM((tm, d), BF)],
            compiler_params=_cp("parallel", "arbitrary"),
            name="in_proj",
        )(x, g, w, *carried)

    return _over_parts(call, xs, tm)


def _attn_kernel(segs, q_ref, kp_ref, kc_ref, kn_ref, vp_ref, vc_ref, vn_ref,
                 bias_ref, qn_ref, kn_g_ref, sink_ref, o_ref):
    n = pl.program_id(0)
    pos, tlen = _seq_info(n * BLOCK, segs)
    first = pos == 0
    last = pos + BLOCK == tlen
    col = lax.broadcasted_iota(jnp.int32, (BLOCK, 3 * BLOCK), 1)
    dead = (first & (col < BLOCK)) | (last & (col >= 2 * BLOCK))
    scale = HEAD_DIM ** -0.5
    heads = range(N_Q_HEADS)
    hsl = [slice(h * HEAD_DIM, (h + 1) * HEAD_DIM) for h in heads]
    kw, vw = [], []
    for kh in range(N_KV_HEADS):
        ls = hsl[kh]
        kcat = jnp.concatenate([kp_ref[:, ls], kc_ref[:, ls], kn_ref[:, ls]], axis=0)
        kw.append(_rms_rows(kcat, kn_g_ref[...]).astype(BF))
        vw.append(jnp.concatenate([vp_ref[:, ls], vc_ref[:, ls], vn_ref[:, ls]], axis=0).astype(BF))
    for g0 in range(0, N_Q_HEADS, ATTN_GROUP):
        hg = range(g0, g0 + ATTN_GROUP)
        q = {h: _rms_rows(q_ref[:, hsl[h]], qn_ref[...]).astype(BF) for h in hg}
        s = {h: _dot_nt(q[h], kw[h // GQA]) * scale + bias_ref[h] for h in hg}
        s = {h: jnp.where(dead, NEG, s[h]) for h in hg}
        mx = {h: jnp.maximum(jnp.max(s[h], axis=-1, keepdims=True), sink_ref[h]) for h in hg}
        p = {h: jnp.exp(s[h] - mx[h]) for h in hg}
        den = {h: jnp.sum(p[h], axis=-1, keepdims=True) + jnp.exp(sink_ref[h] - mx[h]) for h in hg}
        o = {h: _dot(p[h].astype(BF), vw[h // GQA]) for h in hg}
        for h in hg:
            o_ref[:, hsl[h]] = (o[h] / den[h]).astype(o_ref.dtype)


def _attention(proj, bias, qn, kn, sink, segs):
    m = proj.shape[0]
    nb = m // BLOCK
    aw = N_Q_HEADS * HEAD_DIM
    kvw = N_KV_HEADS * HEAD_DIM
    kcol = aw // kvw
    vcol = kcol + 1
    prev = lambda n: jnp.maximum(n - 1, 0)
    nxt = lambda n: jnp.minimum(n + 1, nb - 1)
    specs = [pl.BlockSpec((BLOCK, aw), lambda n: (n, 0))]
    for c in (kcol, vcol):
        specs += [pl.BlockSpec((BLOCK, kvw), lambda n, c=c: (prev(n), c)),
                  pl.BlockSpec((BLOCK, kvw), lambda n, c=c: (n, c)),
                  pl.BlockSpec((BLOCK, kvw), lambda n, c=c: (nxt(n), c))]
    specs += [pl.BlockSpec((N_Q_HEADS, BLOCK, 3 * BLOCK), lambda n: (0, 0, 0)),
              pl.BlockSpec((1, HEAD_DIM), lambda n: (0, 0)),
              pl.BlockSpec((1, HEAD_DIM), lambda n: (0, 0)),
              pl.BlockSpec(memory_space=pltpu.SMEM)]
    return pl.pallas_call(
        functools.partial(_attn_kernel, segs),
        grid=(nb,),
        in_specs=specs,
        out_specs=pl.BlockSpec((BLOCK, aw), lambda n: (n, 0)),
        out_shape=jax.ShapeDtypeStruct((m, aw), BF),
        compiler_params=_cp("parallel"),
        name="window_attn",
    )(proj, proj, proj, proj, proj, proj, proj, bias, qn, kn, sink)


def _prep_kernel(segs, tt, *refs):
    (pr, pr_p, pr_n, pk, pk_p, pk_n, pv, pv_p, pv_n, px, px_p, px_n,
     spr, snr, spk, snk, spv, snv, spx, snx, w0, a0, wl, g2, kk_g, ka_g, rk_g, bd, tri) = refs[:29]
    outs = refs[29:]
    zouts = (outs[0:5], outs[5:10])
    vo, bvo, go = outs[10:13]

    row0 = pl.program_id(0) * tt
    pos, tlen = _seq_info(row0, segs)
    first = pos == 0
    last = pos + tt == tlen
    rows = lax.broadcasted_iota(jnp.int32, (tt, 1), 0)

    def shifted(x_ref, p_ref, n_ref, sp, sn):
        x = x_ref[...]
        prow = jnp.where(first, 0.0, p_ref[7:8, :])
        nrow = jnp.where(last, 0.0, n_ref[0:1, :])
        prev = jnp.where(rows == 0, prow, pltpu.roll(x, 1, 0))
        nxt = jnp.where(rows == tt - 1, nrow, pltpu.roll(x, tt - 1, 0))
        return x + sp[...] * (prev - x) + sn[...] * (nxt - x)

    r = shifted(pr, pr_p, pr_n, spr, snr)
    k = shifted(pk, pk_p, pk_n, spk, snk)
    v = shifted(pv, pv_p, pv_n, spv, snv)
    x = shifted(px, px_p, px_n, spx, snx)

    xl = x[:, :LORA_W]
    lane = lax.broadcasted_iota(jnp.int32, xl.shape, 1)
    lx = jnp.where(lane < 2 * DECAY_LORA, jnp.tanh(xl), xl).astype(BF)
    sg = jax.nn.sigmoid(x[:, LORA_W:LORA_W + GATE_LORA]).astype(BF)
    go[...] = _dot(sg, g2[...]).astype(go.dtype)

    kkr = k * kk_g[...]
    def head_sums(t):
        wb = bd.shape[0]
        tb = t.astype(BF)
        return jnp.concatenate([_dot(tb[:, c0:c0 + wb], bd[...]) for c0 in range(0, t.shape[1], wb)], axis=1)

    n2 = head_sums(kkr * kkr)
    kk = kkr / jnp.maximum(jnp.sqrt(n2), 1e-12)

    kka = k * ka_g[...]
    kd_sum = None
    for z in (0, 1):
        at_o, rt_o, bt_o, kt_o, ct_o = zouts[z]
        sw = 1.0 / (1.0 + jnp.exp2(w0[z:z + 1, :] + _dot(lx, wl[z])))
        a = 1.0 / (1.0 + jnp.exp2(a0[z:z + 1, :] + _dot(lx, wl[2 + z])))
        lw = -(math.exp(-0.5) * LOG2E) * sw
        hi, lo = _split2(lw)
        cin = _dot(tri[z], hi) + _dot(tri[z], lo)
        kd = k + kka * (a - 1.0)
        b = kk * a
        e_neg = jnp.exp2(-cin)
        at_o[...] = (-kk * jnp.exp2(cin - lw)).astype(BF)
        rt_o[...] = (r * jnp.exp2(cin)).astype(BF)
        bt_o[...] = (b * e_neg).astype(BF)
        kt_o[...] = (kd * e_neg).astype(BF)
        end = CHUNK - 1 if z == 0 else 0
        ct_o[...] = jnp.concatenate(
            [jnp.broadcast_to(cin[c * CHUNK + end:c * CHUNK + end + 1], (8, cin.shape[1]))
             for c in range(tt // CHUNK)], axis=0)
        kd_sum = kd if kd_sum is None else kd_sum + kd

    bonus = head_sums(r * kd_sum * rk_g[...])
    bvo[...] = (bonus * v).astype(bvo.dtype)
    vo[...] = v.astype(BF)


def _prep(proj, p, segs, tt, cb):
    m = proj.shape[0]
    c = p["k_k"].shape[1]
    rcol0 = (N_Q_HEADS + 2 * N_KV_HEADS) * HEAD_DIM
    nrow8 = m // 8
    t8 = tt // 8

    def trio(width, colfn):
        return [pl.BlockSpec((tt, width), lambda i, j: (i, colfn(j))),
                pl.BlockSpec((8, width), lambda i, j: (jnp.maximum(i * t8 - 1, 0), colfn(j))),
                pl.BlockSpec((8, width), lambda i, j: (jnp.minimum((i + 1) * t8, nrow8 - 1), colfn(j)))]

    specs = []
    for sec in range(3):
        base = (rcol0 + sec * c) // cb
        specs += trio(cb, lambda j, base=base: base + j)
    xblk = (rcol0 + 3 * c) // XTRA_W
    specs += trio(XTRA_W, lambda j: xblk)
    vec = lambda: pl.BlockSpec((1, cb), lambda i, j: (0, j))
    specs += [vec(), vec(), vec(), vec(), vec(), vec(),
              pl.BlockSpec((1, XTRA_W), lambda i, j: (0, 0)),
              pl.BlockSpec((1, XTRA_W), lambda i, j: (0, 0)),
              pl.BlockSpec((2, cb), lambda i, j: (0, j)),
              pl.BlockSpec((2, cb), lambda i, j: (0, j)),
              pl.BlockSpec((4, LORA_W, cb), lambda i, j: (0, 0, j)),
              pl.BlockSpec((GATE_LORA, cb), lambda i, j: (0, j)),
              vec(), vec(), vec(),
              pl.BlockSpec(p["bd"].shape, lambda i, j: (0, 0)),
              pl.BlockSpec((2, tt, tt), lambda i, j: (0, 0, 0))]
    big = pl.BlockSpec((tt, cb), lambda i, j: (i, j))
    small = pl.BlockSpec((t8, cb), lambda i, j: (i, j))
    out_specs = ([big] * 4 + [small]) * 2 + [big] * 3
    big_s = jax.ShapeDtypeStruct((m, c), BF)
    small_s = jax.ShapeDtypeStruct((nrow8, c), F32)
    out_shape = ([big_s] * 4 + [small_s]) * 2 + [big_s] * 3
    args = [proj] * 12 + [p["sp_r"], p["sn_r"], p["sp_k"], p["sn_k"], p["sp_v"], p["sn_v"],
                          p["sp_x"], p["sn_x"], p["w0"], p["a0"], p["wl"], p["g2"],
                          p["k_k"], p["k_a"], p["r_k"], p["bd"], p["tri"]]
    return pl.pallas_call(
        functools.partial(_prep_kernel, segs, tt),
        grid=(m // tt, c // cb),
        in_specs=specs,
        out_specs=out_specs,
        out_shape=out_shape,
        compiler_params=_cp("parallel", "arbitrary"),
        name="rwkv_prep",
    )(*args)


def _scan_body(z, segs, ts, npair, at, rt, bt, kt, v, ct, y_ref, h_ref):
    i = pl.program_id(1)
    nblk = pl.num_programs(1)
    blk = i if z == 0 else nblk - 1 - i
    pos, tlen = _seq_info(blk * ts, segs)
    reset = (pos == 0) if z == 0 else (pos + ts == tlen)

    @pl.when(reset)
    def _():
        h_ref[...] = jnp.zeros_like(h_ref)

    n = 2 * CHUNK
    lane = lax.broadcasted_iota(jnp.int32, (CHUNK, LANE), 1)
    m_lo = jnp.where(lane < RWKV_HEAD, 1.0, 0.0).astype(BF)
    m_hi = jnp.where(lane < RWKV_HEAD, 0.0, 1.0).astype(BF)
    row = lax.broadcasted_iota(jnp.int32, (n, n), 0)
    col = lax.broadcasted_iota(jnp.int32, (n, n), 1)
    strict = (row > col) if z == 0 else (row < col)
    incl = (row >= col) if z == 0 else (row <= col)
    blk16 = (row // 16) == (col // 16)
    off32 = ((row // 32) == (col // 32)) & ((row // 16) != (col // 16))
    off64 = ((row // 64) == (col // 64)) & ((row // 32) != (col // 32))
    eye = jnp.where(row == col, 1.0, 0.0).astype(F32)

    nch = ts // CHUNK
    order = list(range(nch)) if z == 0 else list(range(nch - 1, -1, -1))
    pairs = list(range(npair))

    def tile(ref, p, c):
        return ref[c * CHUNK:(c + 1) * CHUNK, p * LANE:(p + 1) * LANE]

    def stack(x):
        return jnp.concatenate([x * m_lo, x * m_hi], axis=0)

    def each(fn, *lists):
        return [fn(*args) for args in zip(*lists)]

    def state_free(c, out):
        xa = [stack(tile(at, p, c)) for p in pairs]
        xr = [stack(tile(rt, p, c)) for p in pairs]
        xb = [stack(tile(bt, p, c)) for p in pairs]
        xk = [stack(tile(kt, p, c)) for p in pairs]
        vs = [stack(tile(v, p, c)) for p in pairs]
        ctr = [ct[c * 8:c * 8 + 1, p * LANE:(p + 1) * LANE] for p in pairs]
        grow = [jnp.broadcast_to(jnp.exp2(r), (n, LANE)).astype(BF) for r in ctr]
        xbg = each(lambda x, g: x * g, xb, grow)
        xkg = each(lambda x, g: x * g, xk, grow)
        gam = [jnp.exp2(jnp.transpose(jnp.broadcast_to(r, (n, n)))) for r in ctr]
        yield
        pm = each(lambda a, r, b, k: _dot_nt(jnp.concatenate([a, r], axis=0), jnp.concatenate([b, k], axis=0)),
                  xa, xr, xb, xk)
        a_ab = [jnp.where(strict, m[:n, :n], 0.0) for m in pm]
        a_ak = [jnp.where(strict, m[:n, n:], 0.0).astype(BF) for m in pm]
        a_rb = [jnp.where(incl, m[n:, :n], 0.0).astype(BF) for m in pm]
        a_rk = [jnp.where(incl, m[n:, n:], 0.0).astype(BF) for m in pm]
        yield
        ad = [jnp.where(blk16, a, 0.0) for a in a_ab]
        pk = [a.astype(BF) for a in ad]
        tinv = [eye + a for a in ad]
        pk = each(lambda a: _dot(a, a).astype(BF), pk)
        av = each(lambda a, w: _dot(a, w).astype(BF), a_ak, vs)
        yield
        for _ in range(2):
            both = each(lambda a, t: _dot(a, jnp.concatenate([a, t.astype(BF)], axis=1)), pk, tinv)
            pk = [m[:, :n].astype(BF) for m in both]
            tinv = each(lambda t, m: t + m[:, n:], tinv, both)
            yield
        tinv = each(lambda t, a: t + _dot(a, t.astype(BF)), tinv, pk)
        yield
        for off in (off32, off64):
            tb = [t.astype(BF) for t in tinv]
            ao = [jnp.where(off, a, 0.0).astype(BF) for a in a_ab]
            mid = each(lambda t, a: _dot(t, a).astype(BF), tb, ao)
            yield
            tinv = each(lambda t, m_, t_b: t + _dot(m_, t_b), tinv, mid, tb)
            yield
        wq = each(lambda t, a, q: _dot(t.astype(BF), jnp.concatenate([a, q], axis=1)).astype(BF), tinv, xa, av)
        yield
        zero = jnp.zeros((n, n), BF)
        wqv = each(lambda w, v_: jnp.concatenate([w, jnp.concatenate([zero, v_], axis=1)], axis=0), wq, vs)
        rb = each(lambda b, k, w: _dot(jnp.concatenate([b, k], axis=1), w), a_rb, a_rk, wqv)
        bgw = each(lambda b, k, w: _dot_tn(jnp.concatenate([b, k], axis=0), w), xbg, xkg, wqv)
        out["ry"] = each(lambda r, m_: (r.astype(F32) + m_[:, :n]).astype(BF), xr, rb)
        out["qy"] = [m_[:, n:] for m_ in rb]
        out["gm"] = [m_[:, :n].astype(BF) for m_ in bgw]
        out["jm"] = [m_[:, n:] for m_ in bgw]
        out["gam"] = gam

    hs = [h_ref[p] for p in pairs]

    def state_step(c, res):
        for p in pairs:
            hb = hs[p].astype(BF)
            yst = _dot(res["ry"][p], hb) + res["qy"][p]
            y_ref[c * CHUNK:(c + 1) * CHUNK, p * LANE:(p + 1) * LANE] = yst[:CHUNK] + yst[CHUNK:]
            hs[p] = res["gam"][p] * hs[p] + _dot(res["gm"][p], hb) + res["jm"][p]

    results = [dict() for _ in order]
    gens = [state_free(c, results[k]) for k, c in enumerate(order)]
    live = list(range(len(order)))
    while live:
        for k in list(live):
            try:
                next(gens[k])
            except StopIteration:
                live.remove(k)
    for k, c in enumerate(order):
        state_step(c, results[k])
    for p in pairs:
        h_ref[p] = hs[p]


def _scan_kernel(z, segs, ts, npair, at, rt, bt, kt, v, ct, y_ref, h_ref):
    _scan_body(z, segs, ts, npair, at, rt, bt, kt, v, ct, y_ref, h_ref)


def _scan_finish_kernel(z, segs, ts, npair, at, rt, bt, kt, v, ct, y0, bv, g, lw, lb, bd, o_ref, h_ref, y_ref):
    _scan_body(z, segs, ts, npair, at, rt, bt, kt, v, ct, y_ref, h_ref)
    inv_n = 1.0 / RWKV_HEAD
    wb = bd.shape[0]
    for c0 in range(0, y_ref.shape[1], wb):
        ls = slice(c0, c0 + wb)
        y = y0[:, ls] + y_ref[:, ls]
        mu = _dot(y.astype(BF), bd[...]) * inv_n
        d = y - mu
        var = _dot((d * d).astype(BF), bd[...]) * inv_n
        yn = d * lax.rsqrt(var + LNX_EPS) * lw[:, ls] + lb[:, ls]
        o_ref[:, ls] = ((yn + bv[:, ls].astype(F32)) * g[:, ls].astype(F32)).astype(o_ref.dtype)


def _scan(z, ops, v, ct, segs, ts, cs, finish=None):
    m, c = v.shape
    nblk = m // ts
    npair = cs // LANE
    rowmap = (lambda j, i: (i, j)) if z == 0 else (lambda j, i: (nblk - 1 - i, j))
    big = pl.BlockSpec((ts, cs), rowmap)
    small = pl.BlockSpec((ts // 8, cs), rowmap)
    state = pltpu.VMEM((npair, LANE, LANE), F32)
    name = "rwkv_scan_fwd" if z == 0 else "rwkv_scan_bwd"
    if finish is None:
        return pl.pallas_call(
            functools.partial(_scan_kernel, z, segs, ts, npair),
            grid=(c // cs, nblk),
            in_specs=[big] * 5 + [small],
            out_specs=big,
            out_shape=jax.ShapeDtypeStruct((m, c), F32),
            scratch_shapes=[state],
            compiler_params=_cp("parallel", "arbitrary"),
            name=name,
        )(*ops, v, ct)
    vec = pl.BlockSpec((1, cs), lambda j, i: (0, j))
    return pl.pallas_call(
        functools.partial(_scan_finish_kernel, z, segs, ts, npair),
        grid=(c // cs, nblk),
        in_specs=[big] * 5 + [small] + [big] * 3 + [vec, vec, pl.BlockSpec(finish[5].shape, lambda j, i: (0, 0))],
        out_specs=big,
        out_shape=jax.ShapeDtypeStruct((m, c), BF),
        scratch_shapes=[state, pltpu.VMEM((ts, cs), F32)],
        compiler_params=_cp("parallel", "arbitrary"),
        name=name + "_finish",
    )(*ops, v, ct, *finish)


def _outproj_kernel(ka, a_ref, b_ref, w_ref, x_ref, *rest):
    o_ref = rest[-1]
    acc = _dot(a_ref[...], w_ref[:ka, :]) + _dot(b_ref[...], w_ref[ka:, :])
    o_ref[...] = x_ref[...] + acc


def _outproj(att, rw, w, layer, xs, tm, tn):
    m, ka = att.shape
    kb = rw.shape[1]
    n = w.shape[2]

    def call(x, t0, prev):
        carried = [] if prev is None else [prev]
        return pl.pallas_call(
            functools.partial(_outproj_kernel, ka),
            grid=(x.shape[0] // tm, n // tn),
            in_specs=[pl.BlockSpec((tm, ka), lambda i, j: (t0 + i, 0)),
                      pl.BlockSpec((tm, kb), lambda i, j: (t0 + i, 0)),
                      pl.BlockSpec((None, ka + kb, tn), lambda i, j: (layer, 0, j)),
                      pl.BlockSpec((tm, tn), lambda i, j: (i, j))]
            + [pl.BlockSpec(memory_space=pl.ANY)] * len(carried),
            out_specs=pl.BlockSpec((tm, tn), lambda i, j: (t0 + i, j)),
            out_shape=jax.ShapeDtypeStruct((m, n), F32),
            input_output_aliases={4: 0} if carried else {},
            compiler_params=_cp("parallel", "parallel"),
            name="out_proj",
        )(att, rw, w, x, *carried)

    return _over_parts(call, xs, tm)


def _memkv_kernel(m_ref, g_ref, wk_ref, wv_ref, kn_ref, k_out, v_out):
    mm = _rms_rows(m_ref[...], g_ref[...]).astype(BF)
    k = _dot(mm, wk_ref[...])
    for h in range(MEM_HEADS):
        hs = slice(h * MEM_HEAD_DIM, (h + 1) * MEM_HEAD_DIM)
        k_out[:, hs] = _rms_rows(k[:, hs], kn_ref[...]).astype(BF)
    v_out[...] = _dot(mm, wv_ref[...]).astype(BF)


def _memkv(mem, g, wk, wv, kn, layer):
    rows, d = mem.shape
    w = wk.shape[2]
    full = lambda shape: pl.BlockSpec(shape, lambda i: (0, 0))
    wfull = lambda shape: pl.BlockSpec((None,) + shape, lambda i: (layer, 0, 0))
    return pl.pallas_call(
        _memkv_kernel,
        grid=(rows // N_MEM,),
        in_specs=[pl.BlockSpec((N_MEM, d), lambda i: (i, 0)), full((1, d)), wfull((d, w)), wfull((d, w)),
                  full((1, MEM_HEAD_DIM))],
        out_specs=[pl.BlockSpec((N_MEM, w), lambda i: (i, 0))] * 2,
        out_shape=[jax.ShapeDtypeStruct((rows, w), BF)] * 2,
        compiler_params=_cp("parallel"),
        name="mem_kv",
    )(mem, g, wk, wv, kn)


def _memattn_kernel(x_ref, k_ref, v_ref, gm_ref, wq_ref, qn_ref, wo_ref, gf_ref, x_out, h_out):
    x = x_ref[...]
    h = _rms_rows(x, gm_ref[...]).astype(BF)
    q = _dot(h, wq_ref[...])
    scale = MEM_HEAD_DIM ** -0.5
    heads = range(MEM_HEADS)
    hsl = [slice(hd * MEM_HEAD_DIM, (hd + 1) * MEM_HEAD_DIM) for hd in heads]
    qh = [_rms_rows(q[:, hsl[hd]], qn_ref[...]).astype(BF) for hd in heads]
    s = [_dot_nt(qh[hd], k_ref[:, hsl[hd]]) * scale for hd in heads]
    p = [jnp.exp(s[hd] - jnp.max(s[hd], axis=-1, keepdims=True)) for hd in heads]
    den = [jnp.sum(p[hd], axis=-1, keepdims=True) for hd in heads]
    pv = [_dot(p[hd].astype(BF), v_ref[:, hsl[hd]]) for hd in heads]
    o = jnp.concatenate([(pv[hd] / den[hd]).astype(BF) for hd in heads], axis=1)
    x2 = x + _dot(o, wo_ref[...])
    x_out[...] = x2
    h_out[...] = _rms_rows(x2, gf_ref[...]).astype(BF)


def _memattn(x, k, v, gm, wq, qn, wo, gf, layer, segs, tm):
    m, d = x.shape
    w = wq.shape[2]
    full = lambda shape: pl.BlockSpec(shape, lambda i: (0, 0))
    wfull = lambda shape: pl.BlockSpec((None,) + shape, lambda i: (layer, 0, 0))
    kv = pl.BlockSpec((N_MEM, w), lambda i: (_batch_of(i * tm, segs), 0))
    row = pl.BlockSpec((tm, d), lambda i: (i, 0))
    return pl.pallas_call(
        _memattn_kernel,
        grid=(m // tm,),
        in_specs=[row, kv, kv, full((1, d)), wfull((d, w)), full((1, MEM_HEAD_DIM)), wfull((w, d)), full((1, d))],
        out_specs=[row, row],
        out_shape=[jax.ShapeDtypeStruct((m, d), F32), jax.ShapeDtypeStruct((m, d), BF)],
        compiler_params=_cp("parallel"),
        name="mem_attn",
    )(x, k, v, gm, wq, qn, wo, gf)


def _ffn_up_kernel(h_ref, wg_ref, wu_ref, o_ref):
    h = h_ref[...]
    gate = _dot(h, wg_ref[...].astype(BF))
    up = _dot(h, wu_ref[...].astype(BF))
    o_ref[...] = (gate * jax.nn.sigmoid(gate) * up).astype(o_ref.dtype)


def _ffn_up(h, wg, wu, layer, tm, tn):
    m, d = h.shape
    f = wg.shape[2]
    wspec = pl.BlockSpec((None, d, tn), lambda i, j: (layer, 0, j))
    return pl.pallas_call(
        _ffn_up_kernel,
        grid=(m // tm, f // tn),
        in_specs=[pl.BlockSpec((tm, d), lambda i, j: (i, 0)), wspec, wspec],
        out_specs=pl.BlockSpec((tm, tn), lambda i, j: (i, j)),
        out_shape=jax.ShapeDtypeStruct((m, f), BF),
        compiler_params=_cp("parallel", "parallel"),
        name="ffn_up",
    )(h, wg, wu)


def _ffn_down_kernel(a_ref, w_ref, x_ref, o_ref):
    o_ref[...] = x_ref[...] + _dot(a_ref[...], w_ref[...])


def _ffn_down(act, wd, layer, x, tm, tn, row0=0, rows=None):
    f = act.shape[1]
    n = wd.shape[2]
    rows = act.shape[0] if rows is None else rows
    assert row0 % tm == 0 and rows % tm == 0
    t0 = row0 // tm
    return pl.pallas_call(
        _ffn_down_kernel,
        grid=(rows // tm, n // tn),
        in_specs=[pl.BlockSpec((tm, f), lambda i, j: (t0 + i, 0)),
                  pl.BlockSpec((None, f, tn), lambda i, j: (layer, 0, j)),
                  pl.BlockSpec((tm, tn), lambda i, j: (t0 + i, j))],
        out_specs=pl.BlockSpec((tm, tn), lambda i, j: (i, j)),
        out_shape=jax.ShapeDtypeStruct((rows, n), F32),
        compiler_params=_cp("parallel", "parallel"),
        name="ffn_down",
    )(act, wd, x)


def _rel_bucket(rel):
    half = N_REL_BUCKETS // 2
    exact = half // 2
    n = np.abs(rel)
    large = exact + (np.log(np.maximum(n, 1) / exact) / np.log(REL_MAX_DIST / exact)
                     * (half - exact)).astype(np.int32)
    large = np.minimum(large, half - 1)
    return (rel > 0).astype(np.int32) * half + np.where(n < exact, n, large)


def _attn_bias(rel_bias):
    qi = np.arange(BLOCK)[:, None]
    kj = np.arange(3 * BLOCK)[None, :]
    rel = kj - BLOCK - qi
    onehot = (_rel_bucket(rel)[..., None] == np.arange(N_REL_BUCKETS)).astype(np.float32)
    bias = jnp.einsum("qkb,bh->hqk", jnp.asarray(onehot), rel_bias.astype(F32), precision=lax.Precision.HIGHEST)
    return jnp.where(jnp.asarray(np.abs(rel) <= BLOCK)[None], bias, NEG)


def _tri_consts(tt):
    t = np.arange(tt)[:, None]
    s = np.arange(tt)[None, :]
    same = (t // CHUNK) == (s // CHUNK)
    mats = [same & (s <= t), same & (s >= t)]
    return jnp.asarray(np.stack(mats).astype(np.float32), dtype=BF)


def _block_diag_ones(cb):
    i = np.arange(cb)
    return jnp.asarray(((i[:, None] // RWKV_HEAD) == (i[None, :] // RWKV_HEAD)).astype(np.float32), dtype=BF)


def _pick(n, pref):
    t = min(n, pref)
    assert n % t == 0, (n, t)
    return t


def kernel(x_prompt, x_sample, mem_prompt, mem_sample, rel_bias, norm_mix, w_in, q_norm, k_norm, sink,
           shift_prev, shift_next, w0, w2, a0, a2, g2, k_k, k_a, r_k, lnx_w, lnx_b, w_out,
           norm_mem, norm_memkv, wq_mem, wk_mem, wv_mem, wo_mem, qn_mem, kn_mem,
           norm_ffn, w_gate, w_up, w_down):
    b1, t1, d = x_prompt.shape
    b2, t2, _ = x_sample.shape
    segs = ((b1, t1), (b2, t2))
    m1 = b1 * t1
    m = m1 + b2 * t2
    xs = [x_prompt.reshape(m1, d), x_sample.reshape(b2 * t2, d)]
    mem = jnp.concatenate([mem_prompt.reshape(-1, d), mem_sample.reshape(-1, d)], axis=0)
    depth = w_in.shape[0]
    c = k_k.shape[1]
    lora0 = 3 * c

    mg = math.gcd(m1, m - m1)
    tm_in = _pick(mg, 1024)
    tn_in = 768
    tm_out, tn_out = _pick(mg, 1024), 1024
    tt = _pick(min(t1, t2), 256)
    cb = _pick(c, 1024)
    ts = _pick(min(t1, t2), 256)
    cs = _pick(c, 1024)
    tm_mem = _pick(min(t1, t2), 256)
    tm_up, tn_up = _pick(m, 2048), 256
    tm_down, tn_down = _pick(mg, 512), 512

    bias = _attn_bias(rel_bias)
    tri = _tri_consts(tt)
    bd = _block_diag_ones(_pick(c, 512))
    row = lambda a: a.reshape(1, -1).astype(F32)

    w_in_b = jnp.pad(w_in.astype(BF), ((0, 0), (0, 0), (0, XTRA_W - LORA_W - GATE_LORA)))
    w_out_b = w_out.astype(BF)
    wq_b, wk_b, wv_b, wo_b = (w.astype(BF) for w in (wq_mem, wk_mem, wv_mem, wo_mem))
    wd_b = w_down.astype(BF)

    for l in range(depth):
        sp, sn = shift_prev[l], shift_next[l]
        padx = lambda a: jnp.pad(a[lora0:], (0, XTRA_W - LORA_W - GATE_LORA)).reshape(1, -1)
        wl = jnp.zeros((4, LORA_W, c), F32)
        wl = wl.at[0, 0:DECAY_LORA].set(w2[l, 0]).at[1, DECAY_LORA:2 * DECAY_LORA].set(w2[l, 1])
        wl = wl.at[2, 2 * DECAY_LORA:2 * DECAY_LORA + AAA_LORA].set(a2[l, 0])
        wl = wl.at[3, 2 * DECAY_LORA + AAA_LORA:].set(a2[l, 1])
        prm = dict(sp_r=row(sp[0:c]), sn_r=row(sn[0:c]), sp_k=row(sp[c:2 * c]), sn_k=row(sn[c:2 * c]),
                   sp_v=row(sp[2 * c:3 * c]), sn_v=row(sn[2 * c:3 * c]), sp_x=padx(sp), sn_x=padx(sn),
                   w0=-LOG2E * w0[l], a0=-LOG2E * a0[l], wl=(-LOG2E * wl).astype(BF), g2=g2[l].astype(BF),
                   k_k=row(k_k[l]), k_a=row(k_a[l]), r_k=row(r_k[l]), bd=bd, tri=tri)

        proj = _inproj(xs, row(norm_mix[l]), w_in_b, l, tm_in, tn_in, m)
        att = _attention(proj, bias, row(q_norm[l]), row(k_norm[l]), sink[l].astype(F32), segs)
        po = _prep(proj, prm, segs, tt, cb)
        v_b, bv, g = po[10], po[11], po[12]
        y0 = _scan(0, po[0:4], v_b, po[4], segs, ts, cs)
        rw = _scan(1, po[5:9], v_b, po[9], segs, ts, cs,
                   finish=(y0, bv, g, row(lnx_w[l]), row(lnx_b[l]), bd))
        x = _outproj(att, rw, w_out_b, l, xs, tm_out, tn_out)

        km, vm = _memkv(mem, row(norm_memkv[l]), wk_b, wv_b, row(kn_mem[l]), l)
        x, h3 = _memattn(x, km, vm, row(norm_mem[l]), wq_b, row(qn_mem[l]), wo_b, row(norm_ffn[l]),
                         l, segs, tm_mem)

        act = _ffn_up(h3, w_gate, w_up, l, tm_up, tn_up)
        if l + 1 < depth:
            xs = [_ffn_down(act, wd_b, l, x, tm_down, tn_down)]
    y1 = _ffn_down(act, wd_b, depth - 1, x, tm_down, tn_down, 0, m1)
    y2 = _ffn_down(act, wd_b, depth - 1, x, tm_down, tn_down, m1, m - m1)
    return (y1.reshape(b1, t1, d), y2.reshape(b2, t2, d))
```

```python
import functools
import math

import numpy as np
import jax
import jax.numpy as jnp
from jax import lax
from jax.experimental import pallas as pl
from jax.experimental.pallas import tpu as pltpu

F32 = jnp.float32
BF = jnp.bfloat16

HEAD_DIM = 128
N_Q_HEADS = 16
N_KV_HEADS = 4
GQA = N_Q_HEADS // N_KV_HEADS
BLOCK = 128
N_REL_BUCKETS = 32
REL_MAX_DIST = 128
RWKV_HEAD = 64
DECAY_LORA = 96
AAA_LORA = 96
GATE_LORA = 256
LNX_EPS = 64e-5
MEM_HEADS = 4
MEM_HEAD_DIM = 128
N_MEM = 256
RMS_EPS = 1e-6
NEG = -1e30
LOG2E = math.log2(math.e)

LANE = 128
CHUNK = 64
ATTN_GROUP = 8
LORA_W = 2 * DECAY_LORA + 2 * AAA_LORA
XTRA_W = 768
VMEM_LIMIT = 56 * 1024 * 1024


def _cp(*sem):
    return pltpu.CompilerParams(dimension_semantics=sem, vmem_limit_bytes=VMEM_LIMIT)


def _dot(a, b):
    return jnp.dot(a, b, preferred_element_type=F32)


def _dot_nt(a, b):
    return lax.dot_general(a, b, (((1,), (1,)), ((), ())), preferred_element_type=F32)


def _dot_tn(a, b):
    return lax.dot_general(a, b, (((0,), (0,)), ((), ())), preferred_element_type=F32)


def _split2(x):
    hi = x.astype(BF)
    lo = (x - hi.astype(F32)).astype(BF)
    return hi, lo


def _rms_rows(x, g):
    ms = jnp.mean(x * x, axis=-1, keepdims=True)
    return x * lax.rsqrt(ms + RMS_EPS) * g


def _seq_info(row, segs):
    off = 0
    pos = None
    tlen = None
    for n, t in segs:
        p = lax.rem(row - off, t)
        if pos is None:
            pos, tlen = p, jnp.int32(t)
        else:
            inside = row >= off
            pos = jnp.where(inside, p, pos)
            tlen = jnp.where(inside, t, tlen)
        off += n * t
    return pos, tlen


def _batch_of(row, segs):
    off = 0
    boff = 0
    res = None
    for n, t in segs:
        b = boff + (row - off) // t
        res = b if res is None else jnp.where(row >= off, b, res)
        off += n * t
        boff += n
    return res


def _tile_bounds(parts, tm):
    bounds, lo = [], 0
    for a in parts:
        assert a.shape[0] % tm == 0
        bounds.append((lo, lo + a.shape[0] // tm))
        lo += a.shape[0] // tm
    return tuple(bounds)


def _inproj_kernel(bounds, *refs):
    x_hbms = refs[:len(bounds)]
    g_ref, w_ref, o_ref, h_ref, x_buf, sem = refs[len(bounds):]
    i = pl.program_id(0)
    rows = x_buf.shape[0]

    def row_tile_copy(t, start):
        for (lo, hi), x_hbm in zip(bounds, x_hbms):
            @pl.when((t >= lo) & (t < hi))
            def _():
                cp = pltpu.make_async_copy(x_hbm.at[pl.ds((t - lo) * rows, rows), :], x_buf, sem)
                if start:
                    cp.start()
                else:
                    cp.wait()

    @pl.when(pl.program_id(1) == 0)
    def _():
        @pl.when(i == 0)
        def _():
            row_tile_copy(i, True)

        row_tile_copy(i, False)
        step = min(rows, 256)
        for r0 in range(0, rows, step):
            h_ref[r0:r0 + step, :] = _rms_rows(x_buf[r0:r0 + step, :], g_ref[...]).astype(BF)

        @pl.when(i + 1 < pl.num_programs(0))
        def _():
            row_tile_copy(i + 1, True)

    o_ref[...] = _dot(h_ref[...], w_ref[...])


def _inproj(xs, g, w, layer, tm, tn):
    d = xs[0].shape[1]
    n = w.shape[2]
    bounds = _tile_bounds(xs, tm)
    nt = bounds[-1][1]
    return pl.pallas_call(
        functools.partial(_inproj_kernel, bounds),
        grid=(nt, n // tn),
        in_specs=[pl.BlockSpec(memory_space=pl.ANY)] * len(xs)
        + [pl.BlockSpec((1, d), lambda i, j: (0, 0)),
           pl.BlockSpec((None, d, tn), lambda i, j: (layer, 0, j))],
        out_specs=pl.BlockSpec((tm, tn), lambda i, j: (i, j)),
        out_shape=jax.ShapeDtypeStruct((nt * tm, n), F32),
        scratch_shapes=[pltpu.VMEM((tm, d), BF), pltpu.VMEM((tm, d), F32), pltpu.SemaphoreType.DMA(())],
        compiler_params=_cp("arbitrary", "arbitrary"),
        name="in_proj",
    )(*xs, g, w)


def _attn_kernel(segs, q_ref, kp_ref, kc_ref, kn_ref, vp_ref, vc_ref, vn_ref,
                 bias_ref, qn_ref, kn_g_ref, sink_ref, o_ref):
    n = pl.program_id(0)
    pos, tlen = _seq_info(n * BLOCK, segs)
    first = pos == 0
    last = pos + BLOCK == tlen
    col = lax.broadcasted_iota(jnp.int32, (BLOCK, 3 * BLOCK), 1)
    dead = (first & (col < BLOCK)) | (last & (col >= 2 * BLOCK))
    scale = HEAD_DIM ** -0.5
    heads = range(N_Q_HEADS)
    hsl = [slice(h * HEAD_DIM, (h + 1) * HEAD_DIM) for h in heads]
    kw, vw = [], []
    for kh in range(N_KV_HEADS):
        ls = hsl[kh]
        kcat = jnp.concatenate([kp_ref[:, ls], kc_ref[:, ls], kn_ref[:, ls]], axis=0)
        kw.append(_rms_rows(kcat, kn_g_ref[...]).astype(BF))
        vw.append(jnp.concatenate([vp_ref[:, ls], vc_ref[:, ls], vn_ref[:, ls]], axis=0).astype(BF))
    for g0 in range(0, N_Q_HEADS, ATTN_GROUP):
        hg = range(g0, g0 + ATTN_GROUP)
        q = {h: _rms_rows(q_ref[:, hsl[h]], qn_ref[...]).astype(BF) for h in hg}
        s = {h: _dot_nt(q[h], kw[h // GQA]) * scale + bias_ref[h] for h in hg}
        s = {h: jnp.where(dead, NEG, s[h]) for h in hg}
        mx = {h: jnp.maximum(jnp.max(s[h], axis=-1, keepdims=True), sink_ref[h]) for h in hg}
        p = {h: jnp.exp(s[h] - mx[h]) for h in hg}
        den = {h: jnp.sum(p[h], axis=-1, keepdims=True) + jnp.exp(sink_ref[h] - mx[h]) for h in hg}
        o = {h: _dot(p[h].astype(BF), vw[h // GQA]) for h in hg}
        for h in hg:
            o_ref[:, hsl[h]] = (o[h] / den[h]).astype(o_ref.dtype)


def _attention(proj, bias, qn, kn, sink, segs):
    m = proj.shape[0]
    nb = m // BLOCK
    aw = N_Q_HEADS * HEAD_DIM
    kvw = N_KV_HEADS * HEAD_DIM
    kcol = aw // kvw
    vcol = kcol + 1
    prev = lambda n: jnp.maximum(n - 1, 0)
    nxt = lambda n: jnp.minimum(n + 1, nb - 1)
    specs = [pl.BlockSpec((BLOCK, aw), lambda n: (n, 0))]
    for c in (kcol, vcol):
        specs += [pl.BlockSpec((BLOCK, kvw), lambda n, c=c: (prev(n), c)),
                  pl.BlockSpec((BLOCK, kvw), lambda n, c=c: (n, c)),
                  pl.BlockSpec((BLOCK, kvw), lambda n, c=c: (nxt(n), c))]
    specs += [pl.BlockSpec((N_Q_HEADS, BLOCK, 3 * BLOCK), lambda n: (0, 0, 0)),
              pl.BlockSpec((1, HEAD_DIM), lambda n: (0, 0)),
              pl.BlockSpec((1, HEAD_DIM), lambda n: (0, 0)),
              pl.BlockSpec(memory_space=pltpu.SMEM)]
    return pl.pallas_call(
        functools.partial(_attn_kernel, segs),
        grid=(nb,),
        in_specs=specs,
        out_specs=pl.BlockSpec((BLOCK, aw), lambda n: (n, 0)),
        out_shape=jax.ShapeDtypeStruct((m, aw), BF),
        compiler_params=_cp("parallel"),
        name="window_attn",
    )(proj, proj, proj, proj, proj, proj, proj, bias, qn, kn, sink)


def _prep_kernel(segs, tt, *refs):
    (pr, pr_p, pr_n, pk, pk_p, pk_n, pv, pv_p, pv_n, px, px_p, px_n,
     spr, snr, spk, snk, spv, snv, spx, snx, w0, a0, wl, g2, kk_g, ka_g, rk_g, bd, tri) = refs[:29]
    outs = refs[29:]
    zouts = (outs[0:5], outs[5:10])
    vo, bvo, go = outs[10:13]

    row0 = pl.program_id(0) * tt
    pos, tlen = _seq_info(row0, segs)
    first = pos == 0
    last = pos + tt == tlen
    rows = lax.broadcasted_iota(jnp.int32, (tt, 1), 0)

    def shifted(x_ref, p_ref, n_ref, sp, sn):
        x = x_ref[...]
        prow = jnp.where(first, 0.0, p_ref[7:8, :])
        nrow = jnp.where(last, 0.0, n_ref[0:1, :])
        prev = jnp.where(rows == 0, prow, pltpu.roll(x, 1, 0))
        nxt = jnp.where(rows == tt - 1, nrow, pltpu.roll(x, tt - 1, 0))
        return x + sp[...] * (prev - x) + sn[...] * (nxt - x)

    r = shifted(pr, pr_p, pr_n, spr, snr)
    k = shifted(pk, pk_p, pk_n, spk, snk)
    v = shifted(pv, pv_p, pv_n, spv, snv)
    x = shifted(px, px_p, px_n, spx, snx)

    xl = x[:, :LORA_W]
    lane = lax.broadcasted_iota(jnp.int32, xl.shape, 1)
    lx = jnp.where(lane < 2 * DECAY_LORA, jnp.tanh(xl), xl).astype(BF)
    sg = jax.nn.sigmoid(x[:, LORA_W:LORA_W + GATE_LORA]).astype(BF)
    go[...] = _dot(sg, g2[...]).astype(go.dtype)

    kkr = k * kk_g[...]
    def head_sums(t):
        wb = bd.shape[0]
        tb = t.astype(BF)
        return jnp.concatenate([_dot(tb[:, c0:c0 + wb], bd[...]) for c0 in range(0, t.shape[1], wb)], axis=1)

    n2 = head_sums(kkr * kkr)
    kk = kkr / jnp.maximum(jnp.sqrt(n2), 1e-12)

    kka = k * ka_g[...]
    kd_sum = None
    for z in (0, 1):
        at_o, rt_o, bt_o, kt_o, ct_o = zouts[z]
        sw = 1.0 / (1.0 + jnp.exp2(w0[z:z + 1, :] + _dot(lx, wl[z])))
        a = 1.0 / (1.0 + jnp.exp2(a0[z:z + 1, :] + _dot(lx, wl[2 + z])))
        lw = -(math.exp(-0.5) * LOG2E) * sw
        hi, lo = _split2(lw)
        cin = _dot(tri[z], hi) + _dot(tri[z], lo)
        kd = k + kka * (a - 1.0)
        b = kk * a
        e_neg = jnp.exp2(-cin)
        at_o[...] = (-kk * jnp.exp2(cin - lw)).astype(BF)
        rt_o[...] = (r * jnp.exp2(cin)).astype(BF)
        bt_o[...] = (b * e_neg).astype(BF)
        kt_o[...] = (kd * e_neg).astype(BF)
        end = CHUNK - 1 if z == 0 else 0
        ct_o[...] = jnp.concatenate(
            [jnp.broadcast_to(cin[c * CHUNK + end:c * CHUNK + end + 1], (8, cin.shape[1]))
             for c in range(tt // CHUNK)], axis=0)
        kd_sum = kd if kd_sum is None else kd_sum + kd

    bonus = head_sums(r * kd_sum * rk_g[...])
    bvo[...] = (bonus * v).astype(bvo.dtype)
    vo[...] = v.astype(BF)


def _prep(proj, p, segs, tt, cb):
    m = proj.shape[0]
    c = p["k_k"].shape[1]
    rcol0 = (N_Q_HEADS + 2 * N_KV_HEADS) * HEAD_DIM
    nrow8 = m // 8
    t8 = tt // 8

    def trio(width, colfn):
        return [pl.BlockSpec((tt, width), lambda i, j: (i, colfn(j))),
                pl.BlockSpec((8, width), lambda i, j: (jnp.maximum(i * t8 - 1, 0), colfn(j))),
                pl.BlockSpec((8, width), lambda i, j: (jnp.minimum((i + 1) * t8, nrow8 - 1), colfn(j)))]

    specs = []
    for sec in range(3):
        base = (rcol0 + sec * c) // cb
        specs += trio(cb, lambda j, base=base: base + j)
    xblk = (rcol0 + 3 * c) // XTRA_W
    specs += trio(XTRA_W, lambda j: xblk)
    vec = lambda: pl.BlockSpec((1, cb), lambda i, j: (0, j))
    specs += [vec(), vec(), vec(), vec(), vec(), vec(),
              pl.BlockSpec((1, XTRA_W), lambda i, j: (0, 0)),
              pl.BlockSpec((1, XTRA_W), lambda i, j: (0, 0)),
              pl.BlockSpec((2, cb), lambda i, j: (0, j)),
              pl.BlockSpec((2, cb), lambda i, j: (0, j)),
              pl.BlockSpec((4, LORA_W, cb), lambda i, j: (0, 0, j)),
              pl.BlockSpec((GATE_LORA, cb), lambda i, j: (0, j)),
              vec(), vec(), vec(),
              pl.BlockSpec(p["bd"].shape, lambda i, j: (0, 0)),
              pl.BlockSpec((2, tt, tt), lambda i, j: (0, 0, 0))]
    big = pl.BlockSpec((tt, cb), lambda i, j: (i, j))
    small = pl.BlockSpec((t8, cb), lambda i, j: (i, j))
    out_specs = ([big] * 4 + [small]) * 2 + [big] * 3
    big_s = jax.ShapeDtypeStruct((m, c), BF)
    small_s = jax.ShapeDtypeStruct((nrow8, c), F32)
    out_shape = ([big_s] * 4 + [small_s]) * 2 + [big_s] * 3
    args = [proj] * 12 + [p["sp_r"], p["sn_r"], p["sp_k"], p["sn_k"], p["sp_v"], p["sn_v"],
                          p["sp_x"], p["sn_x"], p["w0"], p["a0"], p["wl"], p["g2"],
                          p["k_k"], p["k_a"], p["r_k"], p["bd"], p["tri"]]
    return pl.pallas_call(
        functools.partial(_prep_kernel, segs, tt),
        grid=(m // tt, c // cb),
        in_specs=specs,
        out_specs=out_specs,
        out_shape=out_shape,
        compiler_params=_cp("parallel", "arbitrary"),
        name="rwkv_prep",
    )(*args)


def _scan_body(z, segs, ts, npair, at, rt, bt, kt, v, ct, y_ref, h_ref):
    i = pl.program_id(1)
    nblk = pl.num_programs(1)
    blk = i if z == 0 else nblk - 1 - i
    pos, tlen = _seq_info(blk * ts, segs)
    reset = (pos == 0) if z == 0 else (pos + ts == tlen)

    @pl.when(reset)
    def _():
        h_ref[...] = jnp.zeros_like(h_ref)

    n = 2 * CHUNK
    lane = lax.broadcasted_iota(jnp.int32, (CHUNK, LANE), 1)
    m_lo = jnp.where(lane < RWKV_HEAD, 1.0, 0.0).astype(BF)
    m_hi = jnp.where(lane < RWKV_HEAD, 0.0, 1.0).astype(BF)
    row = lax.broadcasted_iota(jnp.int32, (n, n), 0)
    col = lax.broadcasted_iota(jnp.int32, (n, n), 1)
    strict = (row > col) if z == 0 else (row < col)
    incl = (row >= col) if z == 0 else (row <= col)
    blk16 = (row // 16) == (col // 16)
    off32 = ((row // 32) == (col // 32)) & ((row // 16) != (col // 16))
    off64 = ((row // 64) == (col // 64)) & ((row // 32) != (col // 32))
    eye = jnp.where(row == col, 1.0, 0.0).astype(F32)

    nch = ts // CHUNK
    order = list(range(nch)) if z == 0 else list(range(nch - 1, -1, -1))
    pairs = list(range(npair))

    def tile(ref, p, c):
        return ref[c * CHUNK:(c + 1) * CHUNK, p * LANE:(p + 1) * LANE]

    def stack(x):
        return jnp.concatenate([x * m_lo, x * m_hi], axis=0)

    def each(fn, *lists):
        return [fn(*args) for args in zip(*lists)]

    def state_free(c, out):
        xa = [stack(tile(at, p, c)) for p in pairs]
        xr = [stack(tile(rt, p, c)) for p in pairs]
        xb = [stack(tile(bt, p, c)) for p in pairs]
        xk = [stack(tile(kt, p, c)) for p in pairs]
        vs = [stack(tile(v, p, c)) for p in pairs]
        ctr = [ct[c * 8:c * 8 + 1, p * LANE:(p + 1) * LANE] for p in pairs]
        grow = [jnp.broadcast_to(jnp.exp2(r), (n, LANE)).astype(BF) for r in ctr]
        xbg = each(lambda x, g: x * g, xb, grow)
        xkg = each(lambda x, g: x * g, xk, grow)
        gam = [jnp.exp2(jnp.transpose(jnp.broadcast_to(r, (n, n)))) for r in ctr]
        yield
        pm = each(lambda a, r, b, k: _dot_nt(jnp.concatenate([a, r], axis=0), jnp.concatenate([b, k], axis=0)),
                  xa, xr, xb, xk)
        a_ab = [jnp.where(strict, m[:n, :n], 0.0) for m in pm]
        a_ak = [jnp.where(strict, m[:n, n:], 0.0).astype(BF) for m in pm]
        a_rb = [jnp.where(incl, m[n:, :n], 0.0).astype(BF) for m in pm]
        a_rk = [jnp.where(incl, m[n:, n:], 0.0).astype(BF) for m in pm]
        yield
        ad = [jnp.where(blk16, a, 0.0) for a in a_ab]
        pk = [a.astype(BF) for a in ad]
        tinv = [eye + a for a in ad]
        pk = each(lambda a: _dot(a, a).astype(BF), pk)
        av = each(lambda a, w: _dot(a, w).astype(BF), a_ak, vs)
        yield
        for _ in range(2):
            both = each(lambda a, t: _dot(a, jnp.concatenate([a, t.astype(BF)], axis=1)), pk, tinv)
            pk = [m[:, :n].astype(BF) for m in both]
            tinv = each(lambda t, m: t + m[:, n:], tinv, both)
            yield
        tinv = each(lambda t, a: t + _dot(a, t.astype(BF)), tinv, pk)
        yield
        for off in (off32, off64):
            tb = [t.astype(BF) for t in tinv]
            ao = [jnp.where(off, a, 0.0).astype(BF) for a in a_ab]
            mid = each(lambda t, a: _dot(t, a).astype(BF), tb, ao)
            yield
            tinv = each(lambda t, m_, t_b: t + _dot(m_, t_b), tinv, mid, tb)
            yield
        wq = each(lambda t, a, q: _dot(t.astype(BF), jnp.concatenate([a, q], axis=1)).astype(BF), tinv, xa, av)
        yield
        zero = jnp.zeros((n, n), BF)
        wqv = each(lambda w, v_: jnp.concatenate([w, jnp.concatenate([zero, v_], axis=1)], axis=0), wq, vs)
        rb = each(lambda b, k, w: _dot(jnp.concatenate([b, k], axis=1), w), a_rb, a_rk, wqv)
        bgw = each(lambda b, k, w: _dot_tn(jnp.concatenate([b, k], axis=0), w), xbg, xkg, wqv)
        out["ry"] = each(lambda r, m_: (r.astype(F32) + m_[:, :n]).astype(BF), xr, rb)
        out["qy"] = [m_[:, n:] for m_ in rb]
        out["gm"] = [m_[:, :n].astype(BF) for m_ in bgw]
        out["jm"] = [m_[:, n:] for m_ in bgw]
        out["gam"] = gam

    hs = [h_ref[p] for p in pairs]

    def state_step(c, res):
        for p in pairs:
            hb = hs[p].astype(BF)
            yst = _dot(res["ry"][p], hb) + res["qy"][p]
            y_ref[c * CHUNK:(c + 1) * CHUNK, p * LANE:(p + 1) * LANE] = yst[:CHUNK] + yst[CHUNK:]
            hs[p] = res["gam"][p] * hs[p] + _dot(res["gm"][p], hb) + res["jm"][p]

    results = [dict() for _ in order]
    gens = [state_free(c, results[k]) for k, c in enumerate(order)]
    live = list(range(len(order)))
    while live:
        for k in list(live):
            try:
                next(gens[k])
            except StopIteration:
                live.remove(k)
    for k, c in enumerate(order):
        state_step(c, results[k])
    for p in pairs:
        h_ref[p] = hs[p]


def _scan_kernel(z, segs, ts, npair, at, rt, bt, kt, v, ct, y_ref, h_ref):
    _scan_body(z, segs, ts, npair, at, rt, bt, kt, v, ct, y_ref, h_ref)


def _scan_finish_kernel(z, segs, ts, npair, at, rt, bt, kt, v, ct, y0, bv, g, lw, lb, bd, o_ref, h_ref, y_ref):
    _scan_body(z, segs, ts, npair, at, rt, bt, kt, v, ct, y_ref, h_ref)
    inv_n = 1.0 / RWKV_HEAD
    wb = bd.shape[0]
    for c0 in range(0, y_ref.shape[1], wb):
        ls = slice(c0, c0 + wb)
        y = y0[:, ls] + y_ref[:, ls]
        mu = _dot(y.astype(BF), bd[...]) * inv_n
        d = y - mu
        var = _dot((d * d).astype(BF), bd[...]) * inv_n
        yn = d * lax.rsqrt(var + LNX_EPS) * lw[:, ls] + lb[:, ls]
        o_ref[:, ls] = ((yn + bv[:, ls].astype(F32)) * g[:, ls].astype(F32)).astype(o_ref.dtype)


def _scan(z, ops, v, ct, segs, ts, cs, finish=None):
    m, c = v.shape
    nblk = m // ts
    npair = cs // LANE
    rowmap = (lambda j, i: (i, j)) if z == 0 else (lambda j, i: (nblk - 1 - i, j))
    big = pl.BlockSpec((ts, cs), rowmap)
    small = pl.BlockSpec((ts // 8, cs), rowmap)
    state = pltpu.VMEM((npair, LANE, LANE), F32)
    name = "rwkv_scan_fwd" if z == 0 else "rwkv_scan_bwd"
    if finish is None:
        return pl.pallas_call(
            functools.partial(_scan_kernel, z, segs, ts, npair),
            grid=(c // cs, nblk),
            in_specs=[big] * 5 + [small],
            out_specs=big,
            out_shape=jax.ShapeDtypeStruct((m, c), F32),
            scratch_shapes=[state],
            compiler_params=_cp("parallel", "arbitrary"),
            name=name,
        )(*ops, v, ct)
    vec = pl.BlockSpec((1, cs), lambda j, i: (0, j))
    return pl.pallas_call(
        functools.partial(_scan_finish_kernel, z, segs, ts, npair),
        grid=(c // cs, nblk),
        in_specs=[big] * 5 + [small] + [big] * 3 + [vec, vec, pl.BlockSpec(finish[5].shape, lambda j, i: (0, 0))],
        out_specs=big,
        out_shape=jax.ShapeDtypeStruct((m, c), BF),
        scratch_shapes=[state, pltpu.VMEM((ts, cs), F32)],
        compiler_params=_cp("parallel", "arbitrary"),
        name=name + "_finish",
    )(*ops, v, ct, *finish)


def _outproj_kernel(ka, a_ref, b_ref, w_ref, x_ref, o_ref):
    acc = _dot(a_ref[...], w_ref[:ka, :]) + _dot(b_ref[...], w_ref[ka:, :])
    o_ref[...] = x_ref[...] + acc


def _outproj(att, rw, w, layer, xs, tm, tn):
    ka = att.shape[1]
    kb = rw.shape[1]
    n = w.shape[2]
    outs = []
    for (t0, _), x in zip(_tile_bounds(xs, tm), xs):
        outs.append(pl.pallas_call(
            functools.partial(_outproj_kernel, ka),
            grid=(x.shape[0] // tm, n // tn),
            in_specs=[pl.BlockSpec((tm, ka), lambda i, j, t0=t0: (t0 + i, 0)),
                      pl.BlockSpec((tm, kb), lambda i, j, t0=t0: (t0 + i, 0)),
                      pl.BlockSpec((None, ka + kb, tn), lambda i, j: (layer, 0, j)),
                      pl.BlockSpec((tm, tn), lambda i, j: (i, j))],
            out_specs=pl.BlockSpec((tm, tn), lambda i, j: (i, j)),
            out_shape=jax.ShapeDtypeStruct((x.shape[0], n), F32),
            compiler_params=_cp("parallel", "parallel"),
            name="out_proj",
        )(att, rw, w, x))
    return outs


def _memkv_kernel(m_ref, g_ref, wk_ref, wv_ref, kn_ref, k_out, v_out):
    mm = _rms_rows(m_ref[...], g_ref[...]).astype(BF)
    k = _dot(mm, wk_ref[...])
    for h in range(MEM_HEADS):
        hs = slice(h * MEM_HEAD_DIM, (h + 1) * MEM_HEAD_DIM)
        k_out[:, hs] = _rms_rows(k[:, hs], kn_ref[...]).astype(BF)
    v_out[...] = _dot(mm, wv_ref[...]).astype(BF)


def _memkv(mem, g, wk, wv, kn, layer):
    rows, d = mem.shape
    w = wk.shape[2]
    full = lambda shape: pl.BlockSpec(shape, lambda i: (0, 0))
    wfull = lambda shape: pl.BlockSpec((None,) + shape, lambda i: (layer, 0, 0))
    return pl.pallas_call(
        _memkv_kernel,
        grid=(rows // N_MEM,),
        in_specs=[pl.BlockSpec((N_MEM, d), lambda i: (i, 0)), full((1, d)), wfull((d, w)), wfull((d, w)),
                  full((1, MEM_HEAD_DIM))],
        out_specs=[pl.BlockSpec((N_MEM, w), lambda i: (i, 0))] * 2,
        out_shape=[jax.ShapeDtypeStruct((rows, w), BF)] * 2,
        compiler_params=_cp("parallel"),
        name="mem_kv",
    )(mem, g, wk, wv, kn)


def _memattn_kernel(bounds, *refs):
    x_refs = refs[:len(bounds)]
    k_ref, v_ref, gm_ref, wq_ref, qn_ref, wo_ref, gf_ref, x_out, h_out = refs[len(bounds):]
    x = x_refs[0][...]
    for (lo, _), x_ref in zip(bounds[1:], x_refs[1:]):
        x = jnp.where(pl.program_id(0) >= lo, x_ref[...], x)
    h = _rms_rows(x, gm_ref[...]).astype(BF)
    q = _dot(h, wq_ref[...])
    scale = MEM_HEAD_DIM ** -0.5
    heads = range(MEM_HEADS)
    hsl = [slice(hd * MEM_HEAD_DIM, (hd + 1) * MEM_HEAD_DIM) for hd in heads]
    qh = [_rms_rows(q[:, hsl[hd]], qn_ref[...]).astype(BF) for hd in heads]
    s = [_dot_nt(qh[hd], k_ref[:, hsl[hd]]) * scale for hd in heads]
    p = [jnp.exp(s[hd] - jnp.max(s[hd], axis=-1, keepdims=True)) for hd in heads]
    den = [jnp.sum(p[hd], axis=-1, keepdims=True) for hd in heads]
    pv = [_dot(p[hd].astype(BF), v_ref[:, hsl[hd]]) for hd in heads]
    o = jnp.concatenate([(pv[hd] / den[hd]).astype(BF) for hd in heads], axis=1)
    x2 = x + _dot(o, wo_ref[...])
    x_out[...] = x2
    h_out[...] = _rms_rows(x2, gf_ref[...]).astype(BF)


def _memattn(xs, k, v, gm, wq, qn, wo, gf, layer, segs, tm):
    d = xs[0].shape[1]
    w = wq.shape[2]
    bounds = _tile_bounds(xs, tm)
    nt = bounds[-1][1]
    full = lambda shape: pl.BlockSpec(shape, lambda i: (0, 0))
    wfull = lambda shape: pl.BlockSpec((None,) + shape, lambda i: (layer, 0, 0))
    kv = pl.BlockSpec((N_MEM, w), lambda i: (_batch_of(i * tm, segs), 0))
    row = pl.BlockSpec((tm, d), lambda i: (i, 0))
    parts = [pl.BlockSpec((tm, d), lambda i, lo=lo, hi=hi: (jnp.clip(i - lo, 0, hi - lo - 1), 0)) for lo, hi in bounds]
    return pl.pallas_call(
        functools.partial(_memattn_kernel, bounds),
        grid=(nt,),
        in_specs=parts + [kv, kv, full((1, d)), wfull((d, w)), full((1, MEM_HEAD_DIM)), wfull((w, d)), full((1, d))],
        out_specs=[row, row],
        out_shape=[jax.ShapeDtypeStruct((nt * tm, d), F32), jax.ShapeDtypeStruct((nt * tm, d), BF)],
        compiler_params=_cp("parallel"),
        name="mem_attn",
    )(*xs, k, v, gm, wq, qn, wo, gf)


def _ffn_up_kernel(h_ref, wg_ref, wu_ref, o_ref):
    h = h_ref[...]
    gate = _dot(h, wg_ref[...].astype(BF))
    up = _dot(h, wu_ref[...].astype(BF))
    o_ref[...] = (gate * jax.nn.sigmoid(gate) * up).astype(o_ref.dtype)


def _ffn_up(h, wg, wu, layer, tm, tn):
    m, d = h.shape
    f = wg.shape[2]
    wspec = pl.BlockSpec((None, d, tn), lambda i, j: (layer, 0, j))
    return pl.pallas_call(
        _ffn_up_kernel,
        grid=(m // tm, f // tn),
        in_specs=[pl.BlockSpec((tm, d), lambda i, j: (i, 0)), wspec, wspec],
        out_specs=pl.BlockSpec((tm, tn), lambda i, j: (i, j)),
        out_shape=jax.ShapeDtypeStruct((m, f), BF),
        compiler_params=_cp("parallel", "parallel"),
        name="ffn_up",
    )(h, wg, wu)


def _ffn_down_kernel(a_ref, w_ref, x_ref, o_ref):
    o_ref[...] = x_ref[...] + _dot(a_ref[...], w_ref[...])


def _ffn_down(act, wd, layer, x, tm, tn, row0=0, rows=None):
    f = act.shape[1]
    n = wd.shape[2]
    rows = act.shape[0] if rows is None else rows
    assert row0 % tm == 0 and rows % tm == 0
    t0 = row0 // tm
    return pl.pallas_call(
        _ffn_down_kernel,
        grid=(rows // tm, n // tn),
        in_specs=[pl.BlockSpec((tm, f), lambda i, j: (t0 + i, 0)),
                  pl.BlockSpec((None, f, tn), lambda i, j: (layer, 0, j)),
                  pl.BlockSpec((tm, tn), lambda i, j: (t0 + i, j))],
        out_specs=pl.BlockSpec((tm, tn), lambda i, j: (i, j)),
        out_shape=jax.ShapeDtypeStruct((rows, n), F32),
        compiler_params=_cp("parallel", "parallel"),
        name="ffn_down",
    )(act, wd, x)


def _rel_bucket(rel):
    half = N_REL_BUCKETS // 2
    exact = half // 2
    n = np.abs(rel)
    large = exact + (np.log(np.maximum(n, 1) / exact) / np.log(REL_MAX_DIST / exact)
                     * (half - exact)).astype(np.int32)
    large = np.minimum(large, half - 1)
    return (rel > 0).astype(np.int32) * half + np.where(n < exact, n, large)


def _attn_bias(rel_bias):
    qi = np.arange(BLOCK)[:, None]
    kj = np.arange(3 * BLOCK)[None, :]
    rel = kj - BLOCK - qi
    onehot = (_rel_bucket(rel)[..., None] == np.arange(N_REL_BUCKETS)).astype(np.float32)
    bias = jnp.einsum("qkb,bh->hqk", jnp.asarray(onehot), rel_bias.astype(F32), precision=lax.Precision.HIGHEST)
    return jnp.where(jnp.asarray(np.abs(rel) <= BLOCK)[None], bias, NEG)


def _tri_consts(tt):
    t = np.arange(tt)[:, None]
    s = np.arange(tt)[None, :]
    same = (t // CHUNK) == (s // CHUNK)
    mats = [same & (s <= t), same & (s >= t)]
    return jnp.asarray(np.stack(mats).astype(np.float32), dtype=BF)


def _block_diag_ones(cb):
    i = np.arange(cb)
    return jnp.asarray(((i[:, None] // RWKV_HEAD) == (i[None, :] // RWKV_HEAD)).astype(np.float32), dtype=BF)


def _pick(n, pref):
    t = min(n, pref)
    assert n % t == 0, (n, t)
    return t


def kernel(x_prompt, x_sample, mem_prompt, mem_sample, rel_bias, norm_mix, w_in, q_norm, k_norm, sink,
           shift_prev, shift_next, w0, w2, a0, a2, g2, k_k, k_a, r_k, lnx_w, lnx_b, w_out,
           norm_mem, norm_memkv, wq_mem, wk_mem, wv_mem, wo_mem, qn_mem, kn_mem,
           norm_ffn, w_gate, w_up, w_down):
    b1, t1, d = x_prompt.shape
    b2, t2, _ = x_sample.shape
    segs = ((b1, t1), (b2, t2))
    m1 = b1 * t1
    m = m1 + b2 * t2
    xs = [x_prompt.reshape(m1, d), x_sample.reshape(b2 * t2, d)]
    mem = jnp.concatenate([mem_prompt.reshape(-1, d), mem_sample.reshape(-1, d)], axis=0)
    depth = w_in.shape[0]
    c = k_k.shape[1]
    lora0 = 3 * c

    mg = math.gcd(m1, m - m1)
    tm_in = _pick(mg, 1024)
    tn_in = 768
    tm_out, tn_out = _pick(mg, 1024), 1024
    tt = _pick(min(t1, t2), 256)
    cb = _pick(c, 1024)
    ts = _pick(min(t1, t2), 256)
    cs = _pick(c, 1024)
    tm_mem = _pick(min(t1, t2), 256)
    tm_up, tn_up = _pick(m, 2048), 256
    tm_down, tn_down = _pick(mg, 512), 512

    bias = _attn_bias(rel_bias)
    tri = _tri_consts(tt)
    bd = _block_diag_ones(_pick(c, 512))
    row = lambda a: a.reshape(1, -1).astype(F32)

    w_in_b = jnp.pad(w_in.astype(BF), ((0, 0), (0, 0), (0, XTRA_W - LORA_W - GATE_LORA)))
    w_out_b = w_out.astype(BF)
    wq_b, wk_b, wv_b, wo_b = (w.astype(BF) for w in (wq_mem, wk_mem, wv_mem, wo_mem))
    wd_b = w_down.astype(BF)

    for l in range(depth):
        sp, sn = shift_prev[l], shift_next[l]
        padx = lambda a: jnp.pad(a[lora0:], (0, XTRA_W - LORA_W - GATE_LORA)).reshape(1, -1)
        wl = jnp.zeros((4, LORA_W, c), F32)
        wl = wl.at[0, 0:DECAY_LORA].set(w2[l, 0]).at[1, DECAY_LORA:2 * DECAY_LORA].set(w2[l, 1])
        wl = wl.at[2, 2 * DECAY_LORA:2 * DECAY_LORA + AAA_LORA].set(a2[l, 0])
        wl = wl.at[3, 2 * DECAY_LORA + AAA_LORA:].set(a2[l, 1])
        prm = dict(sp_r=row(sp[0:c]), sn_r=row(sn[0:c]), sp_k=row(sp[c:2 * c]), sn_k=row(sn[c:2 * c]),
                   sp_v=row(sp[2 * c:3 * c]), sn_v=row(sn[2 * c:3 * c]), sp_x=padx(sp), sn_x=padx(sn),
                   w0=-LOG2E * w0[l], a0=-LOG2E * a0[l], wl=(-LOG2E * wl).astype(BF), g2=g2[l].astype(BF),
                   k_k=row(k_k[l]), k_a=row(k_a[l]), r_k=row(r_k[l]), bd=bd, tri=tri)

        proj = _inproj(xs, row(norm_mix[l]), w_in_b, l, tm_in, tn_in)
        att = _attention(proj, bias, row(q_norm[l]), row(k_norm[l]), sink[l].astype(F32), segs)
        po = _prep(proj, prm, segs, tt, cb)
        v_b, bv, g = po[10], po[11], po[12]
        y0 = _scan(0, po[0:4], v_b, po[4], segs, ts, cs)
        rw = _scan(1, po[5:9], v_b, po[9], segs, ts, cs,
                   finish=(y0, bv, g, row(lnx_w[l]), row(lnx_b[l]), bd))
        xs = _outproj(att, rw, w_out_b, l, xs, tm_out, tn_out)

        km, vm = _memkv(mem, row(norm_memkv[l]), wk_b, wv_b, row(kn_mem[l]), l)
        x, h3 = _memattn(xs, km, vm, row(norm_mem[l]), wq_b, row(qn_mem[l]), wo_b, row(norm_ffn[l]),
                         l, segs, tm_mem)

        act = _ffn_up(h3, w_gate, w_up, l, tm_up, tn_up)
        if l + 1 < depth:
            xs = [_ffn_down(act, wd_b, l, x, tm_down, tn_down)]
    y1 = _ffn_down(act, wd_b, depth - 1, x, tm_down, tn_down, 0, m1)
    y2 = _ffn_down(act, wd_b, depth - 1, x, tm_down, tn_down, m1, m - m1)
    return (y1.reshape(b1, t1, d), y2.reshape(b2, t2, d))
```

```python
import functools
import math

import numpy as np
import jax
import jax.numpy as jnp
from jax import lax
from jax.experimental import pallas as pl
from jax.experimental.pallas import tpu as pltpu

F32 = jnp.float32
BF = jnp.bfloat16

HEAD_DIM = 128
N_Q_HEADS = 16
N_KV_HEADS = 4
GQA = N_Q_HEADS // N_KV_HEADS
BLOCK = 128
N_REL_BUCKETS = 32
REL_MAX_DIST = 128
RWKV_HEAD = 64
DECAY_LORA = 96
AAA_LORA = 96
GATE_LORA = 256
LNX_EPS = 64e-5
MEM_HEADS = 4
MEM_HEAD_DIM = 128
N_MEM = 256
RMS_EPS = 1e-6
NEG = -1e30
LOG2E = math.log2(math.e)

LANE = 128
CHUNK = 64
ATTN_GROUP = 8
LORA_W = 2 * DECAY_LORA + 2 * AAA_LORA
XTRA_W = 768
VMEM_LIMIT = 56 * 1024 * 1024


def _cp(*sem):
    return pltpu.CompilerParams(dimension_semantics=sem, vmem_limit_bytes=VMEM_LIMIT)


def _dot(a, b):
    return jnp.dot(a, b, preferred_element_type=F32)


def _dot_nt(a, b):
    return lax.dot_general(a, b, (((1,), (1,)), ((), ())), preferred_element_type=F32)


def _dot_tn(a, b):
    return lax.dot_general(a, b, (((0,), (0,)), ((), ())), preferred_element_type=F32)


def _split2(x):
    hi = x.astype(BF)
    lo = (x - hi.astype(F32)).astype(BF)
    return hi, lo


def _rms_rows(x, g):
    ms = jnp.mean(x * x, axis=-1, keepdims=True)
    return x * lax.rsqrt(ms + RMS_EPS) * g


def _seq_info(row, segs):
    off = 0
    pos = None
    tlen = None
    for n, t in segs:
        p = lax.rem(row - off, t)
        if pos is None:
            pos, tlen = p, jnp.int32(t)
        else:
            inside = row >= off
            pos = jnp.where(inside, p, pos)
            tlen = jnp.where(inside, t, tlen)
        off += n * t
    return pos, tlen


def _batch_of(row, segs):
    off = 0
    boff = 0
    res = None
    for n, t in segs:
        b = boff + (row - off) // t
        res = b if res is None else jnp.where(row >= off, b, res)
        off += n * t
        boff += n
    return res


def _tile_bounds(parts, tm):
    bounds, lo = [], 0
    for a in parts:
        assert a.shape[0] % tm == 0
        bounds.append((lo, lo + a.shape[0] // tm))
        lo += a.shape[0] // tm
    return tuple(bounds)


def _inproj_kernel(bounds, *refs):
    x_hbms = refs[:len(bounds)]
    g_ref, w_ref, o_ref, h_ref, x_buf, sem = refs[len(bounds):]
    i = pl.program_id(0)
    rows = x_buf.shape[0]

    def row_tile_copy(t, start):
        for (lo, hi), x_hbm in zip(bounds, x_hbms):
            @pl.when((t >= lo) & (t < hi))
            def _():
                cp = pltpu.make_async_copy(x_hbm.at[pl.ds((t - lo) * rows, rows), :], x_buf, sem)
                if start:
                    cp.start()
                else:
                    cp.wait()

    @pl.when(pl.program_id(1) == 0)
    def _():
        @pl.when(i == 0)
        def _():
            row_tile_copy(i, True)

        row_tile_copy(i, False)
        step = min(rows, 256)
        for r0 in range(0, rows, step):
            h_ref[r0:r0 + step, :] = _rms_rows(x_buf[r0:r0 + step, :], g_ref[...]).astype(BF)

        @pl.when(i + 1 < pl.num_programs(0))
        def _():
            row_tile_copy(i + 1, True)

    o_ref[...] = _dot(h_ref[...], w_ref[...])


def _inproj(xs, g, w, layer, tm, tn):
    d = xs[0].shape[1]
    n = w.shape[2]
    bounds = _tile_bounds(xs, tm)
    nt = bounds[-1][1]
    return pl.pallas_call(
        functools.partial(_inproj_kernel, bounds),
        grid=(nt, n // tn),
        in_specs=[pl.BlockSpec(memory_space=pl.ANY)] * len(xs)
        + [pl.BlockSpec((1, d), lambda i, j: (0, 0)),
           pl.BlockSpec((None, d, tn), lambda i, j: (layer, 0, j))],
        out_specs=pl.BlockSpec((tm, tn), lambda i, j: (i, j)),
        out_shape=jax.ShapeDtypeStruct((nt * tm, n), F32),
        scratch_shapes=[pltpu.VMEM((tm, d), BF), pltpu.VMEM((tm, d), F32), pltpu.SemaphoreType.DMA(())],
        compiler_params=_cp("arbitrary", "arbitrary"),
        name="in_proj",
    )(*xs, g, w)


def _cast_pad_kernel(w_ref, o_ref):
    n = w_ref.shape[1]
    o_ref[:, :n] = w_ref[...].astype(BF)
    o_ref[:, n:] = jnp.zeros((o_ref.shape[0], o_ref.shape[1] - n), BF)


def _cast_pad(w, n_out, tr):
    nl, k, n = w.shape
    return pl.pallas_call(
        _cast_pad_kernel,
        grid=(nl, k // tr),
        in_specs=[pl.BlockSpec((None, tr, n), lambda l, i: (l, i, 0))],
        out_specs=pl.BlockSpec((None, tr, n_out), lambda l, i: (l, i, 0)),
        out_shape=jax.ShapeDtypeStruct((nl, k, n_out), BF),
        compiler_params=_cp("parallel", "parallel"),
        name="cast_pad",
    )(w)


def _attn_kernel(segs, q_ref, kp_ref, kc_ref, kn_ref, vp_ref, vc_ref, vn_ref,
                 bias_ref, qn_ref, kn_g_ref, sink_ref, o_ref):
    n = pl.program_id(0)
    pos, tlen = _seq_info(n * BLOCK, segs)
    first = pos == 0
    last = pos + BLOCK == tlen
    col = lax.broadcasted_iota(jnp.int32, (BLOCK, 3 * BLOCK), 1)
    dead = (first & (col < BLOCK)) | (last & (col >= 2 * BLOCK))
    scale = HEAD_DIM ** -0.5
    heads = range(N_Q_HEADS)
    hsl = [slice(h * HEAD_DIM, (h + 1) * HEAD_DIM) for h in heads]
    kw, vw = [], []
    for kh in range(N_KV_HEADS):
        ls = hsl[kh]
        kcat = jnp.concatenate([kp_ref[:, ls], kc_ref[:, ls], kn_ref[:, ls]], axis=0)
        kw.append(_rms_rows(kcat, kn_g_ref[...]).astype(BF))
        vw.append(jnp.concatenate([vp_ref[:, ls], vc_ref[:, ls], vn_ref[:, ls]], axis=0).astype(BF))
    for g0 in range(0, N_Q_HEADS, ATTN_GROUP):
        hg = range(g0, g0 + ATTN_GROUP)
        q = {h: _rms_rows(q_ref[:, hsl[h]], qn_ref[...]).astype(BF) for h in hg}
        s = {h: _dot_nt(q[h], kw[h // GQA]) * scale + bias_ref[h] for h in hg}
        s = {h: jnp.where(dead, NEG, s[h]) for h in hg}
        mx = {h: jnp.maximum(jnp.max(s[h], axis=-1, keepdims=True), sink_ref[h]) for h in hg}
        p = {h: jnp.exp(s[h] - mx[h]) for h in hg}
        den = {h: jnp.sum(p[h], axis=-1, keepdims=True) + jnp.exp(sink_ref[h] - mx[h]) for h in hg}
        o = {h: _dot(p[h].astype(BF), vw[h // GQA]) for h in hg}
        for h in hg:
            o_ref[:, hsl[h]] = (o[h] / den[h]).astype(o_ref.dtype)


def _attention(proj, bias, qn, kn, sink, segs):
    m = proj.shape[0]
    nb = m // BLOCK
    aw = N_Q_HEADS * HEAD_DIM
    kvw = N_KV_HEADS * HEAD_DIM
    kcol = aw // kvw
    vcol = kcol + 1
    prev = lambda n: jnp.maximum(n - 1, 0)
    nxt = lambda n: jnp.minimum(n + 1, nb - 1)
    specs = [pl.BlockSpec((BLOCK, aw), lambda n: (n, 0))]
    for c in (kcol, vcol):
        specs += [pl.BlockSpec((BLOCK, kvw), lambda n, c=c: (prev(n), c)),
                  pl.BlockSpec((BLOCK, kvw), lambda n, c=c: (n, c)),
                  pl.BlockSpec((BLOCK, kvw), lambda n, c=c: (nxt(n), c))]
    specs += [pl.BlockSpec((N_Q_HEADS, BLOCK, 3 * BLOCK), lambda n: (0, 0, 0)),
              pl.BlockSpec((1, HEAD_DIM), lambda n: (0, 0)),
              pl.BlockSpec((1, HEAD_DIM), lambda n: (0, 0)),
              pl.BlockSpec(memory_space=pltpu.SMEM)]
    return pl.pallas_call(
        functools.partial(_attn_kernel, segs),
        grid=(nb,),
        in_specs=specs,
        out_specs=pl.BlockSpec((BLOCK, aw), lambda n: (n, 0)),
        out_shape=jax.ShapeDtypeStruct((m, aw), BF),
        compiler_params=_cp("parallel"),
        name="window_attn",
    )(proj, proj, proj, proj, proj, proj, proj, bias, qn, kn, sink)


def _prep_kernel(segs, tt, *refs):
    (pr, pr_p, pr_n, pk, pk_p, pk_n, pv, pv_p, pv_n, px, px_p, px_n,
     spr, snr, spk, snk, spv, snv, spx, snx, w0, a0, wl, g2, kk_g, ka_g, rk_g, bd, tri) = refs[:29]
    outs = refs[29:]
    zouts = (outs[0:5], outs[5:10])
    vo, bvo, go = outs[10:13]

    row0 = pl.program_id(0) * tt
    pos, tlen = _seq_info(row0, segs)
    first = pos == 0
    last = pos + tt == tlen
    rows = lax.broadcasted_iota(jnp.int32, (tt, 1), 0)

    def shifted(x_ref, p_ref, n_ref, sp, sn):
        x = x_ref[...]
        prow = jnp.where(first, 0.0, p_ref[7:8, :])
        nrow = jnp.where(last, 0.0, n_ref[0:1, :])
        prev = jnp.where(rows == 0, prow, pltpu.roll(x, 1, 0))
        nxt = jnp.where(rows == tt - 1, nrow, pltpu.roll(x, tt - 1, 0))
        return x + sp[...] * (prev - x) + sn[...] * (nxt - x)

    r = shifted(pr, pr_p, pr_n, spr, snr)
    k = shifted(pk, pk_p, pk_n, spk, snk)
    v = shifted(pv, pv_p, pv_n, spv, snv)
    x = shifted(px, px_p, px_n, spx, snx)

    xl = x[:, :LORA_W]
    lane = lax.broadcasted_iota(jnp.int32, xl.shape, 1)
    lx = jnp.where(lane < 2 * DECAY_LORA, jnp.tanh(xl), xl).astype(BF)
    sg = jax.nn.sigmoid(x[:, LORA_W:LORA_W + GATE_LORA]).astype(BF)
    go[...] = _dot(sg, g2[...]).astype(go.dtype)

    kkr = k * kk_g[...]
    def head_sums(t):
        wb = bd.shape[0]
        tb = t.astype(BF)
        return jnp.concatenate([_dot(tb[:, c0:c0 + wb], bd[...]) for c0 in range(0, t.shape[1], wb)], axis=1)

    n2 = head_sums(kkr * kkr)
    kk = kkr / jnp.maximum(jnp.sqrt(n2), 1e-12)

    kka = k * ka_g[...]
    kd_sum = None
    for z in (0, 1):
        at_o, rt_o, bt_o, kt_o, ct_o = zouts[z]
        sw = 1.0 / (1.0 + jnp.exp2(w0[z:z + 1, :] + _dot(lx, wl[z])))
        a = 1.0 / (1.0 + jnp.exp2(a0[z:z + 1, :] + _dot(lx, wl[2 + z])))
        lw = -(math.exp(-0.5) * LOG2E) * sw
        hi, lo = _split2(lw)
        cin = _dot(tri[z], hi) + _dot(tri[z], lo)
        kd = k + kka * (a - 1.0)
        b = kk * a
        e_neg = jnp.exp2(-cin)
        at_o[...] = (-kk * jnp.exp2(cin - lw)).astype(BF)
        rt_o[...] = (r * jnp.exp2(cin)).astype(BF)
        bt_o[...] = (b * e_neg).astype(BF)
        kt_o[...] = (kd * e_neg).astype(BF)
        end = CHUNK - 1 if z == 0 else 0
        ct_o[...] = jnp.concatenate(
            [jnp.broadcast_to(cin[c * CHUNK + end:c * CHUNK + end + 1], (8, cin.shape[1]))
             for c in range(tt // CHUNK)], axis=0)
        kd_sum = kd if kd_sum is None else kd_sum + kd

    bonus = head_sums(r * kd_sum * rk_g[...])
    bvo[...] = (bonus * v).astype(bvo.dtype)
    vo[...] = v.astype(BF)


def _prep(proj, p, segs, tt, cb):
    m = proj.shape[0]
    c = p["k_k"].shape[1]
    rcol0 = (N_Q_HEADS + 2 * N_KV_HEADS) * HEAD_DIM
    nrow8 = m // 8
    t8 = tt // 8

    def trio(width, colfn):
        return [pl.BlockSpec((tt, width), lambda i, j: (i, colfn(j))),
                pl.BlockSpec((8, width), lambda i, j: (jnp.maximum(i * t8 - 1, 0), colfn(j))),
                pl.BlockSpec((8, width), lambda i, j: (jnp.minimum((i + 1) * t8, nrow8 - 1), colfn(j)))]

    specs = []
    for sec in range(3):
        base = (rcol0 + sec * c) // cb
        specs += trio(cb, lambda j, base=base: base + j)
    xblk = (rcol0 + 3 * c) // XTRA_W
    specs += trio(XTRA_W, lambda j: xblk)
    vec = lambda: pl.BlockSpec((1, cb), lambda i, j: (0, j))
    specs += [vec(), vec(), vec(), vec(), vec(), vec(),
              pl.BlockSpec((1, XTRA_W), lambda i, j: (0, 0)),
              pl.BlockSpec((1, XTRA_W), lambda i, j: (0, 0)),
              pl.BlockSpec((2, cb), lambda i, j: (0, j)),
              pl.BlockSpec((2, cb), lambda i, j: (0, j)),
              pl.BlockSpec((4, LORA_W, cb), lambda i, j: (0, 0, j)),
              pl.BlockSpec((GATE_LORA, cb), lambda i, j: (0, j)),
              vec(), vec(), vec(),
              pl.BlockSpec(p["bd"].shape, lambda i, j: (0, 0)),
              pl.BlockSpec((2, tt, tt), lambda i, j: (0, 0, 0))]
    big = pl.BlockSpec((tt, cb), lambda i, j: (i, j))
    small = pl.BlockSpec((t8, cb), lambda i, j: (i, j))
    out_specs = ([big] * 4 + [small]) * 2 + [big] * 3
    big_s = jax.ShapeDtypeStruct((m, c), BF)
    small_s = jax.ShapeDtypeStruct((nrow8, c), F32)
    out_shape = ([big_s] * 4 + [small_s]) * 2 + [big_s] * 3
    args = [proj] * 12 + [p["sp_r"], p["sn_r"], p["sp_k"], p["sn_k"], p["sp_v"], p["sn_v"],
                          p["sp_x"], p["sn_x"], p["w0"], p["a0"], p["wl"], p["g2"],
                          p["k_k"], p["k_a"], p["r_k"], p["bd"], p["tri"]]
    return pl.pallas_call(
        functools.partial(_prep_kernel, segs, tt),
        grid=(m // tt, c // cb),
        in_specs=specs,
        out_specs=out_specs,
        out_shape=out_shape,
        compiler_params=_cp("parallel", "arbitrary"),
        name="rwkv_prep",
    )(*args)


def _scan_body(z, segs, ts, npair, at, rt, bt, kt, v, ct, y_ref, h_ref):
    i = pl.program_id(1)
    nblk = pl.num_programs(1)
    blk = i if z == 0 else nblk - 1 - i
    pos, tlen = _seq_info(blk * ts, segs)
    reset = (pos == 0) if z == 0 else (pos + ts == tlen)

    @pl.when(reset)
    def _():
        h_ref[...] = jnp.zeros_like(h_ref)

    n = 2 * CHUNK
    lane = lax.broadcasted_iota(jnp.int32, (CHUNK, LANE), 1)
    m_lo = jnp.where(lane < RWKV_HEAD, 1.0, 0.0).astype(BF)
    m_hi = jnp.where(lane < RWKV_HEAD, 0.0, 1.0).astype(BF)
    row = lax.broadcasted_iota(jnp.int32, (n, n), 0)
    col = lax.broadcasted_iota(jnp.int32, (n, n), 1)
    strict = (row > col) if z == 0 else (row < col)
    incl = (row >= col) if z == 0 else (row <= col)
    blk16 = (row // 16) == (col // 16)
    off32 = ((row // 32) == (col // 32)) & ((row // 16) != (col // 16))
    off64 = ((row // 64) == (col // 64)) & ((row // 32) != (col // 32))
    eye = jnp.where(row == col, 1.0, 0.0).astype(F32)

    nch = ts // CHUNK
    order = list(range(nch)) if z == 0 else list(range(nch - 1, -1, -1))
    pairs = list(range(npair))

    def tile(ref, p, c):
        return ref[c * CHUNK:(c + 1) * CHUNK, p * LANE:(p + 1) * LANE]

    def stack(x):
        return jnp.concatenate([x * m_lo, x * m_hi], axis=0)

    def each(fn, *lists):
        return [fn(*args) for args in zip(*lists)]

    def state_free(c, out):
        xa = [stack(tile(at, p, c)) for p in pairs]
        xr = [stack(tile(rt, p, c)) for p in pairs]
        xb = [stack(tile(bt, p, c)) for p in pairs]
        xk = [stack(tile(kt, p, c)) for p in pairs]
        vs = [stack(tile(v, p, c)) for p in pairs]
        ctr = [ct[c * 8:c * 8 + 1, p * LANE:(p + 1) * LANE] for p in pairs]
        grow = [jnp.broadcast_to(jnp.exp2(r), (n, LANE)).astype(BF) for r in ctr]
        xbg = each(lambda x, g: x * g, xb, grow)
        xkg = each(lambda x, g: x * g, xk, grow)
        gam = [jnp.exp2(jnp.transpose(jnp.broadcast_to(r, (n, n)))) for r in ctr]
        yield
        pm = each(lambda a, r, b, k: _dot_nt(jnp.concatenate([a, r], axis=0), jnp.concatenate([b, k], axis=0)),
                  xa, xr, xb, xk)
        a_ab = [jnp.where(strict, m[:n, :n], 0.0) for m in pm]
        a_ak = [jnp.where(strict, m[:n, n:], 0.0).astype(BF) for m in pm]
        a_rb = [jnp.where(incl, m[n:, :n], 0.0).astype(BF) for m in pm]
        a_rk = [jnp.where(incl, m[n:, n:], 0.0).astype(BF) for m in pm]
        yield
        ad = [jnp.where(blk16, a, 0.0) for a in a_ab]
        pk = [a.astype(BF) for a in ad]
        tinv = [eye + a for a in ad]
        pk = each(lambda a: _dot(a, a).astype(BF), pk)
        av = each(lambda a, w: _dot(a, w).astype(BF), a_ak, vs)
        yield
        for _ in range(2):
            both = each(lambda a, t: _dot(a, jnp.concatenate([a, t.astype(BF)], axis=1)), pk, tinv)
            pk = [m[:, :n].astype(BF) for m in both]
            tinv = each(lambda t, m: t + m[:, n:], tinv, both)
            yield
        tinv = each(lambda t, a: t + _dot(a, t.astype(BF)), tinv, pk)
        yield
        for off in (off32, off64):
            tb = [t.astype(BF) for t in tinv]
            ao = [jnp.where(off, a, 0.0).astype(BF) for a in a_ab]
            mid = each(lambda t, a: _dot(t, a).astype(BF), tb, ao)
            yield
            tinv = each(lambda t, m_, t_b: t + _dot(m_, t_b), tinv, mid, tb)
            yield
        wq = each(lambda t, a, q: _dot(t.astype(BF), jnp.concatenate([a, q], axis=1)).astype(BF), tinv, xa, av)
        yield
        zero = jnp.zeros((n, n), BF)
        wqv = each(lambda w, v_: jnp.concatenate([w, jnp.concatenate([zero, v_], axis=1)], axis=0), wq, vs)
        rb = each(lambda b, k, w: _dot(jnp.concatenate([b, k], axis=1), w), a_rb, a_rk, wqv)
        bgw = each(lambda b, k, w: _dot_tn(jnp.concatenate([b, k], axis=0), w), xbg, xkg, wqv)
        out["ry"] = each(lambda r, m_: (r.astype(F32) + m_[:, :n]).astype(BF), xr, rb)
        out["qy"] = [m_[:, n:] for m_ in rb]
        out["gm"] = [m_[:, :n].astype(BF) for m_ in bgw]
        out["jm"] = [m_[:, n:] for m_ in bgw]
        out["gam"] = gam

    hs = [h_ref[p] for p in pairs]

    def state_step(c, res):
        for p in pairs:
            hb = hs[p].astype(BF)
            yst = _dot(res["ry"][p], hb) + res["qy"][p]
            y_ref[c * CHUNK:(c + 1) * CHUNK, p * LANE:(p + 1) * LANE] = yst[:CHUNK] + yst[CHUNK:]
            hs[p] = res["gam"][p] * hs[p] + _dot(res["gm"][p], hb) + res["jm"][p]

    results = [dict() for _ in order]
    gens = [state_free(c, results[k]) for k, c in enumerate(order)]
    live = list(range(len(order)))
    while live:
        for k in list(live):
            try:
                next(gens[k])
            except StopIteration:
                live.remove(k)
    for k, c in enumerate(order):
        state_step(c, results[k])
    for p in pairs:
        h_ref[p] = hs[p]


def _scan_kernel(z, segs, ts, npair, at, rt, bt, kt, v, ct, y_ref, h_ref):
    _scan_body(z, segs, ts, npair, at, rt, bt, kt, v, ct, y_ref, h_ref)


def _scan_finish_kernel(z, segs, ts, npair, at, rt, bt, kt, v, ct, y0, bv, g, lw, lb, bd, o_ref, h_ref, y_ref):
    _scan_body(z, segs, ts, npair, at, rt, bt, kt, v, ct, y_ref, h_ref)
    inv_n = 1.0 / RWKV_HEAD
    wb = bd.shape[0]
    for c0 in range(0, y_ref.shape[1], wb):
        ls = slice(c0, c0 + wb)
        y = y0[:, ls] + y_ref[:, ls]
        mu = _dot(y.astype(BF), bd[...]) * inv_n
        d = y - mu
        var = _dot((d * d).astype(BF), bd[...]) * inv_n
        yn = d * lax.rsqrt(var + LNX_EPS) * lw[:, ls] + lb[:, ls]
        o_ref[:, ls] = ((yn + bv[:, ls].astype(F32)) * g[:, ls].astype(F32)).astype(o_ref.dtype)


def _scan(z, ops, v, ct, segs, ts, cs, finish=None):
    m, c = v.shape
    nblk = m // ts
    npair = cs // LANE
    rowmap = (lambda j, i: (i, j)) if z == 0 else (lambda j, i: (nblk - 1 - i, j))
    big = pl.BlockSpec((ts, cs), rowmap)
    small = pl.BlockSpec((ts // 8, cs), rowmap)
    state = pltpu.VMEM((npair, LANE, LANE), F32)
    name = "rwkv_scan_fwd" if z == 0 else "rwkv_scan_bwd"
    if finish is None:
        return pl.pallas_call(
            functools.partial(_scan_kernel, z, segs, ts, npair),
            grid=(c // cs, nblk),
            in_specs=[big] * 5 + [small],
            out_specs=big,
            out_shape=jax.ShapeDtypeStruct((m, c), F32),
            scratch_shapes=[state],
            compiler_params=_cp("parallel", "arbitrary"),
            name=name,
        )(*ops, v, ct)
    vec = pl.BlockSpec((1, cs), lambda j, i: (0, j))
    return pl.pallas_call(
        functools.partial(_scan_finish_kernel, z, segs, ts, npair),
        grid=(c // cs, nblk),
        in_specs=[big] * 5 + [small] + [big] * 3 + [vec, vec, pl.BlockSpec(finish[5].shape, lambda j, i: (0, 0))],
        out_specs=big,
        out_shape=jax.ShapeDtypeStruct((m, c), BF),
        scratch_shapes=[state, pltpu.VMEM((ts, cs), F32)],
        compiler_params=_cp("parallel", "arbitrary"),
        name=name + "_finish",
    )(*ops, v, ct, *finish)


def _outproj_kernel(ka, a_ref, b_ref, w_ref, x_ref, o_ref):
    acc = _dot(a_ref[...], w_ref[:ka, :]) + _dot(b_ref[...], w_ref[ka:, :])
    o_ref[...] = x_ref[...] + acc


def _outproj(att, rw, w, layer, xs, tm, tn):
    ka = att.shape[1]
    kb = rw.shape[1]
    n = w.shape[2]
    outs = []
    for (t0, _), x in zip(_tile_bounds(xs, tm), xs):
        outs.append(pl.pallas_call(
            functools.partial(_outproj_kernel, ka),
            grid=(x.shape[0] // tm, n // tn),
            in_specs=[pl.BlockSpec((tm, ka), lambda i, j, t0=t0: (t0 + i, 0)),
                      pl.BlockSpec((tm, kb), lambda i, j, t0=t0: (t0 + i, 0)),
                      pl.BlockSpec((None, ka + kb, tn), lambda i, j: (layer, 0, j)),
                      pl.BlockSpec((tm, tn), lambda i, j: (i, j))],
            out_specs=pl.BlockSpec((tm, tn), lambda i, j: (i, j)),
            out_shape=jax.ShapeDtypeStruct((x.shape[0], n), F32),
            compiler_params=_cp("parallel", "parallel"),
            name="out_proj",
        )(att, rw, w, x))
    return outs


def _memkv_kernel(m_ref, g_ref, wk_ref, wv_ref, kn_ref, k_out, v_out):
    mm = _rms_rows(m_ref[...], g_ref[...]).astype(BF)
    k = _dot(mm, wk_ref[...])
    for h in range(MEM_HEADS):
        hs = slice(h * MEM_HEAD_DIM, (h + 1) * MEM_HEAD_DIM)
        k_out[:, hs] = _rms_rows(k[:, hs], kn_ref[...]).astype(BF)
    v_out[...] = _dot(mm, wv_ref[...]).astype(BF)


def _memkv(mem, g, wk, wv, kn, layer):
    rows, d = mem.shape
    w = wk.shape[2]
    full = lambda shape: pl.BlockSpec(shape, lambda i: (0, 0))
    wfull = lambda shape: pl.BlockSpec((None,) + shape, lambda i: (layer, 0, 0))
    return pl.pallas_call(
        _memkv_kernel,
        grid=(rows // N_MEM,),
        in_specs=[pl.BlockSpec((N_MEM, d), lambda i: (i, 0)), full((1, d)), wfull((d, w)), wfull((d, w)),
                  full((1, MEM_HEAD_DIM))],
        out_specs=[pl.BlockSpec((N_MEM, w), lambda i: (i, 0))] * 2,
        out_shape=[jax.ShapeDtypeStruct((rows, w), BF)] * 2,
        compiler_params=_cp("parallel"),
        name="mem_kv",
    )(mem, g, wk, wv, kn)


def _memattn_kernel(bounds, *refs):
    x_refs = refs[:len(bounds)]
    k_ref, v_ref, gm_ref, wq_ref, qn_ref, wo_ref, gf_ref, x_out, h_out = refs[len(bounds):]
    x = x_refs[0][...]
    for (lo, _), x_ref in zip(bounds[1:], x_refs[1:]):
        x = jnp.where(pl.program_id(0) >= lo, x_ref[...], x)
    h = _rms_rows(x, gm_ref[...]).astype(BF)
    q = _dot(h, wq_ref[...])
    scale = MEM_HEAD_DIM ** -0.5
    heads = range(MEM_HEADS)
    hsl = [slice(hd * MEM_HEAD_DIM, (hd + 1) * MEM_HEAD_DIM) for hd in heads]
    qh = [_rms_rows(q[:, hsl[hd]], qn_ref[...]).astype(BF) for hd in heads]
    s = [_dot_nt(qh[hd], k_ref[:, hsl[hd]]) * scale for hd in heads]
    p = [jnp.exp(s[hd] - jnp.max(s[hd], axis=-1, keepdims=True)) for hd in heads]
    den = [jnp.sum(p[hd], axis=-1, keepdims=True) for hd in heads]
    pv = [_dot(p[hd].astype(BF), v_ref[:, hsl[hd]]) for hd in heads]
    o = jnp.concatenate([(pv[hd] / den[hd]).astype(BF) for hd in heads], axis=1)
    x2 = x + _dot(o, wo_ref[...])
    x_out[...] = x2
    h_out[...] = _rms_rows(x2, gf_ref[...]).astype(BF)


def _memattn(xs, k, v, gm, wq, qn, wo, gf, layer, segs, tm):
    d = xs[0].shape[1]
    w = wq.shape[2]
    bounds = _tile_bounds(xs, tm)
    nt = bounds[-1][1]
    full = lambda shape: pl.BlockSpec(shape, lambda i: (0, 0))
    wfull = lambda shape: pl.BlockSpec((None,) + shape, lambda i: (layer, 0, 0))
    kv = pl.BlockSpec((N_MEM, w), lambda i: (_batch_of(i * tm, segs), 0))
    row = pl.BlockSpec((tm, d), lambda i: (i, 0))
    parts = [pl.BlockSpec((tm, d), lambda i, lo=lo, hi=hi: (jnp.clip(i - lo, 0, hi - lo - 1), 0)) for lo, hi in bounds]
    return pl.pallas_call(
        functools.partial(_memattn_kernel, bounds),
        grid=(nt,),
        in_specs=parts + [kv, kv, full((1, d)), wfull((d, w)), full((1, MEM_HEAD_DIM)), wfull((w, d)), full((1, d))],
        out_specs=[row, row],
        out_shape=[jax.ShapeDtypeStruct((nt * tm, d), F32), jax.ShapeDtypeStruct((nt * tm, d), BF)],
        compiler_params=_cp("parallel"),
        name="mem_attn",
    )(*xs, k, v, gm, wq, qn, wo, gf)


def _ffn_up_kernel(h_ref, wg_ref, wu_ref, o_ref):
    h = h_ref[...]
    gate = _dot(h, wg_ref[...].astype(BF))
    up = _dot(h, wu_ref[...].astype(BF))
    o_ref[...] = (gate * jax.nn.sigmoid(gate) * up).astype(o_ref.dtype)


def _ffn_up(h, wg, wu, layer, tm, tn):
    m, d = h.shape
    f = wg.shape[2]
    wspec = pl.BlockSpec((None, d, tn), lambda i, j: (layer, 0, j))
    return pl.pallas_call(
        _ffn_up_kernel,
        grid=(m // tm, f // tn),
        in_specs=[pl.BlockSpec((tm, d), lambda i, j: (i, 0)), wspec, wspec],
        out_specs=pl.BlockSpec((tm, tn), lambda i, j: (i, j)),
        out_shape=jax.ShapeDtypeStruct((m, f), BF),
        compiler_params=_cp("parallel", "parallel"),
        name="ffn_up",
    )(h, wg, wu)


def _ffn_down_kernel(a_ref, w_ref, x_ref, o_ref):
    o_ref[...] = x_ref[...] + _dot(a_ref[...], w_ref[...])


def _ffn_down(act, wd, layer, x, tm, tn, row0=0, rows=None):
    f = act.shape[1]
    n = wd.shape[2]
    rows = act.shape[0] if rows is None else rows
    assert row0 % tm == 0 and rows % tm == 0
    t0 = row0 // tm
    return pl.pallas_call(
        _ffn_down_kernel,
        grid=(rows // tm, n // tn),
        in_specs=[pl.BlockSpec((tm, f), lambda i, j: (t0 + i, 0)),
                  pl.BlockSpec((None, f, tn), lambda i, j: (layer, 0, j)),
                  pl.BlockSpec((tm, tn), lambda i, j: (t0 + i, j))],
        out_specs=pl.BlockSpec((tm, tn), lambda i, j: (i, j)),
        out_shape=jax.ShapeDtypeStruct((rows, n), F32),
        compiler_params=_cp("parallel", "parallel"),
        name="ffn_down",
    )(act, wd, x)


def _rel_bucket(rel):
    half = N_REL_BUCKETS // 2
    exact = half // 2
    n = np.abs(rel)
    large = exact + (np.log(np.maximum(n, 1) / exact) / np.log(REL_MAX_DIST / exact)
                     * (half - exact)).astype(np.int32)
    large = np.minimum(large, half - 1)
    return (rel > 0).astype(np.int32) * half + np.where(n < exact, n, large)


def _attn_bias(rel_bias):
    qi = np.arange(BLOCK)[:, None]
    kj = np.arange(3 * BLOCK)[None, :]
    rel = kj - BLOCK - qi
    onehot = (_rel_bucket(rel)[..., None] == np.arange(N_REL_BUCKETS)).astype(np.float32)
    bias = jnp.einsum("qkb,bh->hqk", jnp.asarray(onehot), rel_bias.astype(F32), precision=lax.Precision.HIGHEST)
    return jnp.where(jnp.asarray(np.abs(rel) <= BLOCK)[None], bias, NEG)


def _tri_consts(tt):
    t = np.arange(tt)[:, None]
    s = np.arange(tt)[None, :]
    same = (t // CHUNK) == (s // CHUNK)
    mats = [same & (s <= t), same & (s >= t)]
    return jnp.asarray(np.stack(mats).astype(np.float32), dtype=BF)


def _block_diag_ones(cb):
    i = np.arange(cb)
    return jnp.asarray(((i[:, None] // RWKV_HEAD) == (i[None, :] // RWKV_HEAD)).astype(np.float32), dtype=BF)


def _pick(n, pref):
    t = min(n, pref)
    assert n % t == 0, (n, t)
    return t


def kernel(x_prompt, x_sample, mem_prompt, mem_sample, rel_bias, norm_mix, w_in, q_norm, k_norm, sink,
           shift_prev, shift_next, w0, w2, a0, a2, g2, k_k, k_a, r_k, lnx_w, lnx_b, w_out,
           norm_mem, norm_memkv, wq_mem, wk_mem, wv_mem, wo_mem, qn_mem, kn_mem,
           norm_ffn, w_gate, w_up, w_down):
    b1, t1, d = x_prompt.shape
    b2, t2, _ = x_sample.shape
    segs = ((b1, t1), (b2, t2))
    m1 = b1 * t1
    m = m1 + b2 * t2
    xs = [x_prompt.reshape(m1, d), x_sample.reshape(b2 * t2, d)]
    mem = jnp.concatenate([mem_prompt.reshape(-1, d), mem_sample.reshape(-1, d)], axis=0)
    depth = w_in.shape[0]
    c = k_k.shape[1]
    lora0 = 3 * c

    mg = math.gcd(m1, m - m1)
    tm_in = _pick(mg, 1024)
    tn_in = 768
    tm_out, tn_out = _pick(mg, 1024), 1024
    tt = _pick(min(t1, t2), 256)
    cb = _pick(c, 1024)
    ts = _pick(min(t1, t2), 256)
    cs = _pick(c, 1024)
    tm_mem = _pick(min(t1, t2), 256)
    tm_up, tn_up = _pick(m, 2048), 256
    tm_down, tn_down = _pick(mg, 512), 512

    bias = _attn_bias(rel_bias)
    tri = _tri_consts(tt)
    bd = _block_diag_ones(_pick(c, 512))
    row = lambda a: a.reshape(1, -1).astype(F32)

    w_in_b = _cast_pad(w_in, w_in.shape[2] + XTRA_W - LORA_W - GATE_LORA, _pick(d, 128))
    w_out_b = w_out.astype(BF)
    wq_b, wk_b, wv_b, wo_b = (w.astype(BF) for w in (wq_mem, wk_mem, wv_mem, wo_mem))
    wd_b = w_down.astype(BF)

    for l in range(depth):
        sp, sn = shift_prev[l], shift_next[l]
        padx = lambda a: jnp.pad(a[lora0:], (0, XTRA_W - LORA_W - GATE_LORA)).reshape(1, -1)
        wl = jnp.zeros((4, LORA_W, c), F32)
        wl = wl.at[0, 0:DECAY_LORA].set(w2[l, 0]).at[1, DECAY_LORA:2 * DECAY_LORA].set(w2[l, 1])
        wl = wl.at[2, 2 * DECAY_LORA:2 * DECAY_LORA + AAA_LORA].set(a2[l, 0])
        wl = wl.at[3, 2 * DECAY_LORA + AAA_LORA:].set(a2[l, 1])
        prm = dict(sp_r=row(sp[0:c]), sn_r=row(sn[0:c]), sp_k=row(sp[c:2 * c]), sn_k=row(sn[c:2 * c]),
                   sp_v=row(sp[2 * c:3 * c]), sn_v=row(sn[2 * c:3 * c]), sp_x=padx(sp), sn_x=padx(sn),
                   w0=-LOG2E * w0[l], a0=-LOG2E * a0[l], wl=(-LOG2E * wl).astype(BF), g2=g2[l].astype(BF),
                   k_k=row(k_k[l]), k_a=row(k_a[l]), r_k=row(r_k[l]), bd=bd, tri=tri)

        proj = _inproj(xs, row(norm_mix[l]), w_in_b, l, tm_in, tn_in)
        att = _attention(proj, bias, row(q_norm[l]), row(k_norm[l]), sink[l].astype(F32), segs)
        po = _prep(proj, prm, segs, tt, cb)
        v_b, bv, g = po[10], po[11], po[12]
        y0 = _scan(0, po[0:4], v_b, po[4], segs, ts, cs)
        rw = _scan(1, po[5:9], v_b, po[9], segs, ts, cs,
                   finish=(y0, bv, g, row(lnx_w[l]), row(lnx_b[l]), bd))
        xs = _outproj(att, rw, w_out_b, l, xs, tm_out, tn_out)

        km, vm = _memkv(mem, row(norm_memkv[l]), wk_b, wv_b, row(kn_mem[l]), l)
        x, h3 = _memattn(xs, km, vm, row(norm_mem[l]), wq_b, row(qn_mem[l]), wo_b, row(norm_ffn[l]),
                         l, segs, tm_mem)

        act = _ffn_up(h3, w_gate, w_up, l, tm_up, tn_up)
        if l + 1 < depth:
            xs = [_ffn_down(act, wd_b, l, x, tm_down, tn_down)]
    y1 = _ffn_down(act, wd_b, depth - 1, x, tm_down, tn_down, 0, m1)
    y2 = _ffn_down(act, wd_b, depth - 1, x, tm_down, tn_down, m1, m - m1)
    return (y1.reshape(b1, t1, d), y2.reshape(b2, t2, d))
```
